```python
import math
import jax, jax.numpy as jnp
from jax import lax
import numpy as np

D_MODEL = 1024
BATCH = 8
SEQ = 2048
DEPTH = 1
DEC_BATCH = 32
DEC_SEQ = 16
PAST_LEN = 2048

CHUNK = 64
N_HEADS = 8
N_KV_HEADS = 2
HEAD_DIM = 64
GROUP = N_HEADS // N_KV_HEADS
ATT_WIDTH = N_HEADS * HEAD_DIM
KV_WIDTH = N_KV_HEADS * HEAD_DIM
N_IDX_HEADS = 8
IDX_DIM = 64
MAX_TOPK = 256
Q_BLOCK = 128
N_BUCKETS = 32
MAX_DISTANCE = 128
RW_HEADS = 8
RW_HEAD = 64
RW_WIDTH = RW_HEADS * RW_HEAD
DECAY_LORA = 64
AAA_LORA = 64
GATE_LORA = 128
GN_EPS = 64e-5
D_FF = 4 * D_MODEL
NORM_EPS = 1e-6
NEG = -1e30

RW_SPLITS = (RW_WIDTH, RW_WIDTH, RW_WIDTH, DECAY_LORA, AAA_LORA, GATE_LORA)
RW_COLS = sum(RW_SPLITS)
IN_SPLITS = (ATT_WIDTH, KV_WIDTH, KV_WIDTH, N_IDX_HEADS * IDX_DIM, IDX_DIM, N_IDX_HEADS,
             RW_COLS, D_MODEL, D_MODEL)
IN_COLS = sum(IN_SPLITS)

kernel_name = "dsa_rwkv7_gated_streaming_step"


def split_cols(a, sizes):
    offs = np.cumsum(sizes)[:-1].tolist()
    return jnp.split(a, offs, axis=-1)


def rms_norm(x, g):
    xf = x.astype(jnp.float32)
    y = xf * lax.rsqrt(jnp.mean(xf * xf, axis=-1, keepdims=True) + NORM_EPS)
    return (y * g.astype(jnp.float32)).astype(x.dtype)


def t5_bucket(rel):
    half = N_BUCKETS // 2
    max_exact = half // 2
    ret = jnp.where(rel > 0, half, 0)
    n = jnp.abs(rel)
    nf = jnp.maximum(n, 1).astype(jnp.float32)
    large = max_exact + (jnp.log(nf / max_exact) / math.log(MAX_DISTANCE / max_exact)
                         * (half - max_exact)).astype(jnp.int32)
    large = jnp.minimum(large, half - 1)
    return ret + jnp.where(n < max_exact, n, large)


def dsa_attention(q, q_idx, w_idx, k, v, k_idx, q_pos, k_pos, rel_bias):
    B, T = q.shape[0], q.shape[1]
    L = k.shape[1]
    topk = min(MAX_TOPK, L // 4)
    qb = Q_BLOCK if T % Q_BLOCK == 0 else T
    nb = T // qb
    k_chunk = k_pos // CHUNK

    def blocks(a):
        return jnp.moveaxis(a.reshape((B, nb, qb) + a.shape[2:]), 1, 0)

    def one_block(args):
        q_b, qi_b, wi_b, qp = args
        q_chunk = qp // CHUNK
        s = jnp.einsum('bqhd,bsd->bqhs', qi_b, k_idx,
                       preferred_element_type=jnp.float32) * (IDX_DIM ** -0.5)
        score = jnp.einsum('bqh,bqhs->bqs', wi_b.astype(jnp.float32), jax.nn.relu(s))
        adm = k_chunk[None, :] <= q_chunk[:, None]
        score = jnp.where(adm[None], score, NEG)
        _, idx = lax.top_k(score, topk)
        sel_pos = k_pos[idx]
        valid = (sel_pos // CHUNK) <= q_chunk[None, :, None]
        k_sel = jax.vmap(lambda kb, ib: kb[ib])(k, idx)
        v_sel = jax.vmap(lambda vb, ib: vb[ib])(v, idx)
        qg = q_b.reshape(B, qb, N_KV_HEADS, GROUP, HEAD_DIM)
        logits = jnp.einsum('bqjgd,bqnjd->bqjgn', qg, k_sel,
                            preferred_element_type=jnp.float32) * (HEAD_DIM ** -0.5)
        bias = rel_bias[t5_bucket(sel_pos - qp[None, :, None])]
        bias = jnp.moveaxis(bias.reshape(B, qb, topk, N_KV_HEADS, GROUP), 2, -1)
        logits = jnp.where(valid[:, :, None, None, :], logits + bias.astype(jnp.float32), NEG)
        probs = jax.nn.softmax(logits, axis=-1)
        o = jnp.einsum('bqjgn,bqnjd->bqjgd', probs.astype(v.dtype), v_sel)
        return o.reshape(B, qb, ATT_WIDTH)

    out = lax.map(one_block, (blocks(q), blocks(q_idx), blocks(w_idx), q_pos.reshape(nb, qb)))
    return jnp.moveaxis(out, 0, 1).reshape(B, T, ATT_WIDTH)


def rwkv7_mix(p_rw, shift0, wkv0, p):
    B, T = p_rw.shape[0], p_rw.shape[1]
    f32 = jnp.float32
    prev = jnp.concatenate([shift0.astype(p_rw.dtype), p_rw[:, :-1]], axis=1)
    xs = p_rw + (prev - p_rw) * p['mu_shift']
    r, k, v, wd, ad, gd = split_cols(xs, RW_SPLITS)
    w = -jax.nn.softplus(-(p['w0'] + jnp.tanh(wd) @ p['w_decay_up'])) - 0.5
    a = jax.nn.sigmoid(p['a0'] + ad @ p['w_a_up'])
    g = jax.nn.sigmoid(gd) @ p['w_g_up']

    def heads(t):
        return t.reshape(B, T, RW_HEADS, RW_HEAD).astype(f32)

    kk = heads(k * p['k_k'])
    kk = kk / jnp.maximum(jnp.sqrt(jnp.sum(kk * kk, axis=-1, keepdims=True)), 1e-12)
    a_h = heads(a)
    k_h = heads(k) * (1.0 + (a_h - 1.0) * heads(p['k_a'] * jnp.ones_like(k)))
    r_h, v_h = heads(r), heads(v)
    decay = jnp.exp(-jnp.exp(heads(w)))
    b_h = kk * a_h

    def step(S, inp):
        r_t, d_t, k_t, v_t, kk_t, b_t = inp
        sa = jnp.einsum('bhij,bhj->bhi', S, kk_t)
        S = (S * d_t[:, :, None, :] - sa[..., None] * b_t[:, :, None, :]
             + v_t[..., None] * k_t[:, :, None, :])
        return S, jnp.einsum('bhij,bhj->bhi', S, r_t)

    def tm(t):
        return jnp.moveaxis(t, 1, 0)

    S_T, ys = lax.scan(step, wkv0.astype(f32),
                       (tm(r_h), tm(decay), tm(k_h), tm(v_h), tm(kk), tm(b_h)))
    y = jnp.moveaxis(ys, 0, 1)
    mean = jnp.mean(y, axis=-1, keepdims=True)
    var = jnp.mean(jnp.square(y - mean), axis=-1, keepdims=True)
    yn = ((y - mean) * lax.rsqrt(var + GN_EPS)).reshape(B, T, RW_WIDTH)
    yn = yn * p['lnx_w'].astype(f32) + p['lnx_b'].astype(f32)
    bonus = jnp.sum(r_h * k_h * p['r_k'].astype(f32), axis=-1, keepdims=True) * v_h
    out = (yn + bonus.reshape(B, T, RW_WIDTH)) * g.astype(f32)
    return out.astype(p_rw.dtype), S_T.astype(wkv0.dtype)


def trunk_layer(x, k_hist, v_hist, kidx_hist, wkv0, shift0, rel_bias, p):
    B, T = x.shape[0], x.shape[1]
    past = k_hist.shape[1]
    h = rms_norm(x, p['norm1_g'])
    proj = h @ p['w_in']
    q, k, v, qi, ki, wi, p_rw, ga, gb = split_cols(proj, IN_SPLITS)
    q = q.reshape(B, T, N_HEADS, HEAD_DIM)
    k = k.reshape(B, T, N_KV_HEADS, HEAD_DIM)
    v = v.reshape(B, T, N_KV_HEADS, HEAD_DIM)
    qi = qi.reshape(B, T, N_IDX_HEADS, IDX_DIM)
    wi = wi * (N_IDX_HEADS ** -0.5)
    k_all = jnp.concatenate([k_hist.astype(k.dtype), k], axis=1)
    v_all = jnp.concatenate([v_hist.astype(v.dtype), v], axis=1)
    ki_all = jnp.concatenate([kidx_hist.astype(ki.dtype), ki], axis=1)
    k_pos = jnp.arange(past + T, dtype=jnp.int32)
    q_pos = past + jnp.arange(T, dtype=jnp.int32)
    att = dsa_attention(q, qi, wi, k_all, v_all, ki_all, q_pos, k_pos, rel_bias)
    z_a = att @ p['w_att_out']
    rw, wkv_T = rwkv7_mix(p_rw, shift0, wkv0, p)
    z_b = rw @ p['w_rw_out']
    mixed = jax.nn.sigmoid(ga) * z_a + jax.nn.sigmoid(gb) * z_b
    x = x + mixed @ p['w_out']
    h2 = rms_norm(x, p['norm2_g'])
    x = x + jnp.square(jax.nn.relu(h2 @ p['w_ff1'])) @ p['w_ff2']
    return x, (k, v, ki, wkv_T, p_rw[:, -1:])


def setup_inputs(seed: int = 0) -> dict:
    key = jax.random.key(seed)
    keys = iter(jax.random.split(key, 40))

    def nrm(shape, scale):
        return scale * jax.random.normal(next(keys), shape, jnp.float32)

    def unif(shape, lo, hi):
        return jax.random.uniform(next(keys), shape, jnp.float32, lo, hi)

    D = D_MODEL
    return {
        "x_prompt": nrm((BATCH, SEQ, D), 1.0),
        "x_sample": nrm((DEC_BATCH, DEC_SEQ, D), 1.0),
        "cache_k": nrm((DEPTH, DEC_BATCH, PAST_LEN, N_KV_HEADS, HEAD_DIM), 1.0),
        "cache_v": nrm((DEPTH, DEC_BATCH, PAST_LEN, N_KV_HEADS, HEAD_DIM), 1.0),
        "cache_kidx": nrm((DEPTH, DEC_BATCH, PAST_LEN, IDX_DIM), 1.0),
        "state_wkv": nrm((DEPTH, DEC_BATCH, RW_HEADS, RW_HEAD, RW_HEAD), 0.1),
        "state_shift": nrm((DEPTH, DEC_BATCH, 1, RW_COLS), 1.0),
        "rel_bias": nrm((N_BUCKETS, N_HEADS), 0.5),
        "norm1_g": 1.0 + nrm((DEPTH, D), 0.02),
        "w_in": nrm((DEPTH, D, IN_COLS), D ** -0.5),
        "w_att_out": nrm((DEPTH, ATT_WIDTH, D), ATT_WIDTH ** -0.5),
        "mu_shift": unif((DEPTH, RW_COLS), 0.0, 1.0),
        "w0": unif((DEPTH, RW_WIDTH), -6.0, 0.0),
        "w_decay_up": nrm((DEPTH, DECAY_LORA, RW_WIDTH), 0.1),
        "a0": nrm((DEPTH, RW_WIDTH), 0.1),
        "w_a_up": nrm((DEPTH, AAA_LORA, RW_WIDTH), AAA_LORA ** -0.5),
        "w_g_up": nrm((DEPTH, GATE_LORA, RW_WIDTH), GATE_LORA ** -0.5),
        "k_k": 0.85 + nrm((DEPTH, RW_WIDTH), 0.02),
        "k_a": 1.0 + nrm((DEPTH, RW_WIDTH), 0.02),
        "r_k": nrm((DEPTH, RW_HEADS, RW_HEAD), 0.1),
        "lnx_w": 1.0 + nrm((DEPTH, RW_WIDTH), 0.02),
        "lnx_b": nrm((DEPTH, RW_WIDTH), 0.01),
        "w_rw_out": nrm((DEPTH, RW_WIDTH, D), RW_WIDTH ** -0.5),
        "w_out": nrm((DEPTH, D, D), D ** -0.5),
        "norm2_g": 1.0 + nrm((DEPTH, D), 0.02),
        "w_ff1": nrm((DEPTH, D, D_FF), D ** -0.5),
        "w_ff2": nrm((DEPTH, D_FF, D), D_FF ** -0.5),
        "norm_f_g": 1.0 + nrm((D,), 0.02),
    }


def reference(x_prompt, x_sample, cache_k, cache_v, cache_kidx, state_wkv, state_shift,
              rel_bias, norm1_g, w_in, w_att_out, mu_shift, w0, w_decay_up, a0, w_a_up,
              w_g_up, k_k, k_a, r_k, lnx_w, lnx_b, w_rw_out, w_out, norm2_g, w_ff1, w_ff2,
              norm_f_g):
    xp, xs = x_prompt, x_sample
    Bp = x_prompt.shape[0]
    dt = x_prompt.dtype
    st_p = [[], [], [], [], []]
    st_s = [[], [], [], [], []]
    for l in range(DEPTH):
        p = dict(norm1_g=norm1_g[l], w_in=w_in[l], w_att_out=w_att_out[l], mu_shift=mu_shift[l],
                 w0=w0[l], w_decay_up=w_decay_up[l], a0=a0[l], w_a_up=w_a_up[l],
                 w_g_up=w_g_up[l], k_k=k_k[l], k_a=k_a[l], r_k=r_k[l], lnx_w=lnx_w[l],
                 lnx_b=lnx_b[l], w_rw_out=w_rw_out[l], w_out=w_out[l], norm2_g=norm2_g[l],
                 w_ff1=w_ff1[l], w_ff2=w_ff2[l])
        xp, sp = trunk_layer(
            xp,
            jnp.zeros((Bp, 0, N_KV_HEADS, HEAD_DIM), dt),
            jnp.zeros((Bp, 0, N_KV_HEADS, HEAD_DIM), dt),
            jnp.zeros((Bp, 0, IDX_DIM), dt),
            jnp.zeros((Bp, RW_HEADS, RW_HEAD, RW_HEAD), dt),
            jnp.zeros((Bp, 1, RW_COLS), dt),
            rel_bias, p)
        xs, ss = trunk_layer(xs, cache_k[l], cache_v[l], cache_kidx[l], state_wkv[l],
                             state_shift[l], rel_bias, p)
        for i in range(5):
            st_p[i].append(sp[i])
            st_s[i].append(ss[i])
    y_prompt = rms_norm(xp, norm_f_g)
    y_sample = rms_norm(xs, norm_f_g)
    return (y_prompt, y_sample,
            jnp.stack(st_p[0]), jnp.stack(st_p[1]), jnp.stack(st_p[2]),
            jnp.stack(st_p[3]), jnp.stack(st_p[4]),
            jnp.stack(st_s[0]), jnp.stack(st_s[1]), jnp.stack(st_s[2]),
            jnp.stack(st_s[3]), jnp.stack(st_s[4]))
```

```python
import functools
import math

import jax
import jax.numpy as jnp
from jax import lax
from jax.experimental import pallas as pl
from jax.experimental.pallas import tpu as pltpu

F32 = jnp.float32
BF16 = jnp.bfloat16

D_MODEL = 1024
CHUNK = 64
N_HEADS = 8
N_KV_HEADS = 2
HEAD_DIM = 64
GROUP = N_HEADS // N_KV_HEADS
ATT_WIDTH = N_HEADS * HEAD_DIM
KV_WIDTH = N_KV_HEADS * HEAD_DIM
N_IDX_HEADS = 8
IDX_DIM = 64
MAX_TOPK = 256
Q_BLOCK = 128
N_BUCKETS = 32
RW_HEADS = 8
RW_HEAD = 64
RW_WIDTH = RW_HEADS * RW_HEAD
DECAY_LORA = 64
AAA_LORA = 64
GATE_LORA = 128
RW_COLS = 3 * RW_WIDTH + DECAY_LORA + AAA_LORA + GATE_LORA
GN_EPS = 64e-5
D_FF = 4 * D_MODEL
NORM_EPS = 1e-6
NEG = -1e30

LANES = 128
N_PAIRS = RW_HEADS // 2
VMEM_LIMIT = 56 * 1024 * 1024

HI = lax.Precision.HIGHEST


def _dot_hi(a, b):
    return jnp.dot(a, b, precision=HI, preferred_element_type=F32)


def _dot_hi_nt(a, b):
    return lax.dot_general(a, b, (((1,), (1,)), ((), ())), precision=HI,
                           preferred_element_type=F32)


def _dot_hi_tn(a, b):
    return lax.dot_general(a, b, (((0,), (0,)), ((), ())), precision=HI,
                           preferred_element_type=F32)


def _lane_iota(shape):
    return lax.broadcasted_iota(jnp.int32, shape, len(shape) - 1)


def _row_iota(shape):
    return lax.broadcasted_iota(jnp.int32, shape, len(shape) - 2)


def _bdrows(x, half):
    low = (_lane_iota(x.shape) % (2 * half)) < half
    return jnp.concatenate([jnp.where(low, x, 0.0), jnp.where(low, 0.0, x)], axis=0)


def _rwkv_kernel(x_ref, shift0_ref, h0_ref, mu_ref, w0_ref, a0_ref, kk_ref, ka_ref, rk_ref,
                 lnw_ref, lnb_ref, wlora_ref, wg_ref,
                 out_ref, hT_ref, xbuf, hst, *, C, n_sq):
    c = pl.program_id(1)
    W = RW_WIDTH

    @pl.when(c == 0)
    def _():
        xbuf[7:8, :] = shift0_ref[0]
        hst[...] = h0_ref[0]

    x = x_ref[0]
    xbuf[8:8 + C, :] = x
    prev = xbuf[7:7 + C, :]
    xbuf[7:8, :] = x[C - 1:C, :]
    xs = x + (prev - x) * mu_ref[...]
    r = xs[:, 0:W]
    k = xs[:, W:2 * W]
    v = xs[:, 2 * W:3 * W]
    da = xs[:, 3 * W:3 * W + LANES]
    gd = xs[:, 3 * W + LANES:3 * W + 2 * LANES]

    t_in = jnp.where(_lane_iota(da.shape) < DECAY_LORA, jnp.tanh(da), da)
    lo = _dot_hi(t_in, wlora_ref[...])
    z = -(w0_ref[...] + lo[:, 0:W])
    softplus = jnp.maximum(z, 0.0) + jnp.log1p(jnp.exp(-jnp.abs(z)))
    w = -softplus - 0.5
    ld = -jnp.exp(w)
    a = jax.nn.sigmoid(a0_ref[...] + lo[:, W:2 * W])
    g = _dot_hi(jax.nn.sigmoid(gd), wg_ref[...])

    seg = (_row_iota((LANES, LANES)) // RW_HEAD == _lane_iota((LANES, LANES)) // RW_HEAD).astype(F32)

    def segsum(t):
        return jnp.concatenate(
            [_dot_hi(t[:, p * LANES:(p + 1) * LANES], seg) for p in range(N_PAIRS)], axis=1)

    kk0 = k * kk_ref[...]
    kk = kk0 / jnp.maximum(jnp.sqrt(segsum(kk0 * kk0)), 1e-12)
    k2 = k * (1.0 + (a - 1.0) * ka_ref[...])
    bb = kk * a

    tri = (_lane_iota((C, C)) <= _row_iota((C, C))).astype(F32)
    lp = _dot_hi(tri, ld)
    lpp = lp - ld
    lpc = lp[C - 1:C, :]
    mid = lp[C // 2 - 1:C // 2, :]
    r_abs = r * jnp.exp(lp)
    kap_abs = kk * jnp.exp(lpp)
    r_mid = r * jnp.exp(lp - mid)
    kap_mid = kk * jnp.exp(lpp - mid)
    e_mid = jnp.exp(mid - lp)
    b_mid = bb * e_mid
    k_mid = k2 * e_mid
    e_end = jnp.exp(lpc - lp)
    b_end = bb * e_end
    k_end = k2 * e_end
    p_end = jnp.exp(lpc)

    tt = _row_iota((C, 2 * C))
    ss = _lane_iota((C, 2 * C)) % C
    strict = ss < tt
    incl = ss <= tt
    eye_p = (ss == tt).astype(F32)
    jrow = _row_iota((LANES, LANES))
    jlane = _lane_iota((LANES, LANES))
    same_head = (jrow // RW_HEAD) == (jlane // RW_HEAD)
    eye_j = jrow == jlane

    ys = []
    for p in range(N_PAIRS):
        sl = slice(p * LANES, (p + 1) * LANES)
        lhs = jnp.concatenate([kap_mid[:, sl], r_mid[:, sl]], axis=0)
        rhs = jnp.concatenate([_bdrows(b_mid[:, sl], RW_HEAD), _bdrows(k_mid[:, sl], RW_HEAD)],
                              axis=0)
        A = _dot_hi_nt(lhs, rhs)
        n_kb = jnp.where(strict, A[0:C, 0:2 * C], 0.0)
        a_kk = jnp.where(strict, A[0:C, 2 * C:4 * C], 0.0)
        a_rb = jnp.where(incl, A[C:2 * C, 0:2 * C], 0.0)
        a_rk = jnp.where(incl, A[C:2 * C, 2 * C:4 * C], 0.0)
        T = eye_p - n_kb
        pw = n_kb
        for _ in range(n_sq):
            pw = _dot_hi(pw, _bdrows(pw, C))
            T = T + _dot_hi(T, _bdrows(pw, C))
        av = _dot_hi(jnp.concatenate([a_kk, a_rk], axis=0), _bdrows(v[:, sl], RW_HEAD))
        zz = jnp.concatenate([kap_abs[:, sl], av[0:C]], axis=1)
        wu = _dot_hi(T, _bdrows(zz, RW_HEAD))
        qy = jnp.concatenate([r_abs[:, sl], av[C:2 * C]], axis=1) - _dot_hi(a_rb, _bdrows(wu, RW_HEAD))
        bw = _dot_hi_tn(b_end[:, sl], wu)
        kv = _dot_hi_tn(k_end[:, sl], v[:, sl])
        G = jnp.where(same_head, kv - bw[:, LANES:2 * LANES], 0.0)
        M = jnp.where(same_head, jnp.where(eye_j, p_end[:, sl], 0.0) - bw[:, 0:LANES], 0.0)
        h0 = hst[p]
        ys.append(_dot_hi(qy[:, 0:LANES], h0) + qy[:, LANES:2 * LANES])
        hst[p] = _dot_hi(M, h0) + G
    y = jnp.concatenate(ys, axis=1)

    mean = segsum(y) * (1.0 / RW_HEAD)
    yc = y - mean
    var = segsum(yc * yc) * (1.0 / RW_HEAD)
    yn = yc * lax.rsqrt(var + GN_EPS) * lnw_ref[...] + lnb_ref[...]
    bonus = segsum(r * k2 * rk_ref[...]) * v
    out_ref[0] = ((yn + bonus) * g).astype(out_ref.dtype)

    @pl.when(c == pl.num_programs(1) - 1)
    def _():
        hT_ref[0] = hst[...]


def _rwkv_call(p_rw, shift0, h0_bd, prm, C):
    B, T, _ = p_rw.shape
    n_sq = int(math.log2(C)) - 1
    assert 2 ** (n_sq + 1) == C and T % C == 0
    row = lambda a: a.reshape(1, -1).astype(F32)
    vecs = [row(prm['mu_shift']), row(prm['w0']), row(prm['a0']), row(prm['k_k']), row(prm['k_a']),
            row(prm['r_k']), row(prm['lnx_w']), row(prm['lnx_b'])]
    wlora = jnp.zeros((LANES, 2 * RW_WIDTH), F32)
    wlora = wlora.at[0:DECAY_LORA, 0:RW_WIDTH].set(prm['w_decay_up'])
    wlora = wlora.at[DECAY_LORA:, RW_WIDTH:].set(prm['w_a_up'])
    wg = prm['w_g_up'].astype(F32)
    const = lambda shape: pl.BlockSpec(shape, lambda b, c: (0,) * len(shape))
    kern = functools.partial(_rwkv_kernel, C=C, n_sq=n_sq)
    return pl.pallas_call(
        kern,
        grid=(B, T // C),
        in_specs=[pl.BlockSpec((1, C, RW_COLS), lambda b, c: (b, c, 0)),
                  pl.BlockSpec((1, 1, RW_COLS), lambda b, c: (b, 0, 0)),
                  pl.BlockSpec((1, N_PAIRS, LANES, LANES), lambda b, c: (b, 0, 0, 0)),
                  const((1, RW_COLS))] + [const((1, RW_WIDTH))] * 7
                 + [const((LANES, 2 * RW_WIDTH)), const((GATE_LORA, RW_WIDTH))],
        out_specs=[pl.BlockSpec((1, C, RW_WIDTH), lambda b, c: (b, c, 0)),
                   pl.BlockSpec((1, N_PAIRS, LANES, LANES), lambda b, c: (b, 0, 0, 0))],
        out_shape=[jax.ShapeDtypeStruct((B, T, RW_WIDTH), BF16),
                   jax.ShapeDtypeStruct((B, N_PAIRS, LANES, LANES), F32)],
        scratch_shapes=[pltpu.VMEM((8 + C, RW_COLS), F32),
                        pltpu.VMEM((N_PAIRS, LANES, LANES), F32)],
        compiler_params=pltpu.CompilerParams(dimension_semantics=("arbitrary", "arbitrary"),
                                             vmem_limit_bytes=VMEM_LIMIT),
        name="rwkv7_chunked",
    )(p_rw, shift0, h0_bd, *vecs, wlora, wg)


def _state_to_bd(wkv):
    B = wkv.shape[0]
    h = jnp.swapaxes(wkv, -1, -2).reshape(B, N_PAIRS, 2, RW_HEAD, RW_HEAD)
    z = jnp.zeros_like(h[:, :, 0])
    top = jnp.concatenate([h[:, :, 0], z], axis=-1)
    bot = jnp.concatenate([z, h[:, :, 1]], axis=-1)
    return jnp.concatenate([top, bot], axis=-2)


def _bd_to_state(hbd):
    B = hbd.shape[0]
    h0 = hbd[:, :, 0:RW_HEAD, 0:RW_HEAD]
    h1 = hbd[:, :, RW_HEAD:, RW_HEAD:]
    h = jnp.stack([h0, h1], axis=2).reshape(B, RW_HEADS, RW_HEAD, RW_HEAD)
    return jnp.swapaxes(h, -1, -2)


_INT_MIN = -2 ** 31
_T5_STEPS = ((91, 14), (64, 13), (46, 12), (32, 11), (23, 10), (16, 9), (12, 8), (8, 7), (7, 6),
             (6, 5), (5, 4), (4, 3), (3, 2), (2, 1), (1, 0))
_T5_FAR = 15


def _half_masked(x, half):
    low = (_lane_iota(x.shape) % LANES) < HEAD_DIM
    keep = low if half == 0 else jnp.logical_not(low)
    return jnp.where(keep, x, jnp.zeros_like(x))


def _dup_half(x, half):
    xr = pltpu.roll(x, HEAD_DIM, axis=1)
    low = _lane_iota(x.shape) < HEAD_DIM
    return jnp.where(low, x, xr) if half == 0 else jnp.where(low, xr, x)


def _dot_nt(a, b):
    return lax.dot_general(a, b, (((1,), (1,)), ((), ())), preferred_element_type=F32)


def _dsa_kernel(*refs, Tq, P, Tn, Lp, topk):
    if P > 0:
        (q_ref, qi_ref, wi_ref, hk_ref, hv_ref, hki_ref, nk_ref, nv_ref, nki_ref, rb_ref,
         att_ref, kd, vd, kid, lg, tbl) = refs
    else:
        (q_ref, qi_ref, wi_ref, nk_ref, nv_ref, nki_ref, rb_ref,
         att_ref, kd, vd, kid, lg, tbl) = refs
    b = pl.program_id(0)
    i = pl.program_id(1)
    L = P + Tn

    @pl.when(jnp.logical_and(b == 0, i == 0))
    def _():
        shape = (Tq, 2 * LANES)
        rel = _lane_iota(shape) - LANES - _row_iota(shape)
        n = jnp.abs(rel)
        for h in range(N_HEADS):
            far = rb_ref[_T5_FAR, h]
            neg = jnp.full(shape, far, F32)
            pos = jnp.full(shape, rb_ref[_T5_FAR + N_BUCKETS // 2, h], F32)
            for bound, bucket in _T5_STEPS:
                neg = jnp.where(n < bound, rb_ref[bucket, h], neg)
                pos = jnp.where(n < bound, rb_ref[bucket + N_BUCKETS // 2, h], pos)
            tbl[h] = jnp.where(rel > 0, pos, neg) - far

    @pl.when(i == 0)
    def _():
        def stage(k_src, v_src, ki_src, r0, n):
            kf = k_src[0]
            vf = v_src[0]
            kif = ki_src[0]
            for j in range(N_KV_HEADS):
                kd[j, r0:r0 + n, :] = _dup_half(kf, j).astype(BF16)
                vd[j, r0:r0 + n, :] = _dup_half(vf, j).astype(BF16)
            kid[r0:r0 + n, :] = jnp.concatenate([kif, kif], axis=1).astype(BF16)

        if P > 0:
            stage(hk_ref, hv_ref, hki_ref, 0, P)
        stage(nk_ref, nv_ref, nki_ref, P, Tn)
        if Lp > L:
            z = jnp.zeros((Lp - L, LANES), BF16)
            for j in range(N_KV_HEADS):
                kd[j, L:Lp, :] = z
                vd[j, L:Lp, :] = z
            kid[L:Lp, :] = z

    q = q_ref[0]
    qi = qi_ref[0]
    wi = wi_ref[0]

    kidv = kid[...]
    score = jnp.zeros((Tq, Lp), F32)
    for h in range(N_IDX_HEADS):
        pr = h // 2
        lhs = _half_masked(qi[:, pr * LANES:(pr + 1) * LANES], h % 2)
        s_h = _dot_nt(lhs, kidv)
        score = score + wi[:, h:h + 1] * jnp.maximum(s_h, 0.0)

    s_pos = _lane_iota((Tq, Lp))
    q_pos = P + i * Tq + _row_iota((Tq, Lp))
    adm = jnp.logical_and(s_pos // CHUNK <= q_pos // CHUNK, s_pos < L)
    score = jnp.where(adm, score, NEG) + 0.0
    bits = pltpu.bitcast(score, jnp.int32)
    key = jnp.where(bits < 0, bits ^ jnp.int32(0x7FFFFFFF), bits)

    def count(mask):
        return jnp.sum(jnp.where(mask, 1.0, 0.0), axis=1, keepdims=True)

    kf = float(topk)
    ans = jnp.where(count(key >= 0) >= kf, jnp.int32(0), jnp.int32(_INT_MIN))

    def bit_step(it, ans):
        cand = ans | (jnp.int32(1) << (30 - it))
        return jnp.where(count(key >= cand) >= kf, cand, ans)

    thr = lax.fori_loop(0, 31, bit_step, ans)
    gt = key > thr
    eq = key == thr
    need = kf - count(gt)

    n_idx_bits = (Lp - 1).bit_length()

    def idx_step(it, x):
        cand = x | (jnp.int32(1) << (n_idx_bits - 1 - it))
        return jnp.where(count(jnp.logical_and(eq, s_pos < cand)) < need, cand, x)

    last = lax.fori_loop(0, n_idx_bits, idx_step, jnp.zeros((Tq, 1), jnp.int32))
    sel = jnp.logical_and(jnp.logical_or(gt, jnp.logical_and(eq, s_pos <= last)), adm)
    sel4 = jnp.concatenate([sel] * GROUP, axis=0)

    blk_end = (P + i * Tq) // LANES + 1
    for j in range(N_KV_HEADS):
        rows = []
        for g in range(GROUP):
            h = j * GROUP + g
            rows.append(_half_masked(q[:, (h // 2) * LANES:(h // 2 + 1) * LANES], h % 2))
        lhs = jnp.concatenate(rows, axis=0)
        lg[...] = _dot_nt(lhs, kd[j])

        @pl.when(blk_end >= 2)
        def _():
            start = pl.multiple_of((blk_end - 2) * LANES, LANES)
            for g in range(GROUP):
                r = slice(g * Tq, (g + 1) * Tq)
                lg[r, pl.ds(start, 2 * LANES)] = lg[r, pl.ds(start, 2 * LANES)] + tbl[j * GROUP + g]

        @pl.when(blk_end < 2)
        def _():
            for g in range(GROUP):
                r = slice(g * Tq, (g + 1) * Tq)
                lg[r, 0:LANES] = lg[r, 0:LANES] + tbl[j * GROUP + g][:, LANES:2 * LANES]

        logits = jnp.where(sel4, lg[...], NEG)
        m = jnp.max(logits, axis=1, keepdims=True)
        p = jnp.exp(logits - m)
        l = jnp.sum(p, axis=1, keepdims=True)
        o = jnp.dot(p.astype(BF16), vd[j], preferred_element_type=F32) / l
        low = _lane_iota((Tq, LANES)) < HEAD_DIM
        for pp in range(GROUP // 2):
            o_even = o[(2 * pp) * Tq:(2 * pp + 1) * Tq]
            o_odd = o[(2 * pp + 1) * Tq:(2 * pp + 2) * Tq]
            c0 = (j * GROUP // 2 + pp) * LANES
            att_ref[0, :, c0:c0 + LANES] = jnp.where(low, o_even, o_odd).astype(att_ref.dtype)


def _dsa_call(q, qi, wi, hist, new, rel_bias, Tq):
    B, T, _ = q.shape
    Tn = new[0].shape[1]
    P = 0 if hist is None else hist[0].shape[1]
    L = P + Tn
    Lp = -(-L // LANES) * LANES
    topk = min(MAX_TOPK, L // 4)
    assert T % Tq == 0 and Tn == T and P % LANES == 0 and (Tq % LANES == 0 or T == Tq)
    qspec = lambda w: pl.BlockSpec((1, Tq, w), lambda b, i: (b, i, 0))
    full = lambda n, w: pl.BlockSpec((1, n, w), lambda b, i: (b, 0, 0))
    in_specs = [qspec(ATT_WIDTH), qspec(N_IDX_HEADS * IDX_DIM), qspec(N_IDX_HEADS)]
    args = [q, qi, wi]
    if P > 0:
        in_specs += [full(P, KV_WIDTH), full(P, KV_WIDTH), full(P, IDX_DIM)]
        args += list(hist)
    in_specs += [full(Tn, KV_WIDTH), full(Tn, KV_WIDTH), full(Tn, IDX_DIM),
                 pl.BlockSpec(memory_space=pltpu.SMEM)]
    args += list(new) + [rel_bias.astype(F32)]
    kern = functools.partial(_dsa_kernel, Tq=Tq, P=P, Tn=Tn, Lp=Lp, topk=topk)
    return pl.pallas_call(
        kern,
        grid=(B, T // Tq),
        in_specs=in_specs,
        out_specs=pl.BlockSpec((1, Tq, ATT_WIDTH), lambda b, i: (b, i, 0)),
        out_shape=jax.ShapeDtypeStruct((B, T, ATT_WIDTH), BF16),
        scratch_shapes=[pltpu.VMEM((N_KV_HEADS, Lp, LANES), BF16),
                        pltpu.VMEM((N_KV_HEADS, Lp, LANES), BF16),
                        pltpu.VMEM((Lp, LANES), BF16),
                        pltpu.VMEM((GROUP * Tq, Lp), F32),
                        pltpu.VMEM((N_HEADS, Tq, 2 * LANES), F32)],
        compiler_params=pltpu.CompilerParams(dimension_semantics=("arbitrary", "arbitrary"),
                                             vmem_limit_bytes=VMEM_LIMIT),
        name="dsa_attention",
    )(*args)


ATT_COLS = 2 * ATT_WIDTH + 2 * KV_WIDTH + LANES
_C_Q, _C_QI, _C_K, _C_V, _C_KI = 0, 512, 1024, 1152, 1280
_C_WI = _C_KI + IDX_DIM


def _rms(x, g):
    return x * lax.rsqrt(jnp.mean(x * x, axis=-1, keepdims=True) + NORM_EPS) * g


def _inproj_kernel(x_ref, g_ref, wa_ref, wr_ref, q_ref, qi_ref, k_ref, v_ref, ki_ref, wi_ref, prw_ref):
    h = _rms(x_ref[...], g_ref[...]).astype(BF16)
    pa = jnp.dot(h, wa_ref[...], preferred_element_type=F32)
    q_ref[...] = (pa[:, _C_Q:_C_Q + ATT_WIDTH] * (HEAD_DIM ** -0.5)).astype(BF16)
    qi_ref[...] = (pa[:, _C_QI:_C_QI + ATT_WIDTH] * (IDX_DIM ** -0.5)).astype(BF16)
    k_ref[...] = pa[:, _C_K:_C_K + KV_WIDTH]
    v_ref[...] = pa[:, _C_V:_C_V + KV_WIDTH]
    ki_ref[...] = pa[:, _C_KI:_C_KI + IDX_DIM]
    wi_ref[...] = pa[:, _C_WI:_C_WI + N_IDX_HEADS] * (N_IDX_HEADS ** -0.5)
    prw_ref[...] = jnp.dot(h, wr_ref[...], preferred_element_type=F32)


def _const_spec(shape):
    return pl.BlockSpec(shape, lambda i: (0,) * len(shape), pipeline_mode=pl.Buffered(1))


def _inproj_call(x2d, g, wa, wr, tm):
    N = x2d.shape[0]
    assert N % tm == 0
    rows = lambda w: pl.BlockSpec((tm, w), lambda i: (i, 0))
    widths = [ATT_WIDTH, ATT_WIDTH, KV_WIDTH, KV_WIDTH, IDX_DIM, N_IDX_HEADS, RW_COLS]
    dtypes = [BF16, BF16, F32, F32, F32, F32, F32]
    return pl.pallas_call(
        _inproj_kernel,
        grid=(N // tm,),
        in_specs=[rows(D_MODEL), _const_spec((1, D_MODEL)), _const_spec((D_MODEL, ATT_COLS)),
                  _const_spec((D_MODEL, RW_COLS))],
        out_specs=[rows(w) for w in widths],
        out_shape=[jax.ShapeDtypeStruct((N, w), d) for w, d in zip(widths, dtypes)],
        compiler_params=pltpu.CompilerParams(dimension_semantics=("arbitrary",),
                                             vmem_limit_bytes=VMEM_LIMIT),
        name="rmsnorm_inproj",
    )(x2d, g, wa, wr)


def _tail_kernel(x_ref, att_ref, rw_ref, g1_ref, wgate_ref, wao_ref, wro_ref, wout_ref,
                 g2_ref, wff1_ref, wff2_ref, gf_ref, y_ref):
    x = x_ref[...]
    h = _rms(x, g1_ref[...]).astype(BF16)
    gates = jax.nn.sigmoid(jnp.dot(h, wgate_ref[...], preferred_element_type=F32))
    z_a = jnp.dot(att_ref[...], wao_ref[...], preferred_element_type=F32)
    z_b = jnp.dot(rw_ref[...], wro_ref[...], preferred_element_type=F32)
    mixed = gates[:, 0:D_MODEL] * z_a + gates[:, D_MODEL:2 * D_MODEL] * z_b
    x1 = x + jnp.dot(mixed.astype(BF16), wout_ref[...], preferred_element_type=F32)
    h2 = _rms(x1, g2_ref[...]).astype(BF16)
    f = jnp.maximum(jnp.dot(h2, wff1_ref[...], preferred_element_type=F32), 0.0)
    x2 = x1 + jnp.dot((f * f).astype(BF16), wff2_ref[...], preferred_element_type=F32)
    y_ref[...] = _rms(x2, gf_ref[...])


def _tail_call(x2d, att, rw, g1, wgate, wao, wro, wout, g2, wff1, wff2, gf, tm):
    N = x2d.shape[0]
    assert N % tm == 0
    rows = lambda w: pl.BlockSpec((tm, w), lambda i: (i, 0))
    return pl.pallas_call(
        _tail_kernel,
        grid=(N // tm,),
        in_specs=[rows(D_MODEL), rows(ATT_WIDTH), rows(RW_WIDTH), _const_spec((1, D_MODEL)),
                  _const_spec((D_MODEL, 2 * D_MODEL)), _const_spec((ATT_WIDTH, D_MODEL)),
                  _const_spec((RW_WIDTH, D_MODEL)), _const_spec((D_MODEL, D_MODEL)),
                  _const_spec((1, D_MODEL)), _const_spec((D_MODEL, D_FF)), _const_spec((D_FF, D_MODEL)),
                  _const_spec((1, D_MODEL))],
        out_specs=rows(D_MODEL),
        out_shape=jax.ShapeDtypeStruct((N, D_MODEL), F32),
        compiler_params=pltpu.CompilerParams(dimension_semantics=("arbitrary",),
                                             vmem_limit_bytes=VMEM_LIMIT),
        name="merge_out_mlp",
    )(x2d, att, rw, g1, wgate, wao, wro, wout, g2, wff1, wff2, gf)


_IN_SPLITS = (ATT_WIDTH, KV_WIDTH, KV_WIDTH, N_IDX_HEADS * IDX_DIM, IDX_DIM, N_IDX_HEADS,
              RW_COLS, D_MODEL, D_MODEL)


def _split_w_in(w_in):
    offs = [0]
    for s in _IN_SPLITS:
        offs.append(offs[-1] + s)
    col = lambda n: w_in[:, offs[n]:offs[n + 1]]
    pad = jnp.zeros((D_MODEL, ATT_COLS - _C_WI - N_IDX_HEADS), w_in.dtype)
    wa = jnp.concatenate([col(0), col(3), col(1), col(2), col(4), col(5), pad], axis=1).astype(BF16)
    wr = col(6).astype(BF16)
    wgate = jnp.concatenate([col(7), col(8)], axis=1).astype(BF16)
    return wa, wr, wgate


def _branch(x, hist, wkv0, shift0, rel_bias, prm, w, tm_in, tm_tail, Tq, C):
    B, T, _ = x.shape
    x2d = x.reshape(B * T, D_MODEL)
    q, qi, k, v, ki, wi, p_rw = _inproj_call(x2d, prm['g1'], w['wa'], w['wr'], tm_in)
    r3 = lambda a: a.reshape(B, T, a.shape[-1])
    rw, hT = _rwkv_call(r3(p_rw), shift0, _state_to_bd(wkv0), prm, C)
    att = _dsa_call(r3(q), r3(qi), r3(wi), hist, (r3(k), r3(v), r3(ki)), rel_bias, Tq)
    y = _tail_call(x2d, att.reshape(B * T, ATT_WIDTH), rw.reshape(B * T, RW_WIDTH), prm['g1'],
                   w['wgate'], w['wao'], w['wro'], w['wout'], prm['g2'], w['wff1'], w['wff2'],
                   prm['gf'], tm_tail)
    new_k = k.reshape(1, B, T, N_KV_HEADS, HEAD_DIM)
    new_v = v.reshape(1, B, T, N_KV_HEADS, HEAD_DIM)
    new_ki = ki.reshape(1, B, T, IDX_DIM)
    wkv = _bd_to_state(hT)[None]
    shift = r3(p_rw)[:, T - 1:T, :][None]
    return y.reshape(B, T, D_MODEL), new_k, new_v, new_ki, wkv, shift


def kernel(x_prompt, x_sample, cache_k, cache_v, cache_kidx, state_wkv, state_shift, rel_bias,
           norm1_g, w_in, w_att_out, mu_shift, w0, w_decay_up, a0, w_a_up, w_g_up, k_k, k_a, r_k,
           lnx_w, lnx_b, w_rw_out, w_out, norm2_g, w_ff1, w_ff2, norm_f_g):
    assert w_in.shape[0] == 1, "single-layer step"
    wa, wr, wgate = _split_w_in(w_in[0])
    w = dict(wa=wa, wr=wr, wgate=wgate, wao=w_att_out[0].astype(BF16), wro=w_rw_out[0].astype(BF16),
             wout=w_out[0].astype(BF16), wff1=w_ff1[0].astype(BF16), wff2=w_ff2[0].astype(BF16))
    prm = dict(mu_shift=mu_shift[0], w0=w0[0], w_decay_up=w_decay_up[0], a0=a0[0], w_a_up=w_a_up[0],
               w_g_up=w_g_up[0], k_k=k_k[0], k_a=k_a[0], r_k=r_k[0].reshape(-1), lnx_w=lnx_w[0],
               lnx_b=lnx_b[0], g1=norm1_g[0].reshape(1, -1), g2=norm2_g[0].reshape(1, -1),
               gf=norm_f_g.reshape(1, -1))
    Bp, Tp, _ = x_prompt.shape
    Bs, Ts, _ = x_sample.shape
    past = cache_k.shape[2]
    zero_state = jnp.zeros((Bp, RW_HEADS, RW_HEAD, RW_HEAD), F32)
    zero_shift = jnp.zeros((Bp, 1, RW_COLS), F32)
    outs_p = _branch(x_prompt, None, zero_state, zero_shift, rel_bias, prm, w,
                     tm_in=512, tm_tail=256, Tq=Q_BLOCK, C=CHUNK)
    hist = (cache_k[0].reshape(Bs, past, KV_WIDTH), cache_v[0].reshape(Bs, past, KV_WIDTH),
            cache_kidx[0])
    outs_s = _branch(x_sample, hist, state_wkv[0], state_shift[0], rel_bias, prm, w,
                     tm_in=Bs * Ts, tm_tail=256, Tq=Ts, C=Ts)
    return (outs_p[0], outs_s[0]) + outs_p[1:] + outs_s[1:]
```

```python
import functools
import math

import jax
import jax.numpy as jnp
from jax import lax
from jax.experimental import pallas as pl
from jax.experimental.pallas import tpu as pltpu

F32 = jnp.float32
BF16 = jnp.bfloat16

D_MODEL = 1024
CHUNK = 64
N_HEADS = 8
N_KV_HEADS = 2
HEAD_DIM = 64
GROUP = N_HEADS // N_KV_HEADS
ATT_WIDTH = N_HEADS * HEAD_DIM
KV_WIDTH = N_KV_HEADS * HEAD_DIM
N_IDX_HEADS = 8
IDX_DIM = 64
MAX_TOPK = 256
Q_BLOCK = 128
N_BUCKETS = 32
RW_HEADS = 8
RW_HEAD = 64
RW_WIDTH = RW_HEADS * RW_HEAD
DECAY_LORA = 64
AAA_LORA = 64
GATE_LORA = 128
RW_COLS = 3 * RW_WIDTH + DECAY_LORA + AAA_LORA + GATE_LORA
GN_EPS = 64e-5
D_FF = 4 * D_MODEL
NORM_EPS = 1e-6
NEG = -1e30

LANES = 128
N_PAIRS = RW_HEADS // 2
VMEM_LIMIT = 56 * 1024 * 1024

CHUNK_PIECES = 1
SEG_PIECES = 2

_NN, _NT, _TN = (1, 0), (1, 1), (0, 0)


def _split(x, n):
    pieces = []
    rem = x
    for i in range(n):
        p = rem.astype(BF16)
        pieces.append(p)
        if i + 1 < n:
            rem = rem - p.astype(F32)
    return tuple(pieces)


def _mm(a, b, dims=_NN):
    n = max(len(a), len(b))
    terms = [(i, j) for i in range(len(a)) for j in range(len(b)) if i + j < n]
    ca, cb = dims
    lhs = jnp.concatenate([a[i] for i, _ in terms], axis=ca) if len(terms) > 1 else a[0]
    rhs = jnp.concatenate([b[j] for _, j in terms], axis=cb) if len(terms) > 1 else b[0]
    return lax.dot_general(lhs, rhs, (((ca,), (cb,)), ((), ())), preferred_element_type=F32)


def _lane_iota(shape):
    return lax.broadcasted_iota(jnp.int32, shape, len(shape) - 1)


def _row_iota(shape):
    return lax.broadcasted_iota(jnp.int32, shape, len(shape) - 2)


def _bdrows(x, half):
    low = (_lane_iota(x.shape) % (2 * half)) < half
    return jnp.concatenate([jnp.where(low, x, 0.0), jnp.where(low, 0.0, x)], axis=0)


def _rwkv_kernel(x_ref, shift0_ref, h0_ref, mu_ref, w0_ref, a0_ref, kk_ref, ka_ref, rk_ref,
                 lnw_ref, lnb_ref, wlora_ref, wg_ref,
                 out_ref, hT_ref, xbuf, hst, *, C, n_sq, nb):
    c = pl.program_id(1)

    @pl.when(c == 0)
    def _():
        for n in range(nb):
            xbuf[n, 7:8, :] = shift0_ref[n]
        hst[...] = h0_ref[...]

    W = RW_WIDTH
    R = nb * C
    prevs = []
    for n in range(nb):
        xn = x_ref[n]
        xbuf[n, 8:8 + C, :] = xn
        prevs.append(xbuf[n, 7:7 + C, :])
        xbuf[n, 7:8, :] = xn[C - 1:C, :]
    x = x_ref[...].reshape(R, RW_COLS)
    prev = jnp.concatenate(prevs, axis=0)
    xs = x + (prev - x) * mu_ref[...]
    r = xs[:, 0:W]
    k = xs[:, W:2 * W]
    v = xs[:, 2 * W:3 * W]
    da = xs[:, 3 * W:3 * W + LANES]
    gd = xs[:, 3 * W + LANES:3 * W + 2 * LANES]

    t_in = jnp.where(_lane_iota(da.shape) < DECAY_LORA, jnp.tanh(da), da)
    lo = jnp.dot(t_in.astype(BF16), wlora_ref[...], preferred_element_type=F32)
    z = -(w0_ref[...] + lo[:, 0:W])
    softplus = jnp.maximum(z, 0.0) + jnp.log1p(jnp.exp(-jnp.abs(z)))
    w = -softplus - 0.5
    ld = -jnp.exp(w)
    a = jax.nn.sigmoid(a0_ref[...] + lo[:, W:2 * W])
    g = jnp.dot(jax.nn.sigmoid(gd).astype(BF16), wg_ref[...], preferred_element_type=F32)

    seg = (_row_iota((LANES, LANES)) // RW_HEAD == _lane_iota((LANES, LANES)) // RW_HEAD).astype(BF16)

    def segsum(t):
        return jnp.concatenate(
            [_mm(_split(t[:, p * LANES:(p + 1) * LANES], SEG_PIECES), (seg,)) for p in range(N_PAIRS)],
            axis=1)

    kk0 = k * kk_ref[...]
    kk = kk0 / jnp.maximum(jnp.sqrt(segsum(kk0 * kk0)), 1e-12)
    k2 = k * (1.0 + (a - 1.0) * ka_ref[...])
    bb = kk * a

    ti, si = _row_iota((R, R)), _lane_iota((R, R))
    tri = jnp.logical_and(si <= ti, si // C == ti // C).astype(BF16)
    lp = _mm((tri,), _split(ld, 3))
    lpp = lp - ld
    rows_of = lambda i: jnp.concatenate(
        [jnp.broadcast_to(lp[n * C + i:n * C + i + 1, :], (C, W)) for n in range(nb)], axis=0)
    lpc = rows_of(C - 1)
    mid = rows_of(C // 2 - 1)
    r_abs = r * jnp.exp(lp)
    kap_abs = kk * jnp.exp(lpp)
    r_mid = r * jnp.exp(lp - mid)
    kap_mid = kk * jnp.exp(lpp - mid)
    e_mid = jnp.exp(mid - lp)
    b_mid = bb * e_mid
    k_mid = k2 * e_mid
    e_end = jnp.exp(lpc - lp)
    b_end = bb * e_end
    k_end = k2 * e_end
    p_end = jnp.exp(lpc)

    tt = _row_iota((C, 2 * C))
    ss = _lane_iota((C, 2 * C)) % C
    strict = ss < tt
    incl = ss <= tt
    eye_p = (ss == tt).astype(F32)
    jrow = _row_iota((LANES, LANES))
    jlane = _lane_iota((LANES, LANES))
    same_head = (jrow // RW_HEAD) == (jlane // RW_HEAD)
    eye_j = jrow == jlane

    sp = lambda t: _split(t, CHUNK_PIECES)
    bd = lambda pieces, half: tuple(_bdrows(t, half) for t in pieces)

    chains = [(n, p) for n in range(nb) for p in range(N_PAIRS)]
    blk = lambda t, ch: t[ch[0] * C:(ch[0] + 1) * C, ch[1] * LANES:(ch[1] + 1) * LANES]
    each = lambda f: [f(i, ch) for i, ch in enumerate(chains)]

    A = each(lambda i, ch: _mm(
        sp(jnp.concatenate([blk(kap_mid, ch), blk(r_mid, ch)], axis=0)),
        sp(jnp.concatenate([_bdrows(blk(b_mid, ch), RW_HEAD), _bdrows(blk(k_mid, ch), RW_HEAD)],
                           axis=0)), _NT))
    n_kb = each(lambda i, ch: jnp.where(strict, A[i][0:C, 0:2 * C], 0.0))
    a_kk = each(lambda i, ch: jnp.where(strict, A[i][0:C, 2 * C:4 * C], 0.0))
    a_rb = each(lambda i, ch: jnp.where(incl, A[i][C:2 * C, 0:2 * C], 0.0))
    a_rk = each(lambda i, ch: jnp.where(incl, A[i][C:2 * C, 2 * C:4 * C], 0.0))
    T = each(lambda i, ch: eye_p - n_kb[i])
    pw = each(lambda i, ch: sp(n_kb[i]))
    for _ in range(n_sq):
        pw = each(lambda i, ch: sp(_mm(pw[i], bd(pw[i], C))))
        T = each(lambda i, ch: T[i] + _mm(sp(T[i]), bd(pw[i], C)))
    v_p = each(lambda i, ch: sp(blk(v, ch)))
    av = each(lambda i, ch: _mm(sp(jnp.concatenate([a_kk[i], a_rk[i]], axis=0)),
                                bd(v_p[i], RW_HEAD)))
    wu = each(lambda i, ch: sp(_mm(
        sp(T[i]), bd(sp(jnp.concatenate([blk(kap_abs, ch), av[i][0:C]], axis=1)), RW_HEAD))))
    qy = each(lambda i, ch: jnp.concatenate([blk(r_abs, ch), av[i][C:2 * C]], axis=1)
              - _mm(sp(a_rb[i]), bd(wu[i], RW_HEAD)))
    bw = each(lambda i, ch: _mm(sp(blk(b_end, ch)), wu[i], _TN))
    kv = each(lambda i, ch: _mm(_split(blk(k_end, ch), 2), _split(blk(v, ch), 2), _TN))
    G = each(lambda i, ch: jnp.where(same_head, kv[i] - bw[i][:, LANES:2 * LANES], 0.0))
    M = each(lambda i, ch: jnp.where(same_head, bw[i][:, 0:LANES], 0.0))
    p_col = each(lambda i, ch: jnp.sum(jnp.where(eye_j, blk(p_end, ch)[0:1], 0.0), axis=1,
                                       keepdims=True))
    h0f = each(lambda i, ch: hst[ch[0], ch[1]])
    h0 = each(lambda i, ch: sp(h0f[i]))
    yc_ = each(lambda i, ch: _mm(sp(qy[i][:, 0:LANES]), h0[i]) + qy[i][:, LANES:2 * LANES])
    for i, ch in enumerate(chains):
        hst[ch[0], ch[1]] = p_col[i] * h0f[i] - _mm(sp(M[i]), h0[i]) + G[i]
    y = jnp.concatenate([jnp.concatenate(yc_[n * N_PAIRS:(n + 1) * N_PAIRS], axis=1)
                         for n in range(nb)], axis=0)

    mean = segsum(y) * (1.0 / RW_HEAD)
    yc = y - mean
    var = segsum(yc * yc) * (1.0 / RW_HEAD)
    yn = yc * lax.rsqrt(var + GN_EPS) * lnw_ref[...] + lnb_ref[...]
    bonus = segsum(r * k2 * rk_ref[...]) * v
    out_ref[...] = ((yn + bonus) * g).astype(out_ref.dtype).reshape(nb, C, W)

    @pl.when(c == pl.num_programs(1) - 1)
    def _():
        hT_ref[...] = hst[...]


def _rwkv_call(p_rw, shift0, h0_bd, prm, C, nb):
    B, T, _ = p_rw.shape
    n_sq = int(math.log2(C)) - 1
    assert 2 ** (n_sq + 1) == C and T % C == 0 and B % nb == 0
    row = lambda a: a.reshape(1, -1).astype(F32)
    vecs = [row(prm['mu_shift']), row(prm['w0']), row(prm['a0']), row(prm['k_k']), row(prm['k_a']),
            row(prm['r_k']), row(prm['lnx_w']), row(prm['lnx_b'])]
    wlora = jnp.zeros((LANES, 2 * RW_WIDTH), F32)
    wlora = wlora.at[0:DECAY_LORA, 0:RW_WIDTH].set(prm['w_decay_up'])
    wlora = wlora.at[DECAY_LORA:, RW_WIDTH:].set(prm['w_a_up']).astype(BF16)
    wg = prm['w_g_up'].astype(BF16)
    const = lambda shape: pl.BlockSpec(shape, lambda b, c: (0,) * len(shape))
    kern = functools.partial(_rwkv_kernel, C=C, n_sq=n_sq, nb=nb)
    return pl.pallas_call(
        kern,
        grid=(B // nb, T // C),
        in_specs=[pl.BlockSpec((nb, C, RW_COLS), lambda b, c: (b, c, 0)),
                  pl.BlockSpec((nb, 1, RW_COLS), lambda b, c: (b, 0, 0)),
                  pl.BlockSpec((nb, N_PAIRS, LANES, LANES), lambda b, c: (b, 0, 0, 0)),
                  const((1, RW_COLS))] + [const((1, RW_WIDTH))] * 7
                 + [const((LANES, 2 * RW_WIDTH)), const((GATE_LORA, RW_WIDTH))],
        out_specs=[pl.BlockSpec((nb, C, RW_WIDTH), lambda b, c: (b, c, 0)),
                   pl.BlockSpec((nb, N_PAIRS, LANES, LANES), lambda b, c: (b, 0, 0, 0))],
        out_shape=[jax.ShapeDtypeStruct((B, T, RW_WIDTH), BF16),
                   jax.ShapeDtypeStruct((B, N_PAIRS, LANES, LANES), F32)],
        scratch_shapes=[pltpu.VMEM((nb, 8 + C, RW_COLS), F32),
                        pltpu.VMEM((nb, N_PAIRS, LANES, LANES), F32)],
        compiler_params=pltpu.CompilerParams(dimension_semantics=("arbitrary", "arbitrary"),
                                             vmem_limit_bytes=VMEM_LIMIT),
        name="rwkv7_chunked",
    )(p_rw, shift0, h0_bd, *vecs, wlora, wg)


def _state_to_bd(wkv):
    B = wkv.shape[0]
    h = jnp.swapaxes(wkv, -1, -2).reshape(B, N_PAIRS, 2, RW_HEAD, RW_HEAD)
    z = jnp.zeros_like(h[:, :, 0])
    top = jnp.concatenate([h[:, :, 0], z], axis=-1)
    bot = jnp.concatenate([z, h[:, :, 1]], axis=-1)
    return jnp.concatenate([top, bot], axis=-2)


def _bd_to_state(hbd):
    B = hbd.shape[0]
    h0 = hbd[:, :, 0:RW_HEAD, 0:RW_HEAD]
    h1 = hbd[:, :, RW_HEAD:, RW_HEAD:]
    h = jnp.stack([h0, h1], axis=2).reshape(B, RW_HEADS, RW_HEAD, RW_HEAD)
    return jnp.swapaxes(h, -1, -2)


_INT_MIN = -2 ** 31
_T5_STEPS = ((91, 14), (64, 13), (46, 12), (32, 11), (23, 10), (16, 9), (12, 8), (8, 7), (7, 6),
             (6, 5), (5, 4), (4, 3), (3, 2), (2, 1), (1, 0))
_T5_FAR = 15


def _half_masked(x, half):
    low = (_lane_iota(x.shape) % LANES) < HEAD_DIM
    keep = low if half == 0 else jnp.logical_not(low)
    return jnp.where(keep, x, jnp.zeros_like(x))


def _dup_half(x, half):
    xr = pltpu.roll(x, HEAD_DIM, axis=1)
    low = _lane_iota(x.shape) < HEAD_DIM
    return jnp.where(low, x, xr) if half == 0 else jnp.where(low, xr, x)


def _dot_nt(a, b):
    return lax.dot_general(a, b, (((1,), (1,)), ((), ())), preferred_element_type=F32)


def _dsa_kernel(*refs, Tq, P, Tn, Lp, topk):
    if P > 0:
        (q_ref, qi_ref, wi_ref, hk_ref, hv_ref, hki_ref, nk_ref, nv_ref, nki_ref, rb_ref,
         att_ref, kd, vd, kid, lg, tbl) = refs
    else:
        (q_ref, qi_ref, wi_ref, nk_ref, nv_ref, nki_ref, rb_ref,
         att_ref, kd, vd, kid, lg, tbl) = refs
    b = pl.program_id(0)
    i = pl.program_id(1)
    L = P + Tn

    @pl.when(jnp.logical_and(b == 0, i == 0))
    def _():
        shape = (Tq, 2 * LANES)
        rel = _lane_iota(shape) - LANES - _row_iota(shape)
        n = jnp.abs(rel)
        for h in range(N_HEADS):
            far = rb_ref[_T5_FAR, h]
            neg = jnp.full(shape, far, F32)
            pos = jnp.full(shape, rb_ref[_T5_FAR + N_BUCKETS // 2, h], F32)
            for bound, bucket in _T5_STEPS:
                neg = jnp.where(n < bound, rb_ref[bucket, h], neg)
                pos = jnp.where(n < bound, rb_ref[bucket + N_BUCKETS // 2, h], pos)
            tbl[h] = jnp.where(rel > 0, pos, neg) - far

    @pl.when(i == 0)
    def _():
        def stage(k_src, v_src, ki_src, r0, n):
            kf = k_src[0]
            vf = v_src[0]
            kif = ki_src[0]
            for j in range(N_KV_HEADS):
                kd[j, r0:r0 + n, :] = _dup_half(kf, j).astype(BF16)
                vd[j, r0:r0 + n, :] = _dup_half(vf, j).astype(BF16)
            kid[r0:r0 + n, :] = jnp.concatenate([kif, kif], axis=1).astype(BF16)

        if P > 0:
            stage(hk_ref, hv_ref, hki_ref, 0, P)
        stage(nk_ref, nv_ref, nki_ref, P, Tn)
        if Lp > L:
            z = jnp.zeros((Lp - L, LANES), BF16)
            for j in range(N_KV_HEADS):
                kd[j, L:Lp, :] = z
                vd[j, L:Lp, :] = z
            kid[L:Lp, :] = z

    q = q_ref[0]
    qi = qi_ref[0]
    wi = wi_ref[0]

    kidv = kid[...]
    score = jnp.zeros((Tq, Lp), F32)
    for h in range(N_IDX_HEADS):
        pr = h // 2
        lhs = _half_masked(qi[:, pr * LANES:(pr + 1) * LANES], h % 2)
        s_h = _dot_nt(lhs, kidv)
        score = score + wi[:, h:h + 1] * jnp.maximum(s_h, 0.0)

    s_pos = _lane_iota((Tq, Lp))
    q_pos = P + i * Tq + _row_iota((Tq, Lp))
    adm = jnp.logical_and(s_pos // CHUNK <= q_pos // CHUNK, s_pos < L)
    score = jnp.where(adm, score, NEG) + 0.0
    bits = pltpu.bitcast(score, jnp.int32)
    key = jnp.where(bits < 0, bits ^ jnp.int32(0x7FFFFFFF), bits)

    def count(mask):
        return jnp.sum(jnp.where(mask, 1.0, 0.0), axis=1, keepdims=True)

    kf = float(topk)
    ans = jnp.where(count(key >= 0) >= kf, jnp.int32(0), jnp.int32(_INT_MIN))

    def bit_step(it, ans):
        cand = ans | (jnp.int32(1) << (30 - it))
        return jnp.where(count(key >= cand) >= kf, cand, ans)

    thr = lax.fori_loop(0, 31, bit_step, ans)
    gt = key > thr
    eq = key == thr
    need = kf - count(gt)

    n_idx_bits = (Lp - 1).bit_length()

    def idx_step(it, x):
        cand = x | (jnp.int32(1) << (n_idx_bits - 1 - it))
        return jnp.where(count(jnp.logical_and(eq, s_pos < cand)) < need, cand, x)

    last = lax.fori_loop(0, n_idx_bits, idx_step, jnp.zeros((Tq, 1), jnp.int32))
    sel = jnp.logical_and(jnp.logical_or(gt, jnp.logical_and(eq, s_pos <= last)), adm)
    sel4 = jnp.concatenate([sel] * GROUP, axis=0)

    blk_end = (P + i * Tq) // LANES + 1
    for j in range(N_KV_HEADS):
        rows = []
        for g in range(GROUP):
            h = j * GROUP + g
            rows.append(_half_masked(q[:, (h // 2) * LANES:(h // 2 + 1) * LANES], h % 2))
        lhs = jnp.concatenate(rows, axis=0)
        lg[...] = _dot_nt(lhs, kd[j])

        @pl.when(blk_end >= 2)
        def _():
            start = pl.multiple_of((blk_end - 2) * LANES, LANES)
            for g in range(GROUP):
                r = slice(g * Tq, (g + 1) * Tq)
                lg[r, pl.ds(start, 2 * LANES)] = lg[r, pl.ds(start, 2 * LANES)] + tbl[j * GROUP + g]

        @pl.when(blk_end < 2)
        def _():
            for g in range(GROUP):
                r = slice(g * Tq, (g + 1) * Tq)
                lg[r, 0:LANES] = lg[r, 0:LANES] + tbl[j * GROUP + g][:, LANES:2 * LANES]

        logits = jnp.where(sel4, lg[...], NEG)
        m = jnp.max(logits, axis=1, keepdims=True)
        p = jnp.exp(logits - m)
        l = jnp.sum(p, axis=1, keepdims=True)
        o = jnp.dot(p.astype(BF16), vd[j], preferred_element_type=F32) / l
        low = _lane_iota((Tq, LANES)) < HEAD_DIM
        for pp in range(GROUP // 2):
            o_even = o[(2 * pp) * Tq:(2 * pp + 1) * Tq]
            o_odd = o[(2 * pp + 1) * Tq:(2 * pp + 2) * Tq]
            c0 = (j * GROUP // 2 + pp) * LANES
            att_ref[0, :, c0:c0 + LANES] = jnp.where(low, o_even, o_odd).astype(att_ref.dtype)


def _dsa_call(q, qi, wi, hist, new, rel_bias, Tq):
    B, T, _ = q.shape
    Tn = new[0].shape[1]
    P = 0 if hist is None else hist[0].shape[1]
    L = P + Tn
    Lp = -(-L // LANES) * LANES
    topk = min(MAX_TOPK, L // 4)
    assert T % Tq == 0 and Tn == T and P % LANES == 0 and (Tq % LANES == 0 or T == Tq)
    qspec = lambda w: pl.BlockSpec((1, Tq, w), lambda b, i: (b, i, 0))
    full = lambda n, w: pl.BlockSpec((1, n, w), lambda b, i: (b, 0, 0))
    in_specs = [qspec(ATT_WIDTH), qspec(N_IDX_HEADS * IDX_DIM), qspec(N_IDX_HEADS)]
    args = [q, qi, wi]
    if P > 0:
        in_specs += [full(P, KV_WIDTH), full(P, KV_WIDTH), full(P, IDX_DIM)]
        args += list(hist)
    in_specs += [full(Tn, KV_WIDTH), full(Tn, KV_WIDTH), full(Tn, IDX_DIM),
                 pl.BlockSpec(memory_space=pltpu.SMEM)]
    args += list(new) + [rel_bias.astype(F32)]
    kern = functools.partial(_dsa_kernel, Tq=Tq, P=P, Tn=Tn, Lp=Lp, topk=topk)
    return pl.pallas_call(
        kern,
        grid=(B, T // Tq),
        in_specs=in_specs,
        out_specs=pl.BlockSpec((1, Tq, ATT_WIDTH), lambda b, i: (b, i, 0)),
        out_shape=jax.ShapeDtypeStruct((B, T, ATT_WIDTH), BF16),
        scratch_shapes=[pltpu.VMEM((N_KV_HEADS, Lp, LANES), BF16),
                        pltpu.VMEM((N_KV_HEADS, Lp, LANES), BF16),
                        pltpu.VMEM((Lp, LANES), BF16),
                        pltpu.VMEM((GROUP * Tq, Lp), F32),
                        pltpu.VMEM((N_HEADS, Tq, 2 * LANES), F32)],
        compiler_params=pltpu.CompilerParams(dimension_semantics=("arbitrary", "arbitrary"),
                                             vmem_limit_bytes=VMEM_LIMIT),
        name="dsa_attention",
    )(*args)


ATT_COLS = 2 * ATT_WIDTH + 2 * KV_WIDTH + LANES
_C_Q, _C_QI, _C_K, _C_V, _C_KI = 0, 512, 1024, 1152, 1280
_C_WI = _C_KI + IDX_DIM


def _rms(x, g):
    return x * lax.rsqrt(jnp.mean(x * x, axis=-1, keepdims=True) + NORM_EPS) * g


def _inproj_kernel(x_ref, g_ref, wa_ref, wr_ref, q_ref, qi_ref, k_ref, v_ref, ki_ref, wi_ref, prw_ref):
    h = _rms(x_ref[...], g_ref[...]).astype(BF16)
    pa = jnp.dot(h, wa_ref[...], preferred_element_type=F32)
    q_ref[...] = (pa[:, _C_Q:_C_Q + ATT_WIDTH] * (HEAD_DIM ** -0.5)).astype(BF16)
    qi_ref[...] = (pa[:, _C_QI:_C_QI + ATT_WIDTH] * (IDX_DIM ** -0.5)).astype(BF16)
    k_ref[...] = pa[:, _C_K:_C_K + KV_WIDTH]
    v_ref[...] = pa[:, _C_V:_C_V + KV_WIDTH]
    ki_ref[...] = pa[:, _C_KI:_C_KI + IDX_DIM]
    wi_ref[...] = pa[:, _C_WI:_C_WI + N_IDX_HEADS] * (N_IDX_HEADS ** -0.5)
    prw_ref[...] = jnp.dot(h, wr_ref[...], preferred_element_type=F32)


def _const_spec(shape):
    return pl.BlockSpec(shape, lambda i: (0,) * len(shape), pipeline_mode=pl.Buffered(1))


def _inproj_call(x2d, g, wa, wr, tm):
    N = x2d.shape[0]
    assert N % tm == 0
    rows = lambda w: pl.BlockSpec((tm, w), lambda i: (i, 0))
    widths = [ATT_WIDTH, ATT_WIDTH, KV_WIDTH, KV_WIDTH, IDX_DIM, N_IDX_HEADS, RW_COLS]
    dtypes = [BF16, BF16, F32, F32, F32, F32, F32]
    return pl.pallas_call(
        _inproj_kernel,
        grid=(N // tm,),
        in_specs=[rows(D_MODEL), _const_spec((1, D_MODEL)), _const_spec((D_MODEL, ATT_COLS)),
                  _const_spec((D_MODEL, RW_COLS))],
        out_specs=[rows(w) for w in widths],
        out_shape=[jax.ShapeDtypeStruct((N, w), d) for w, d in zip(widths, dtypes)],
        compiler_params=pltpu.CompilerParams(dimension_semantics=("arbitrary",),
                                             vmem_limit_bytes=VMEM_LIMIT),
        name="rmsnorm_inproj",
    )(x2d, g, wa, wr)


def _tail_kernel(x_ref, att_ref, rw_ref, g1_ref, wgate_ref, wao_ref, wro_ref, wout_ref,
                 g2_ref, wff1_ref, wff2_ref, gf_ref, y_ref):
    x = x_ref[...]
    h = _rms(x, g1_ref[...]).astype(BF16)
    gates = jax.nn.sigmoid(jnp.dot(h, wgate_ref[...], preferred_element_type=F32))
    z_a = jnp.dot(att_ref[...], wao_ref[...], preferred_element_type=F32)
    z_b = jnp.dot(rw_ref[...], wro_ref[...], preferred_element_type=F32)
    mixed = gates[:, 0:D_MODEL] * z_a + gates[:, D_MODEL:2 * D_MODEL] * z_b
    x1 = x + jnp.dot(mixed.astype(BF16), wout_ref[...], preferred_element_type=F32)
    h2 = _rms(x1, g2_ref[...]).astype(BF16)
    f = jnp.maximum(jnp.dot(h2, wff1_ref[...], preferred_element_type=F32), 0.0)
    x2 = x1 + jnp.dot((f * f).astype(BF16), wff2_ref[...], preferred_element_type=F32)
    y_ref[...] = _rms(x2, gf_ref[...])


def _tail_call(x2d, att, rw, g1, wgate, wao, wro, wout, g2, wff1, wff2, gf, tm):
    N = x2d.shape[0]
    assert N % tm == 0
    rows = lambda w: pl.BlockSpec((tm, w), lambda i: (i, 0))
    return pl.pallas_call(
        _tail_kernel,
        grid=(N // tm,),
        in_specs=[rows(D_MODEL), rows(ATT_WIDTH), rows(RW_WIDTH), _const_spec((1, D_MODEL)),
                  _const_spec((D_MODEL, 2 * D_MODEL)), _const_spec((ATT_WIDTH, D_MODEL)),
                  _const_spec((RW_WIDTH, D_MODEL)), _const_spec((D_MODEL, D_MODEL)),
                  _const_spec((1, D_MODEL)), _const_spec((D_MODEL, D_FF)), _const_spec((D_FF, D_MODEL)),
                  _const_spec((1, D_MODEL))],
        out_specs=rows(D_MODEL),
        out_shape=jax.ShapeDtypeStruct((N, D_MODEL), F32),
        compiler_params=pltpu.CompilerParams(dimension_semantics=("arbitrary",),
                                             vmem_limit_bytes=VMEM_LIMIT),
        name="merge_out_mlp",
    )(x2d, att, rw, g1, wgate, wao, wro, wout, g2, wff1, wff2, gf)


_IN_SPLITS = (ATT_WIDTH, KV_WIDTH, KV_WIDTH, N_IDX_HEADS * IDX_DIM, IDX_DIM, N_IDX_HEADS,
              RW_COLS, D_MODEL, D_MODEL)


def _split_w_in(w_in):
    offs = [0]
    for s in _IN_SPLITS:
        offs.append(offs[-1] + s)
    col = lambda n: w_in[:, offs[n]:offs[n + 1]]
    pad = jnp.zeros((D_MODEL, ATT_COLS - _C_WI - N_IDX_HEADS), w_in.dtype)
    wa = jnp.concatenate([col(0), col(3), col(1), col(2), col(4), col(5), pad], axis=1).astype(BF16)
    wr = col(6).astype(BF16)
    wgate = jnp.concatenate([col(7), col(8)], axis=1).astype(BF16)
    return wa, wr, wgate


def _branch(x, hist, wkv0, shift0, rel_bias, prm, w, tm_in, tm_tail, Tq, C, nb_rw):
    B, T, _ = x.shape
    x2d = x.reshape(B * T, D_MODEL)
    q, qi, k, v, ki, wi, p_rw = _inproj_call(x2d, prm['g1'], w['wa'], w['wr'], tm_in)
    r3 = lambda a: a.reshape(B, T, a.shape[-1])
    rw, hT = _rwkv_call(r3(p_rw), shift0, _state_to_bd(wkv0), prm, C, nb_rw)
    att = _dsa_call(r3(q), r3(qi), r3(wi), hist, (r3(k), r3(v), r3(ki)), rel_bias, Tq)
    y = _tail_call(x2d, att.reshape(B * T, ATT_WIDTH), rw.reshape(B * T, RW_WIDTH), prm['g1'],
                   w['wgate'], w['wao'], w['wro'], w['wout'], prm['g2'], w['wff1'], w['wff2'],
                   prm['gf'], tm_tail)
    new_k = k.reshape(1, B, T, N_KV_HEADS, HEAD_DIM)
    new_v = v.reshape(1, B, T, N_KV_HEADS, HEAD_DIM)
    new_ki = ki.reshape(1, B, T, IDX_DIM)
    wkv = _bd_to_state(hT)[None]
    shift = r3(p_rw)[:, T - 1:T, :][None]
    return y.reshape(B, T, D_MODEL), new_k, new_v, new_ki, wkv, shift


def kernel(x_prompt, x_sample, cache_k, cache_v, cache_kidx, state_wkv, state_shift, rel_bias,
           norm1_g, w_in, w_att_out, mu_shift, w0, w_decay_up, a0, w_a_up, w_g_up, k_k, k_a, r_k,
           lnx_w, lnx_b, w_rw_out, w_out, norm2_g, w_ff1, w_ff2, norm_f_g):
    assert w_in.shape[0] == 1, "single-layer step"
    wa, wr, wgate = _split_w_in(w_in[0])
    w = dict(wa=wa, wr=wr, wgate=wgate, wao=w_att_out[0].astype(BF16), wro=w_rw_out[0].astype(BF16),
             wout=w_out[0].astype(BF16), wff1=w_ff1[0].astype(BF16), wff2=w_ff2[0].astype(BF16))
    prm = dict(mu_shift=mu_shift[0], w0=w0[0], w_decay_up=w_decay_up[0], a0=a0[0], w_a_up=w_a_up[0],
               w_g_up=w_g_up[0], k_k=k_k[0], k_a=k_a[0], r_k=r_k[0].reshape(-1), lnx_w=lnx_w[0],
               lnx_b=lnx_b[0], g1=norm1_g[0].reshape(1, -1), g2=norm2_g[0].reshape(1, -1),
               gf=norm_f_g.reshape(1, -1))
    Bp, Tp, _ = x_prompt.shape
    Bs, Ts, _ = x_sample.shape
    past = cache_k.shape[2]
    zero_state = jnp.zeros((Bp, RW_HEADS, RW_HEAD, RW_HEAD), F32)
    zero_shift = jnp.zeros((Bp, 1, RW_COLS), F32)
    outs_p = _branch(x_prompt, None, zero_state, zero_shift, rel_bias, prm, w,
                     tm_in=512, tm_tail=256, Tq=Q_BLOCK, C=CHUNK, nb_rw=2)
    hist = (cache_k[0].reshape(Bs, past, KV_WIDTH), cache_v[0].reshape(Bs, past, KV_WIDTH),
            cache_kidx[0])
    outs_s = _branch(x_sample, hist, state_wkv[0], state_shift[0], rel_bias, prm, w,
                     tm_in=Bs * Ts, tm_tail=256, Tq=Ts, C=Ts, nb_rw=4)
    return (outs_p[0], outs_s[0]) + outs_p[1:] + outs_s[1:]
```

```python
import functools
import math

import jax
import jax.numpy as jnp
import numpy as np
from jax import lax
from jax.experimental import pallas as pl
from jax.experimental.pallas import tpu as pltpu

F32 = jnp.float32
BF16 = jnp.bfloat16

D_MODEL = 1024
CHUNK = 64
N_HEADS = 8
N_KV_HEADS = 2
HEAD_DIM = 64
GROUP = N_HEADS // N_KV_HEADS
ATT_WIDTH = N_HEADS * HEAD_DIM
KV_WIDTH = N_KV_HEADS * HEAD_DIM
N_IDX_HEADS = 8
IDX_DIM = 64
MAX_TOPK = 256
Q_BLOCK = 128
N_BUCKETS = 32
RW_HEADS = 8
RW_HEAD = 64
RW_WIDTH = RW_HEADS * RW_HEAD
DECAY_LORA = 64
AAA_LORA = 64
GATE_LORA = 128
RW_COLS = 3 * RW_WIDTH + DECAY_LORA + AAA_LORA + GATE_LORA
GN_EPS = 64e-5
D_FF = 4 * D_MODEL
NORM_EPS = 1e-6
NEG = -1e30

LANES = 128
N_PAIRS = RW_HEADS // 2
VMEM_LIMIT = 56 * 1024 * 1024

CHUNK_PIECES = 1
SEG_PIECES = 2

_NN, _NT, _TN = (1, 0), (1, 1), (0, 0)


def _split(x, n):
    pieces = []
    rem = x
    for i in range(n):
        p = rem.astype(BF16)
        pieces.append(p)
        if i + 1 < n:
            rem = rem - p.astype(F32)
    return tuple(pieces)


def _mm(a, b, dims=_NN):
    n = max(len(a), len(b))
    terms = [(i, j) for i in range(len(a)) for j in range(len(b)) if i + j < n]
    ca, cb = dims
    lhs = jnp.concatenate([a[i] for i, _ in terms], axis=ca) if len(terms) > 1 else a[0]
    rhs = jnp.concatenate([b[j] for _, j in terms], axis=cb) if len(terms) > 1 else b[0]
    return lax.dot_general(lhs, rhs, (((ca,), (cb,)), ((), ())), preferred_element_type=F32)


def _lane_iota(shape):
    return lax.broadcasted_iota(jnp.int32, shape, len(shape) - 1)


def _row_iota(shape):
    return lax.broadcasted_iota(jnp.int32, shape, len(shape) - 2)


def _bdrows(x, half):
    low = (_lane_iota(x.shape) % (2 * half)) < half
    return jnp.concatenate([jnp.where(low, x, 0.0), jnp.where(low, 0.0, x)], axis=0)


def _rwkv_kernel(x_ref, shift0_ref, h0_ref, mu_ref, w0_ref, a0_ref, kk_ref, ka_ref, rk_ref,
                 lnw_ref, lnb_ref, wlora_ref, wg_ref,
                 out_ref, hT_ref, xbuf, hst, *, C, n_sq, nb):
    c = pl.program_id(1)

    @pl.when(c == 0)
    def _():
        for n in range(nb):
            xbuf[n, 7:8, :] = shift0_ref[n]
        hst[...] = h0_ref[...]

    W = RW_WIDTH
    R = nb * C
    prevs = []
    for n in range(nb):
        xn = x_ref[n]
        xbuf[n, 8:8 + C, :] = xn
        prevs.append(xbuf[n, 7:7 + C, :])
        xbuf[n, 7:8, :] = xn[C - 1:C, :]
    x = x_ref[...].reshape(R, RW_COLS)
    prev = jnp.concatenate(prevs, axis=0)
    xs = x + (prev - x) * mu_ref[...]
    r = xs[:, 0:W]
    k = xs[:, W:2 * W]
    v = xs[:, 2 * W:3 * W]
    da = xs[:, 3 * W:3 * W + LANES]
    gd = xs[:, 3 * W + LANES:3 * W + 2 * LANES]

    t_in = jnp.where(_lane_iota(da.shape) < DECAY_LORA, jnp.tanh(da), da)
    lo = jnp.dot(t_in.astype(BF16), wlora_ref[...], preferred_element_type=F32)
    z = -(w0_ref[...] + lo[:, 0:W])
    softplus = jnp.maximum(z, 0.0) + jnp.log1p(jnp.exp(-jnp.abs(z)))
    w = -softplus - 0.5
    ld = -jnp.exp(w)
    a = jax.nn.sigmoid(a0_ref[...] + lo[:, W:2 * W])
    g = jnp.dot(jax.nn.sigmoid(gd).astype(BF16), wg_ref[...], preferred_element_type=F32)

    seg = (_row_iota((LANES, LANES)) // RW_HEAD == _lane_iota((LANES, LANES)) // RW_HEAD).astype(BF16)

    def segsum(t):
        return jnp.concatenate(
            [_mm(_split(t[:, p * LANES:(p + 1) * LANES], SEG_PIECES), (seg,)) for p in range(N_PAIRS)],
            axis=1)

    kk0 = k * kk_ref[...]
    kk = kk0 / jnp.maximum(jnp.sqrt(segsum(kk0 * kk0)), 1e-12)
    k2 = k * (1.0 + (a - 1.0) * ka_ref[...])
    bb = kk * a

    ti, si = _row_iota((R, R)), _lane_iota((R, R))
    tri = jnp.logical_and(si <= ti, si // C == ti // C).astype(BF16)
    lp = _mm((tri,), _split(ld, 3))
    lpp = lp - ld
    rows_of = lambda i: jnp.concatenate(
        [jnp.broadcast_to(lp[n * C + i:n * C + i + 1, :], (C, W)) for n in range(nb)], axis=0)
    lpc = rows_of(C - 1)
    mid = rows_of(C // 2 - 1)
    r_abs = r * jnp.exp(lp)
    kap_abs = kk * jnp.exp(lpp)
    r_mid = r * jnp.exp(lp - mid)
    kap_mid = kk * jnp.exp(lpp - mid)
    e_mid = jnp.exp(mid - lp)
    b_mid = bb * e_mid
    k_mid = k2 * e_mid
    e_end = jnp.exp(lpc - lp)
    b_end = bb * e_end
    k_end = k2 * e_end
    p_end = jnp.exp(lpc)

    tt = _row_iota((C, 2 * C))
    ss = _lane_iota((C, 2 * C)) % C
    strict = ss < tt
    incl = ss <= tt
    eye_p = (ss == tt).astype(F32)
    jrow = _row_iota((LANES, LANES))
    jlane = _lane_iota((LANES, LANES))
    same_head = (jrow // RW_HEAD) == (jlane // RW_HEAD)
    eye_j = jrow == jlane

    sp = lambda t: _split(t, CHUNK_PIECES)
    bd = lambda pieces, half: tuple(_bdrows(t, half) for t in pieces)

    chains = [(n, p) for n in range(nb) for p in range(N_PAIRS)]
    blk = lambda t, ch: t[ch[0] * C:(ch[0] + 1) * C, ch[1] * LANES:(ch[1] + 1) * LANES]
    each = lambda f: [f(i, ch) for i, ch in enumerate(chains)]

    A = each(lambda i, ch: _mm(
        sp(jnp.concatenate([blk(kap_mid, ch), blk(r_mid, ch)], axis=0)),
        sp(jnp.concatenate([_bdrows(blk(b_mid, ch), RW_HEAD), _bdrows(blk(k_mid, ch), RW_HEAD)],
                           axis=0)), _NT))
    n_kb = each(lambda i, ch: jnp.where(strict, A[i][0:C, 0:2 * C], 0.0))
    a_kk = each(lambda i, ch: jnp.where(strict, A[i][0:C, 2 * C:4 * C], 0.0))
    a_rb = each(lambda i, ch: jnp.where(incl, A[i][C:2 * C, 0:2 * C], 0.0))
    a_rk = each(lambda i, ch: jnp.where(incl, A[i][C:2 * C, 2 * C:4 * C], 0.0))
    T = each(lambda i, ch: eye_p - n_kb[i])
    pw = each(lambda i, ch: sp(n_kb[i]))
    for _ in range(n_sq):
        pw = each(lambda i, ch: sp(_mm(pw[i], bd(pw[i], C))))
        T = each(lambda i, ch: T[i] + _mm(sp(T[i]), bd(pw[i], C)))
    v_p = each(lambda i, ch: sp(blk(v, ch)))
    av = each(lambda i, ch: _mm(sp(jnp.concatenate([a_kk[i], a_rk[i]], axis=0)),
                                bd(v_p[i], RW_HEAD)))
    wu = each(lambda i, ch: sp(_mm(
        sp(T[i]), bd(sp(jnp.concatenate([blk(kap_abs, ch), av[i][0:C]], axis=1)), RW_HEAD))))
    qy = each(lambda i, ch: jnp.concatenate([blk(r_abs, ch), av[i][C:2 * C]], axis=1)
              - _mm(sp(a_rb[i]), bd(wu[i], RW_HEAD)))
    bw = each(lambda i, ch: _mm(sp(blk(b_end, ch)), wu[i], _TN))
    kv = each(lambda i, ch: _mm(_split(blk(k_end, ch), 2), _split(blk(v, ch), 2), _TN))
    G = each(lambda i, ch: jnp.where(same_head, kv[i] - bw[i][:, LANES:2 * LANES], 0.0))
    M = each(lambda i, ch: jnp.where(same_head, bw[i][:, 0:LANES], 0.0))
    p_col = each(lambda i, ch: jnp.sum(jnp.where(eye_j, blk(p_end, ch)[0:1], 0.0), axis=1,
                                       keepdims=True))
    h0f = each(lambda i, ch: hst[ch[0], ch[1]])
    h0 = each(lambda i, ch: sp(h0f[i]))
    yc_ = each(lambda i, ch: _mm(sp(qy[i][:, 0:LANES]), h0[i]) + qy[i][:, LANES:2 * LANES])
    for i, ch in enumerate(chains):
        hst[ch[0], ch[1]] = p_col[i] * h0f[i] - _mm(sp(M[i]), h0[i]) + G[i]
    y = jnp.concatenate([jnp.concatenate(yc_[n * N_PAIRS:(n + 1) * N_PAIRS], axis=1)
                         for n in range(nb)], axis=0)

    mean = segsum(y) * (1.0 / RW_HEAD)
    yc = y - mean
    var = segsum(yc * yc) * (1.0 / RW_HEAD)
    yn = yc * lax.rsqrt(var + GN_EPS) * lnw_ref[...] + lnb_ref[...]
    bonus = segsum(r * k2 * rk_ref[...]) * v
    out_ref[...] = ((yn + bonus) * g).astype(out_ref.dtype).reshape(nb, C, W)

    @pl.when(c == pl.num_programs(1) - 1)
    def _():
        hT_ref[...] = hst[...]


def _rwkv_call(p_rw, shift0, h0_bd, prm, C, nb):
    B, T, _ = p_rw.shape
    n_sq = int(math.log2(C)) - 1
    assert 2 ** (n_sq + 1) == C and T % C == 0 and B % nb == 0
    row = lambda a: a.reshape(1, -1).astype(F32)
    vecs = [row(prm['mu_shift']), row(prm['w0']), row(prm['a0']), row(prm['k_k']), row(prm['k_a']),
            row(prm['r_k']), row(prm['lnx_w']), row(prm['lnx_b'])]
    wlora = jnp.zeros((LANES, 2 * RW_WIDTH), F32)
    wlora = wlora.at[0:DECAY_LORA, 0:RW_WIDTH].set(prm['w_decay_up'])
    wlora = wlora.at[DECAY_LORA:, RW_WIDTH:].set(prm['w_a_up']).astype(BF16)
    wg = prm['w_g_up'].astype(BF16)
    const = lambda shape: pl.BlockSpec(shape, lambda b, c: (0,) * len(shape))
    kern = functools.partial(_rwkv_kernel, C=C, n_sq=n_sq, nb=nb)
    return pl.pallas_call(
        kern,
        grid=(B // nb, T // C),
        in_specs=[pl.BlockSpec((nb, C, RW_COLS), lambda b, c: (b, c, 0)),
                  pl.BlockSpec((nb, 1, RW_COLS), lambda b, c: (b, 0, 0)),
                  pl.BlockSpec((nb, N_PAIRS, LANES, LANES), lambda b, c: (b, 0, 0, 0)),
                  const((1, RW_COLS))] + [const((1, RW_WIDTH))] * 7
                 + [const((LANES, 2 * RW_WIDTH)), const((GATE_LORA, RW_WIDTH))],
        out_specs=[pl.BlockSpec((nb, C, RW_WIDTH), lambda b, c: (b, c, 0)),
                   pl.BlockSpec((nb, N_PAIRS, LANES, LANES), lambda b, c: (b, 0, 0, 0))],
        out_shape=[jax.ShapeDtypeStruct((B, T, RW_WIDTH), BF16),
                   jax.ShapeDtypeStruct((B, N_PAIRS, LANES, LANES), F32)],
        scratch_shapes=[pltpu.VMEM((nb, 8 + C, RW_COLS), F32),
                        pltpu.VMEM((nb, N_PAIRS, LANES, LANES), F32)],
        compiler_params=pltpu.CompilerParams(dimension_semantics=("arbitrary", "arbitrary"),
                                             vmem_limit_bytes=VMEM_LIMIT),
        name="rwkv7_chunked",
    )(p_rw, shift0, h0_bd, *vecs, wlora, wg)


def _state_to_bd(wkv):
    B = wkv.shape[0]
    h = jnp.swapaxes(wkv, -1, -2).reshape(B, N_PAIRS, 2, RW_HEAD, RW_HEAD)
    z = jnp.zeros_like(h[:, :, 0])
    top = jnp.concatenate([h[:, :, 0], z], axis=-1)
    bot = jnp.concatenate([z, h[:, :, 1]], axis=-1)
    return jnp.concatenate([top, bot], axis=-2)


def _bd_to_state(hbd):
    B = hbd.shape[0]
    h0 = hbd[:, :, 0:RW_HEAD, 0:RW_HEAD]
    h1 = hbd[:, :, RW_HEAD:, RW_HEAD:]
    h = jnp.stack([h0, h1], axis=2).reshape(B, RW_HEADS, RW_HEAD, RW_HEAD)
    return jnp.swapaxes(h, -1, -2)


_INT_MIN = -2 ** 31
_NEG_KEY = int(np.float32(NEG).view(np.int32)) ^ 0x7FFFFFFF
_T5_STEPS = ((91, 14), (64, 13), (46, 12), (32, 11), (23, 10), (16, 9), (12, 8), (8, 7), (7, 6),
             (6, 5), (5, 4), (4, 3), (3, 2), (2, 1), (1, 0))
_T5_FAR = 15


def _half_masked(x, half):
    low = (_lane_iota(x.shape) % LANES) < HEAD_DIM
    keep = low if half == 0 else jnp.logical_not(low)
    return jnp.where(keep, x, jnp.zeros_like(x))


def _dup_half(x, half):
    xr = pltpu.roll(x, HEAD_DIM, axis=1)
    low = _lane_iota(x.shape) < HEAD_DIM
    return jnp.where(low, x, xr) if half == 0 else jnp.where(low, xr, x)


def _dot_nt(a, b):
    return lax.dot_general(a, b, (((1,), (1,)), ((), ())), preferred_element_type=F32)


def _dsa_kernel(*refs, Tq, P, Tn, Lp, topk, n_cls):
    if P > 0:
        (q_ref, qi_ref, wi_ref, hk_ref, hv_ref, hki_ref, nk_ref, nv_ref, nki_ref, rb_ref,
         att_ref, kd, vd, kid, lg, tbl, sbias) = refs
    else:
        (q_ref, qi_ref, wi_ref, nk_ref, nv_ref, nki_ref, rb_ref,
         att_ref, kd, vd, kid, lg, tbl, sbias) = refs
    b = pl.program_id(0)
    i = pl.program_id(1)
    L = P + Tn

    @pl.when(jnp.logical_and(b == 0, i == 0))
    def _():
        shape = (Tq, 2 * LANES)
        rel = _lane_iota(shape) - LANES - _row_iota(shape)
        n = jnp.abs(rel)
        for h in range(N_HEADS):
            far = rb_ref[_T5_FAR, h]
            neg = jnp.full(shape, far, F32)
            pos = jnp.full(shape, rb_ref[_T5_FAR + N_BUCKETS // 2, h], F32)
            for bound, bucket in _T5_STEPS:
                neg = jnp.where(n < bound, rb_ref[bucket, h], neg)
                pos = jnp.where(n < bound, rb_ref[bucket + N_BUCKETS // 2, h], pos)
            tbl[h] = jnp.where(rel > 0, pos, neg) - far

    @pl.when(i == 0)
    def _():
        def stage(k_src, v_src, ki_src, r0, n):
            kf = k_src[0]
            vf = v_src[0]
            kif = ki_src[0]
            for j in range(N_KV_HEADS):
                kd[j, r0:r0 + n, :] = _dup_half(kf, j).astype(BF16)
                vd[j, r0:r0 + n, :] = _dup_half(vf, j).astype(BF16)
            kid[r0:r0 + n, :] = jnp.concatenate([kif, kif], axis=1).astype(BF16)

        if P > 0:
            stage(hk_ref, hv_ref, hki_ref, 0, P)
        stage(nk_ref, nv_ref, nki_ref, P, Tn)
        if Lp > L:
            z = jnp.zeros((Lp - L, LANES), BF16)
            for j in range(N_KV_HEADS):
                kd[j, L:Lp, :] = z
                vd[j, L:Lp, :] = z
            kid[L:Lp, :] = z

    stride = Lp // n_cls
    k_hi = P + (i + 1) * Tq
    cls = (k_hi - 1) // stride
    for c in range(n_cls):
        pl.when(cls == c)(functools.partial(
            _dsa_block, q_ref, qi_ref, wi_ref, att_ref, kd, vd, kid, lg, tbl, sbias,
            i=i, Tq=Tq, P=P, L=L, Lk=(c + 1) * stride, topk=topk))


def _dsa_block(q_ref, qi_ref, wi_ref, att_ref, kd, vd, kid, lg, tbl, sbias, *, i, Tq, P, L, Lk, topk):
    q = q_ref[0]
    qi = qi_ref[0]
    wi = wi_ref[0]

    kidv = kid[0:Lk, :]
    score = jnp.zeros((Tq, Lk), F32)
    for h in range(N_IDX_HEADS):
        pr = h // 2
        lhs = _half_masked(qi[:, pr * LANES:(pr + 1) * LANES], h % 2)
        s_h = _dot_nt(lhs, kidv)
        score = score + wi[:, h:h + 1] * jnp.maximum(s_h, 0.0)

    s_pos = _lane_iota((Tq, Lk))
    q_pos = P + i * Tq + _row_iota((Tq, Lk))
    adm = jnp.logical_and(s_pos // CHUNK <= q_pos // CHUNK, s_pos < L)
    score = jnp.where(adm, score, NEG)

    def count(mask):
        return jnp.sum(jnp.where(mask, 1.0, 0.0), axis=1, keepdims=True)

    def key_to_float(key):
        return pltpu.bitcast(jnp.where(key < 0, key ^ jnp.int32(0x7FFFFFFF), key), F32)

    kf = float(topk)
    ans = jnp.where(count(score >= 0.0) >= kf, jnp.int32(0), jnp.int32(_INT_MIN))

    def bit_step(it, ans):
        cand = ans | (jnp.int32(1) << (30 - it))
        return jnp.where(count(score >= key_to_float(cand)) >= kf, cand, ans)

    thr_key = lax.fori_loop(0, 31, bit_step, ans)
    thr_key = jnp.maximum(thr_key, jnp.int32(_NEG_KEY))
    thr = key_to_float(thr_key)
    gt = score > thr
    eq = score == thr
    need = kf - count(gt)
    tied = jnp.logical_and(count(eq) > need, thr > NEG)
    any_tied = jnp.max(jnp.where(tied, 1.0, 0.0)) > 0.0

    n_idx_bits = (Lk - 1).bit_length()

    def idx_search():
        def idx_step(it, x):
            cand = x | (jnp.int32(1) << (n_idx_bits - 1 - it))
            return jnp.where(count(jnp.logical_and(eq, s_pos < cand)) < need, cand, x)
        return lax.fori_loop(0, n_idx_bits, idx_step, jnp.zeros((Tq, 1), jnp.int32))

    last = lax.cond(any_tied, idx_search, lambda: jnp.full((Tq, 1), Lk, jnp.int32))
    sel = jnp.logical_and(jnp.logical_or(gt, jnp.logical_and(eq, s_pos <= last)), adm)
    sbias[:, 0:Lk] = jnp.where(sel, 0.0, NEG)

    blk_end = (P + i * Tq) // LANES + 1
    two_blocks = blk_end >= 2
    start = pl.multiple_of(jnp.maximum(blk_end - 2, 0) * LANES, LANES)
    for j in range(N_KV_HEADS):
        rows = []
        for g in range(GROUP):
            h = j * GROUP + g
            rows.append(_half_masked(q[:, (h // 2) * LANES:(h // 2 + 1) * LANES], h % 2))
        lhs = jnp.concatenate(rows, axis=0)
        lg[:, 0:Lk] = _dot_nt(lhs, kd[j, 0:Lk, :])

        @pl.when(two_blocks)
        def _():
            for g in range(GROUP):
                r = slice(g * Tq, (g + 1) * Tq)
                lg[r, pl.ds(start, 2 * LANES)] = lg[r, pl.ds(start, 2 * LANES)] + tbl[j * GROUP + g]

        @pl.when(jnp.logical_not(two_blocks))
        def _():
            for g in range(GROUP):
                r = slice(g * Tq, (g + 1) * Tq)
                lg[r, 0:LANES] = lg[r, 0:LANES] + tbl[j * GROUP + g][:, LANES:2 * LANES]

        sb = sbias[:, 0:Lk]
        os_ = []
        for g in range(GROUP):
            logits = lg[g * Tq:(g + 1) * Tq, 0:Lk] + sb
            m = jnp.max(logits, axis=1, keepdims=True)
            p = jnp.exp(logits - m)
            l = jnp.sum(p, axis=1, keepdims=True)
            os_.append(jnp.dot(p.astype(BF16), vd[j, 0:Lk, :], preferred_element_type=F32) / l)
        low = _lane_iota((Tq, LANES)) < HEAD_DIM
        for pp in range(GROUP // 2):
            c0 = (j * GROUP // 2 + pp) * LANES
            att_ref[0, :, c0:c0 + LANES] = jnp.where(low, os_[2 * pp], os_[2 * pp + 1]).astype(att_ref.dtype)


def _dsa_call(q, qi, wi, hist, new, rel_bias, Tq):
    B, T, _ = q.shape
    Tn = new[0].shape[1]
    P = 0 if hist is None else hist[0].shape[1]
    L = P + Tn
    Lp = -(-L // LANES) * LANES
    topk = min(MAX_TOPK, L // 4)
    assert T % Tq == 0 and Tn == T and P % LANES == 0 and (Tq % LANES == 0 or T == Tq)
    qspec = lambda w: pl.BlockSpec((1, Tq, w), lambda b, i: (b, i, 0))
    full = lambda n, w: pl.BlockSpec((1, n, w), lambda b, i: (b, 0, 0))
    in_specs = [qspec(ATT_WIDTH), qspec(N_IDX_HEADS * IDX_DIM), qspec(N_IDX_HEADS)]
    args = [q, qi, wi]
    if P > 0:
        in_specs += [full(P, KV_WIDTH), full(P, KV_WIDTH), full(P, IDX_DIM)]
        args += list(hist)
    in_specs += [full(Tn, KV_WIDTH), full(Tn, KV_WIDTH), full(Tn, IDX_DIM),
                 pl.BlockSpec(memory_space=pltpu.SMEM)]
    args += list(new) + [rel_bias.astype(F32)]
    n_cls = Lp // 512 if (P == 0 and Lp % 512 == 0) else 1
    kern = functools.partial(_dsa_kernel, Tq=Tq, P=P, Tn=Tn, Lp=Lp, topk=topk, n_cls=n_cls)
    return pl.pallas_call(
        kern,
        grid=(B, T // Tq),
        in_specs=in_specs,
        out_specs=pl.BlockSpec((1, Tq, ATT_WIDTH), lambda b, i: (b, i, 0)),
        out_shape=jax.ShapeDtypeStruct((B, T, ATT_WIDTH), BF16),
        scratch_shapes=[pltpu.VMEM((N_KV_HEADS, Lp, LANES), BF16),
                        pltpu.VMEM((N_KV_HEADS, Lp, LANES), BF16),
                        pltpu.VMEM((Lp, LANES), BF16),
                        pltpu.VMEM((GROUP * Tq, Lp), F32),
                        pltpu.VMEM((N_HEADS, Tq, 2 * LANES), F32),
                        pltpu.VMEM((Tq, Lp), F32)],
        compiler_params=pltpu.CompilerParams(dimension_semantics=("arbitrary", "arbitrary"),
                                             vmem_limit_bytes=VMEM_LIMIT),
        name="dsa_attention",
    )(*args)


ATT_COLS = 2 * ATT_WIDTH + 2 * KV_WIDTH + LANES
_C_Q, _C_QI, _C_K, _C_V, _C_KI = 0, 512, 1024, 1152, 1280
_C_WI = _C_KI + IDX_DIM


def _rms(x, g):
    return x * lax.rsqrt(jnp.mean(x * x, axis=-1, keepdims=True) + NORM_EPS) * g


def _inproj_kernel(x_ref, g_ref, wa_ref, wr_ref, q_ref, qi_ref, k_ref, v_ref, ki_ref, wi_ref, prw_ref):
    h = _rms(x_ref[...], g_ref[...]).astype(BF16)
    pa = jnp.dot(h, wa_ref[...], preferred_element_type=F32)
    q_ref[...] = (pa[:, _C_Q:_C_Q + ATT_WIDTH] * (HEAD_DIM ** -0.5)).astype(BF16)
    qi_ref[...] = (pa[:, _C_QI:_C_QI + ATT_WIDTH] * (IDX_DIM ** -0.5)).astype(BF16)
    k_ref[...] = pa[:, _C_K:_C_K + KV_WIDTH]
    v_ref[...] = pa[:, _C_V:_C_V + KV_WIDTH]
    ki_ref[...] = pa[:, _C_KI:_C_KI + IDX_DIM]
    wi_ref[...] = pa[:, _C_WI:_C_WI + N_IDX_HEADS] * (N_IDX_HEADS ** -0.5)
    prw_ref[...] = jnp.dot(h, wr_ref[...], preferred_element_type=F32)


def _const_spec(shape):
    return pl.BlockSpec(shape, lambda i: (0,) * len(shape), pipeline_mode=pl.Buffered(1))


def _inproj_call(x2d, g, wa, wr, tm):
    N = x2d.shape[0]
    assert N % tm == 0
    rows = lambda w: pl.BlockSpec((tm, w), lambda i: (i, 0))
    widths = [ATT_WIDTH, ATT_WIDTH, KV_WIDTH, KV_WIDTH, IDX_DIM, N_IDX_HEADS, RW_COLS]
    dtypes = [BF16, BF16, F32, F32, F32, F32, F32]
    return pl.pallas_call(
        _inproj_kernel,
        grid=(N // tm,),
        in_specs=[rows(D_MODEL), _const_spec((1, D_MODEL)), _const_spec((D_MODEL, ATT_COLS)),
                  _const_spec((D_MODEL, RW_COLS))],
        out_specs=[rows(w) for w in widths],
        out_shape=[jax.ShapeDtypeStruct((N, w), d) for w, d in zip(widths, dtypes)],
        compiler_params=pltpu.CompilerParams(dimension_semantics=("arbitrary",),
                                             vmem_limit_bytes=VMEM_LIMIT),
        name="rmsnorm_inproj",
    )(x2d, g, wa, wr)


def _tail_kernel(x_ref, att_ref, rw_ref, g1_ref, wgate_ref, wao_ref, wro_ref, wout_ref,
                 g2_ref, wff1_ref, wff2_ref, gf_ref, y_ref):
    x = x_ref[...]
    h = _rms(x, g1_ref[...]).astype(BF16)
    gates = jax.nn.sigmoid(jnp.dot(h, wgate_ref[...], preferred_element_type=F32))
    z_a = jnp.dot(att_ref[...], wao_ref[...], preferred_element_type=F32)
    z_b = jnp.dot(rw_ref[...], wro_ref[...], preferred_element_type=F32)
    mixed = gates[:, 0:D_MODEL] * z_a + gates[:, D_MODEL:2 * D_MODEL] * z_b
    x1 = x + jnp.dot(mixed.astype(BF16), wout_ref[...], preferred_element_type=F32)
    h2 = _rms(x1, g2_ref[...]).astype(BF16)
    f = jnp.maximum(jnp.dot(h2, wff1_ref[...], preferred_element_type=F32), 0.0)
    x2 = x1 + jnp.dot((f * f).astype(BF16), wff2_ref[...], preferred_element_type=F32)
    y_ref[...] = _rms(x2, gf_ref[...])


def _tail_call(x2d, att, rw, g1, wgate, wao, wro, wout, g2, wff1, wff2, gf, tm):
    N = x2d.shape[0]
    assert N % tm == 0
    rows = lambda w: pl.BlockSpec((tm, w), lambda i: (i, 0))
    return pl.pallas_call(
        _tail_kernel,
        grid=(N // tm,),
        in_specs=[rows(D_MODEL), rows(ATT_WIDTH), rows(RW_WIDTH), _const_spec((1, D_MODEL)),
                  _const_spec((D_MODEL, 2 * D_MODEL)), _const_spec((ATT_WIDTH, D_MODEL)),
                  _const_spec((RW_WIDTH, D_MODEL)), _const_spec((D_MODEL, D_MODEL)),
                  _const_spec((1, D_MODEL)), _const_spec((D_MODEL, D_FF)), _const_spec((D_FF, D_MODEL)),
                  _const_spec((1, D_MODEL))],
        out_specs=rows(D_MODEL),
        out_shape=jax.ShapeDtypeStruct((N, D_MODEL), F32),
        compiler_params=pltpu.CompilerParams(dimension_semantics=("arbitrary",),
                                             vmem_limit_bytes=VMEM_LIMIT),
        name="merge_out_mlp",
    )(x2d, att, rw, g1, wgate, wao, wro, wout, g2, wff1, wff2, gf)


_IN_SPLITS = (ATT_WIDTH, KV_WIDTH, KV_WIDTH, N_IDX_HEADS * IDX_DIM, IDX_DIM, N_IDX_HEADS,
              RW_COLS, D_MODEL, D_MODEL)


def _split_w_in(w_in):
    offs = [0]
    for s in _IN_SPLITS:
        offs.append(offs[-1] + s)
    col = lambda n: w_in[:, offs[n]:offs[n + 1]]
    pad = jnp.zeros((D_MODEL, ATT_COLS - _C_WI - N_IDX_HEADS), w_in.dtype)
    wa = jnp.concatenate([col(0), col(3), col(1), col(2), col(4), col(5), pad], axis=1).astype(BF16)
    wr = col(6).astype(BF16)
    wgate = jnp.concatenate([col(7), col(8)], axis=1).astype(BF16)
    return wa, wr, wgate


def _branch(x, hist, wkv0, shift0, rel_bias, prm, w, tm_in, tm_tail, Tq, C, nb_rw):
    B, T, _ = x.shape
    x2d = x.reshape(B * T, D_MODEL)
    q, qi, k, v, ki, wi, p_rw = _inproj_call(x2d, prm['g1'], w['wa'], w['wr'], tm_in)
    r3 = lambda a: a.reshape(B, T, a.shape[-1])
    rw, hT = _rwkv_call(r3(p_rw), shift0, _state_to_bd(wkv0), prm, C, nb_rw)
    att = _dsa_call(r3(q), r3(qi), r3(wi), hist, (r3(k), r3(v), r3(ki)), rel_bias, Tq)
    y = _tail_call(x2d, att.reshape(B * T, ATT_WIDTH), rw.reshape(B * T, RW_WIDTH), prm['g1'],
                   w['wgate'], w['wao'], w['wro'], w['wout'], prm['g2'], w['wff1'], w['wff2'],
                   prm['gf'], tm_tail)
    new_k = k.reshape(1, B, T, N_KV_HEADS, HEAD_DIM)
    new_v = v.reshape(1, B, T, N_KV_HEADS, HEAD_DIM)
    new_ki = ki.reshape(1, B, T, IDX_DIM)
    wkv = _bd_to_state(hT)[None]
    shift = r3(p_rw)[:, T - 1:T, :][None]
    return y.reshape(B, T, D_MODEL), new_k, new_v, new_ki, wkv, shift


def kernel(x_prompt, x_sample, cache_k, cache_v, cache_kidx, state_wkv, state_shift, rel_bias,
           norm1_g, w_in, w_att_out, mu_shift, w0, w_decay_up, a0, w_a_up, w_g_up, k_k, k_a, r_k,
           lnx_w, lnx_b, w_rw_out, w_out, norm2_g, w_ff1, w_ff2, norm_f_g):
    assert w_in.shape[0] == 1, "single-layer step"
    wa, wr, wgate = _split_w_in(w_in[0])
    w = dict(wa=wa, wr=wr, wgate=wgate, wao=w_att_out[0].astype(BF16), wro=w_rw_out[0].astype(BF16),
             wout=w_out[0].astype(BF16), wff1=w_ff1[0].astype(BF16), wff2=w_ff2[0].astype(BF16))
    prm = dict(mu_shift=mu_shift[0], w0=w0[0], w_decay_up=w_decay_up[0], a0=a0[0], w_a_up=w_a_up[0],
               w_g_up=w_g_up[0], k_k=k_k[0], k_a=k_a[0], r_k=r_k[0].reshape(-1), lnx_w=lnx_w[0],
               lnx_b=lnx_b[0], g1=norm1_g[0].reshape(1, -1), g2=norm2_g[0].reshape(1, -1),
               gf=norm_f_g.reshape(1, -1))
    Bp, Tp, _ = x_prompt.shape
    Bs, Ts, _ = x_sample.shape
    past = cache_k.shape[2]
    zero_state = jnp.zeros((Bp, RW_HEADS, RW_HEAD, RW_HEAD), F32)
    zero_shift = jnp.zeros((Bp, 1, RW_COLS), F32)
    outs_p = _branch(x_prompt, None, zero_state, zero_shift, rel_bias, prm, w,
                     tm_in=512, tm_tail=256, Tq=Q_BLOCK, C=CHUNK, nb_rw=2)
    hist = (cache_k[0].reshape(Bs, past, KV_WIDTH), cache_v[0].reshape(Bs, past, KV_WIDTH),
            cache_kidx[0])
    outs_s = _branch(x_sample, hist, state_wkv[0], state_shift[0], rel_bias, prm, w,
                     tm_in=Bs * Ts, tm_tail=256, Tq=Ts, C=Ts, nb_rw=4)
    return (outs_p[0], outs_s[0]) + outs_p[1:] + outs_s[1:]
```

```python
import functools
import math

import jax
import jax.numpy as jnp
import numpy as np
from jax import lax
from jax.experimental import pallas as pl
from jax.experimental.pallas import tpu as pltpu

F32 = jnp.float32
BF16 = jnp.bfloat16

D_MODEL = 1024
CHUNK = 64
N_HEADS = 8
N_KV_HEADS = 2
HEAD_DIM = 64
GROUP = N_HEADS // N_KV_HEADS
ATT_WIDTH = N_HEADS * HEAD_DIM
KV_WIDTH = N_KV_HEADS * HEAD_DIM
N_IDX_HEADS = 8
IDX_DIM = 64
MAX_TOPK = 256
Q_BLOCK = 128
N_BUCKETS = 32
RW_HEADS = 8
RW_HEAD = 64
RW_WIDTH = RW_HEADS * RW_HEAD
DECAY_LORA = 64
AAA_LORA = 64
GATE_LORA = 128
RW_COLS = 3 * RW_WIDTH + DECAY_LORA + AAA_LORA + GATE_LORA
GN_EPS = 64e-5
D_FF = 4 * D_MODEL
NORM_EPS = 1e-6
NEG = -1e30

LANES = 128
N_PAIRS = RW_HEADS // 2
VMEM_LIMIT = 56 * 1024 * 1024

CHUNK_PIECES = 1
SEG_PIECES = 2

_NN, _NT, _TN = (1, 0), (1, 1), (0, 0)


def _split(x, n):
    pieces = []
    rem = x
    for i in range(n):
        p = rem.astype(BF16)
        pieces.append(p)
        if i + 1 < n:
            rem = rem - p.astype(F32)
    return tuple(pieces)


def _mm(a, b, dims=_NN):
    n = max(len(a), len(b))
    terms = [(i, j) for i in range(len(a)) for j in range(len(b)) if i + j < n]
    ca, cb = dims
    lhs = jnp.concatenate([a[i] for i, _ in terms], axis=ca) if len(terms) > 1 else a[0]
    rhs = jnp.concatenate([b[j] for _, j in terms], axis=cb) if len(terms) > 1 else b[0]
    return lax.dot_general(lhs, rhs, (((ca,), (cb,)), ((), ())), preferred_element_type=F32)


def _lane_iota(shape):
    return lax.broadcasted_iota(jnp.int32, shape, len(shape) - 1)


def _row_iota(shape):
    return lax.broadcasted_iota(jnp.int32, shape, len(shape) - 2)


def _bdrows(x, half):
    low = (_lane_iota(x.shape) % (2 * half)) < half
    return jnp.concatenate([jnp.where(low, x, 0.0), jnp.where(low, 0.0, x)], axis=0)


def _rwkv_kernel(x_ref, shift0_ref, h0_ref, mu_ref, w0_ref, a0_ref, kk_ref, ka_ref, rk_ref,
                 lnw_ref, lnb_ref, wlora_ref, wg_ref,
                 out_ref, hT_ref, xbuf, hst, *, C, n_sq, nb):
    c = pl.program_id(1)

    @pl.when(c == 0)
    def _():
        for n in range(nb):
            xbuf[n, 7:8, :] = shift0_ref[n]
        hst[...] = h0_ref[...]

    W = RW_WIDTH
    R = nb * C
    prevs = []
    for n in range(nb):
        xn = x_ref[n]
        xbuf[n, 8:8 + C, :] = xn
        prevs.append(xbuf[n, 7:7 + C, :])
        xbuf[n, 7:8, :] = xn[C - 1:C, :]
    x = x_ref[...].reshape(R, RW_COLS)
    prev = jnp.concatenate(prevs, axis=0)
    xs = x + (prev - x) * mu_ref[...]
    r = xs[:, 0:W]
    k = xs[:, W:2 * W]
    v = xs[:, 2 * W:3 * W]
    da = xs[:, 3 * W:3 * W + LANES]
    gd = xs[:, 3 * W + LANES:3 * W + 2 * LANES]

    t_in = jnp.where(_lane_iota(da.shape) < DECAY_LORA, jnp.tanh(da), da)
    lo = jnp.dot(t_in.astype(BF16), wlora_ref[...], preferred_element_type=F32)
    z = -(w0_ref[...] + lo[:, 0:W])
    softplus = jnp.maximum(z, 0.0) + jnp.log1p(jnp.exp(-jnp.abs(z)))
    w = -softplus - 0.5
    ld = -jnp.exp(w)
    a = jax.nn.sigmoid(a0_ref[...] + lo[:, W:2 * W])
    g = jnp.dot(jax.nn.sigmoid(gd).astype(BF16), wg_ref[...], preferred_element_type=F32)

    seg = (_row_iota((LANES, LANES)) // RW_HEAD == _lane_iota((LANES, LANES)) // RW_HEAD).astype(BF16)

    def segsum(t):
        return jnp.concatenate(
            [_mm(_split(t[:, p * LANES:(p + 1) * LANES], SEG_PIECES), (seg,)) for p in range(N_PAIRS)],
            axis=1)

    kk0 = k * kk_ref[...]
    kk = kk0 / jnp.maximum(jnp.sqrt(segsum(kk0 * kk0)), 1e-12)
    k2 = k * (1.0 + (a - 1.0) * ka_ref[...])
    bb = kk * a

    ti, si = _row_iota((R, R)), _lane_iota((R, R))
    tri = jnp.logical_and(si <= ti, si // C == ti // C).astype(BF16)
    lp = _mm((tri,), _split(ld, 3))
    lpp = lp - ld
    rows_of = lambda i: jnp.concatenate(
        [jnp.broadcast_to(lp[n * C + i:n * C + i + 1, :], (C, W)) for n in range(nb)], axis=0)
    lpc = rows_of(C - 1)
    mid = rows_of(C // 2 - 1)
    r_abs = r * jnp.exp(lp)
    kap_abs = kk * jnp.exp(lpp)
    r_mid = r * jnp.exp(lp - mid)
    kap_mid = kk * jnp.exp(lpp - mid)
    e_mid = jnp.exp(mid - lp)
    b_mid = bb * e_mid
    k_mid = k2 * e_mid
    e_end = jnp.exp(lpc - lp)
    b_end = bb * e_end
    k_end = k2 * e_end
    p_end = jnp.exp(lpc)

    tt = _row_iota((C, 2 * C))
    ss = _lane_iota((C, 2 * C)) % C
    strict = ss < tt
    incl = ss <= tt
    eye_p = (ss == tt).astype(F32)
    jrow = _row_iota((LANES, LANES))
    jlane = _lane_iota((LANES, LANES))
    same_head = (jrow // RW_HEAD) == (jlane // RW_HEAD)
    eye_j = jrow == jlane

    sp = lambda t: _split(t, CHUNK_PIECES)
    bd = lambda pieces, half: tuple(_bdrows(t, half) for t in pieces)

    chains = [(n, p) for n in range(nb) for p in range(N_PAIRS)]
    blk = lambda t, ch: t[ch[0] * C:(ch[0] + 1) * C, ch[1] * LANES:(ch[1] + 1) * LANES]
    each = lambda f: [f(i, ch) for i, ch in enumerate(chains)]

    A = each(lambda i, ch: _mm(
        sp(jnp.concatenate([blk(kap_mid, ch), blk(r_mid, ch)], axis=0)),
        sp(jnp.concatenate([_bdrows(blk(b_mid, ch), RW_HEAD), _bdrows(blk(k_mid, ch), RW_HEAD)],
                           axis=0)), _NT))
    n_kb = each(lambda i, ch: jnp.where(strict, A[i][0:C, 0:2 * C], 0.0))
    a_kk = each(lambda i, ch: jnp.where(strict, A[i][0:C, 2 * C:4 * C], 0.0))
    a_rb = each(lambda i, ch: jnp.where(incl, A[i][C:2 * C, 0:2 * C], 0.0))
    a_rk = each(lambda i, ch: jnp.where(incl, A[i][C:2 * C, 2 * C:4 * C], 0.0))
    T = each(lambda i, ch: eye_p - n_kb[i])
    pw = each(lambda i, ch: sp(n_kb[i]))
    for _ in range(n_sq):
        pw = each(lambda i, ch: sp(_mm(pw[i], bd(pw[i], C))))
        T = each(lambda i, ch: T[i] + _mm(sp(T[i]), bd(pw[i], C)))
    v_p = each(lambda i, ch: sp(blk(v, ch)))
    av = each(lambda i, ch: _mm(sp(jnp.concatenate([a_kk[i], a_rk[i]], axis=0)),
                                bd(v_p[i], RW_HEAD)))
    wu = each(lambda i, ch: sp(_mm(
        sp(T[i]), bd(sp(jnp.concatenate([blk(kap_abs, ch), av[i][0:C]], axis=1)), RW_HEAD))))
    qy = each(lambda i, ch: jnp.concatenate([blk(r_abs, ch), av[i][C:2 * C]], axis=1)
              - _mm(sp(a_rb[i]), bd(wu[i], RW_HEAD)))
    bw = each(lambda i, ch: _mm(sp(blk(b_end, ch)), wu[i], _TN))
    kv = each(lambda i, ch: _mm(_split(blk(k_end, ch), 2), _split(blk(v, ch), 2), _TN))
    G = each(lambda i, ch: jnp.where(same_head, kv[i] - bw[i][:, LANES:2 * LANES], 0.0))
    M = each(lambda i, ch: jnp.where(same_head, bw[i][:, 0:LANES], 0.0))
    p_col = each(lambda i, ch: jnp.sum(jnp.where(eye_j, blk(p_end, ch)[0:1], 0.0), axis=1,
                                       keepdims=True))
    h0f = each(lambda i, ch: hst[ch[0], ch[1]])
    h0 = each(lambda i, ch: sp(h0f[i]))
    yc_ = each(lambda i, ch: _mm(sp(qy[i][:, 0:LANES]), h0[i]) + qy[i][:, LANES:2 * LANES])
    for i, ch in enumerate(chains):
        hst[ch[0], ch[1]] = p_col[i] * h0f[i] - _mm(sp(M[i]), h0[i]) + G[i]
    y = jnp.concatenate([jnp.concatenate(yc_[n * N_PAIRS:(n + 1) * N_PAIRS], axis=1)
                         for n in range(nb)], axis=0)

    mean = segsum(y) * (1.0 / RW_HEAD)
    yc = y - mean
    var = segsum(yc * yc) * (1.0 / RW_HEAD)
    yn = yc * lax.rsqrt(var + GN_EPS) * lnw_ref[...] + lnb_ref[...]
    bonus = segsum(r * k2 * rk_ref[...]) * v
    out_ref[...] = ((yn + bonus) * g).astype(out_ref.dtype).reshape(nb, C, W)

    @pl.when(c == pl.num_programs(1) - 1)
    def _():
        hT_ref[...] = hst[...]


def _rwkv_call(p_rw, shift0, h0_bd, prm, C, nb):
    B, T, _ = p_rw.shape
    n_sq = int(math.log2(C)) - 1
    assert 2 ** (n_sq + 1) == C and T % C == 0 and B % nb == 0
    row = lambda a: a.reshape(1, -1).astype(F32)
    vecs = [row(prm['mu_shift']), row(prm['w0']), row(prm['a0']), row(prm['k_k']), row(prm['k_a']),
            row(prm['r_k']), row(prm['lnx_w']), row(prm['lnx_b'])]
    wlora = jnp.zeros((LANES, 2 * RW_WIDTH), F32)
    wlora = wlora.at[0:DECAY_LORA, 0:RW_WIDTH].set(prm['w_decay_up'])
    wlora = wlora.at[DECAY_LORA:, RW_WIDTH:].set(prm['w_a_up']).astype(BF16)
    wg = prm['w_g_up'].astype(BF16)
    const = lambda shape: pl.BlockSpec(shape, lambda b, c: (0,) * len(shape))
    kern = functools.partial(_rwkv_kernel, C=C, n_sq=n_sq, nb=nb)
    return pl.pallas_call(
        kern,
        grid=(B // nb, T // C),
        in_specs=[pl.BlockSpec((nb, C, RW_COLS), lambda b, c: (b, c, 0)),
                  pl.BlockSpec((nb, 1, RW_COLS), lambda b, c: (b, 0, 0)),
                  pl.BlockSpec((nb, N_PAIRS, LANES, LANES), lambda b, c: (b, 0, 0, 0)),
                  const((1, RW_COLS))] + [const((1, RW_WIDTH))] * 7
                 + [const((LANES, 2 * RW_WIDTH)), const((GATE_LORA, RW_WIDTH))],
        out_specs=[pl.BlockSpec((nb, C, RW_WIDTH), lambda b, c: (b, c, 0)),
                   pl.BlockSpec((nb, N_PAIRS, LANES, LANES), lambda b, c: (b, 0, 0, 0))],
        out_shape=[jax.ShapeDtypeStruct((B, T, RW_WIDTH), BF16),
                   jax.ShapeDtypeStruct((B, N_PAIRS, LANES, LANES), F32)],
        scratch_shapes=[pltpu.VMEM((nb, 8 + C, RW_COLS), F32),
                        pltpu.VMEM((nb, N_PAIRS, LANES, LANES), F32)],
        compiler_params=pltpu.CompilerParams(dimension_semantics=("arbitrary", "arbitrary"),
                                             vmem_limit_bytes=VMEM_LIMIT),
        name="rwkv7_chunked",
    )(p_rw, shift0, h0_bd, *vecs, wlora, wg)


def _state_to_bd(wkv):
    B = wkv.shape[0]
    h = jnp.swapaxes(wkv, -1, -2).reshape(B, N_PAIRS, 2, RW_HEAD, RW_HEAD)
    z = jnp.zeros_like(h[:, :, 0])
    top = jnp.concatenate([h[:, :, 0], z], axis=-1)
    bot = jnp.concatenate([z, h[:, :, 1]], axis=-1)
    return jnp.concatenate([top, bot], axis=-2)


def _bd_to_state(hbd):
    B = hbd.shape[0]
    h0 = hbd[:, :, 0:RW_HEAD, 0:RW_HEAD]
    h1 = hbd[:, :, RW_HEAD:, RW_HEAD:]
    h = jnp.stack([h0, h1], axis=2).reshape(B, RW_HEADS, RW_HEAD, RW_HEAD)
    return jnp.swapaxes(h, -1, -2)


_INT_MIN = -2 ** 31
_NEG_KEY = int(np.float32(NEG).view(np.int32)) ^ 0x7FFFFFFF
_T5_STEPS = ((91, 14), (64, 13), (46, 12), (32, 11), (23, 10), (16, 9), (12, 8), (8, 7), (7, 6),
             (6, 5), (5, 4), (4, 3), (3, 2), (2, 1), (1, 0))
_T5_FAR = 15


def _half_masked(x, half):
    low = (_lane_iota(x.shape) % LANES) < HEAD_DIM
    keep = low if half == 0 else jnp.logical_not(low)
    return jnp.where(keep, x, jnp.zeros_like(x))


def _dup_half(x, half):
    xr = pltpu.roll(x, HEAD_DIM, axis=1)
    low = _lane_iota(x.shape) < HEAD_DIM
    return jnp.where(low, x, xr) if half == 0 else jnp.where(low, xr, x)


def _dot_nt(a, b):
    return lax.dot_general(a, b, (((1,), (1,)), ((), ())), preferred_element_type=F32)


def _dsa_kernel(*refs, Tq, P, Tn, Lp, topk, n_cls):
    if P > 0:
        (q_ref, qi_ref, wi_ref, hk_ref, hv_ref, hki_ref, nk_ref, nv_ref, nki_ref, rb_ref,
         att_ref, kd, vd, kid, lg, tbl, sbias) = refs
    else:
        (q_ref, qi_ref, wi_ref, nk_ref, nv_ref, nki_ref, rb_ref,
         att_ref, kd, vd, kid, lg, tbl, sbias) = refs
    b = pl.program_id(0)
    i = pl.program_id(1)
    L = P + Tn

    @pl.when(jnp.logical_and(b == 0, i == 0))
    def _():
        shape = (Tq, 2 * LANES)
        rel = _lane_iota(shape) - LANES - _row_iota(shape)
        n = jnp.abs(rel)
        for h in range(N_HEADS):
            far = rb_ref[_T5_FAR, h]
            neg = jnp.full(shape, far, F32)
            pos = jnp.full(shape, rb_ref[_T5_FAR + N_BUCKETS // 2, h], F32)
            for bound, bucket in _T5_STEPS:
                neg = jnp.where(n < bound, rb_ref[bucket, h], neg)
                pos = jnp.where(n < bound, rb_ref[bucket + N_BUCKETS // 2, h], pos)
            tbl[h] = jnp.where(rel > 0, pos, neg) - far

    @pl.when(i == 0)
    def _():
        def stage(k_src, v_src, ki_src, r0, n):
            kf = k_src[0]
            vf = v_src[0]
            kif = ki_src[0]
            for j in range(N_KV_HEADS):
                kd[j, r0:r0 + n, :] = _dup_half(kf, j).astype(BF16)
                vd[j, r0:r0 + n, :] = _dup_half(vf, j).astype(BF16)
            kid[r0:r0 + n, :] = jnp.concatenate([kif, kif], axis=1).astype(BF16)

        if P > 0:
            stage(hk_ref, hv_ref, hki_ref, 0, P)
        stage(nk_ref, nv_ref, nki_ref, P, Tn)
        if Lp > L:
            z = jnp.zeros((Lp - L, LANES), BF16)
            for j in range(N_KV_HEADS):
                kd[j, L:Lp, :] = z
                vd[j, L:Lp, :] = z
            kid[L:Lp, :] = z

    stride = Lp // n_cls
    k_hi = P + (i + 1) * Tq
    cls = (k_hi - 1) // stride
    for c in range(n_cls):
        pl.when(cls == c)(functools.partial(
            _dsa_block, q_ref, qi_ref, wi_ref, att_ref, kd, vd, kid, lg, tbl, sbias,
            i=i, Tq=Tq, P=P, L=L, Lk=(c + 1) * stride, topk=topk))


def _dsa_block(q_ref, qi_ref, wi_ref, att_ref, kd, vd, kid, lg, tbl, sbias, *, i, Tq, P, L, Lk, topk):
    q = q_ref[0]
    qi = qi_ref[0]
    key_major = Tq % LANES == 0
    kax = 0 if key_major else 1
    shape = (Lk, Tq) if key_major else (Tq, Lk)
    qvec = (1, Tq) if key_major else (Tq, 1)
    wi = wi_ref[0]

    kidv = kid[0:Lk, :]
    head_q = lambda h: _half_masked(qi[:, (h // 2) * LANES:(h // 2 + 1) * LANES], h % 2)
    score = jnp.zeros(shape, F32)
    if key_major:
        for pr in range(N_IDX_HEADS // 2):
            s2 = _dot_nt(kidv, jnp.concatenate([head_q(2 * pr), head_q(2 * pr + 1)], axis=0))
            for u in range(2):
                h = 2 * pr + u
                score = score + wi[h:h + 1, :] * jnp.maximum(s2[:, u * Tq:(u + 1) * Tq], 0.0)
    else:
        for h in range(N_IDX_HEADS):
            score = score + wi[:, h:h + 1] * jnp.maximum(_dot_nt(head_q(h), kidv), 0.0)

    s_pos = lax.broadcasted_iota(jnp.int32, shape, kax)
    q_pos = P + i * Tq + lax.broadcasted_iota(jnp.int32, shape, 1 - kax)
    adm = jnp.logical_and(s_pos // CHUNK <= q_pos // CHUNK, s_pos < L)
    score = jnp.where(adm, score, NEG)

    def count(mask):
        ones = jnp.where(mask, 1.0, 0.0)
        if key_major:
            ones = jnp.sum(ones.reshape(Lk // LANES, LANES, Tq), axis=0)
        return jnp.sum(ones, axis=kax, keepdims=True)

    def key_to_float(key):
        return pltpu.bitcast(jnp.where(key < 0, key ^ jnp.int32(0x7FFFFFFF), key), F32)

    kf = float(topk)
    ans = jnp.where(count(score >= 0.0) >= kf, jnp.int32(0), jnp.int32(_INT_MIN))

    def bit_step(it, ans):
        cand = ans | (jnp.int32(1) << (30 - it))
        return jnp.where(count(score >= key_to_float(cand)) >= kf, cand, ans)

    thr_key = lax.fori_loop(0, 31, bit_step, ans)
    thr_key = jnp.maximum(thr_key, jnp.int32(_NEG_KEY))
    thr = key_to_float(thr_key)
    gt = score > thr
    eq = score == thr
    need = kf - count(gt)
    tied = jnp.logical_and(count(eq) > need, thr > NEG)
    any_tied = jnp.max(jnp.where(tied, 1.0, 0.0)) > 0.0

    n_idx_bits = (Lk - 1).bit_length()

    def idx_search():
        def idx_step(it, x):
            cand = x | (jnp.int32(1) << (n_idx_bits - 1 - it))
            return jnp.where(count(jnp.logical_and(eq, s_pos < cand)) < need, cand, x)
        return lax.fori_loop(0, n_idx_bits, idx_step, jnp.zeros(qvec, jnp.int32))

    last = lax.cond(any_tied, idx_search, lambda: jnp.full(qvec, Lk, jnp.int32))
    sel = jnp.logical_and(jnp.logical_or(gt, jnp.logical_and(eq, s_pos <= last)), adm)
    sb = jnp.where(sel, 0.0, NEG)
    sbias[:, 0:Lk] = sb.T if key_major else sb

    blk_end = (P + i * Tq) // LANES + 1
    two_blocks = blk_end >= 2
    start = pl.multiple_of(jnp.maximum(blk_end - 2, 0) * LANES, LANES)
    for j in range(N_KV_HEADS):
        rows = []
        for g in range(GROUP):
            h = j * GROUP + g
            rows.append(_half_masked(q[:, (h // 2) * LANES:(h // 2 + 1) * LANES], h % 2))
        lhs = jnp.concatenate(rows, axis=0)
        lg[:, 0:Lk] = _dot_nt(lhs, kd[j, 0:Lk, :])

        @pl.when(two_blocks)
        def _():
            for g in range(GROUP):
                r = slice(g * Tq, (g + 1) * Tq)
                lg[r, pl.ds(start, 2 * LANES)] = lg[r, pl.ds(start, 2 * LANES)] + tbl[j * GROUP + g]

        @pl.when(jnp.logical_not(two_blocks))
        def _():
            for g in range(GROUP):
                r = slice(g * Tq, (g + 1) * Tq)
                lg[r, 0:LANES] = lg[r, 0:LANES] + tbl[j * GROUP + g][:, LANES:2 * LANES]

        sb = sbias[:, 0:Lk]
        os_ = []
        for g in range(GROUP):
            logits = lg[g * Tq:(g + 1) * Tq, 0:Lk] + sb
            m = jnp.max(logits, axis=1, keepdims=True)
            p = jnp.exp(logits - m)
            l = jnp.sum(p, axis=1, keepdims=True)
            os_.append(jnp.dot(p.astype(BF16), vd[j, 0:Lk, :], preferred_element_type=F32) / l)
        low = _lane_iota((Tq, LANES)) < HEAD_DIM
        for pp in range(GROUP // 2):
            c0 = (j * GROUP // 2 + pp) * LANES
            att_ref[0, :, c0:c0 + LANES] = jnp.where(low, os_[2 * pp], os_[2 * pp + 1]).astype(att_ref.dtype)


def _dsa_call(q, qi, wi, hist, new, rel_bias, Tq):
    B, T, _ = q.shape
    Tn = new[0].shape[1]
    P = 0 if hist is None else hist[0].shape[1]
    L = P + Tn
    Lp = -(-L // LANES) * LANES
    topk = min(MAX_TOPK, L // 4)
    assert T % Tq == 0 and Tn == T and P % LANES == 0 and (Tq % LANES == 0 or T == Tq)
    qspec = lambda w: pl.BlockSpec((1, Tq, w), lambda b, i: (b, i, 0))
    full = lambda n, w: pl.BlockSpec((1, n, w), lambda b, i: (b, 0, 0))
    if Tq % LANES == 0:
        wi = jnp.swapaxes(wi, 1, 2)
        wi_spec = pl.BlockSpec((1, N_IDX_HEADS, Tq), lambda b, i: (b, 0, i))
    else:
        wi_spec = qspec(N_IDX_HEADS)
    in_specs = [qspec(ATT_WIDTH), qspec(N_IDX_HEADS * IDX_DIM), wi_spec]
    args = [q, qi, wi]
    if P > 0:
        in_specs += [full(P, KV_WIDTH), full(P, KV_WIDTH), full(P, IDX_DIM)]
        args += list(hist)
    in_specs += [full(Tn, KV_WIDTH), full(Tn, KV_WIDTH), full(Tn, IDX_DIM),
                 pl.BlockSpec(memory_space=pltpu.SMEM)]
    args += list(new) + [rel_bias.astype(F32)]
    n_cls = Lp // 512 if (P == 0 and Lp % 512 == 0) else 1
    kern = functools.partial(_dsa_kernel, Tq=Tq, P=P, Tn=Tn, Lp=Lp, topk=topk, n_cls=n_cls)
    return pl.pallas_call(
        kern,
        grid=(B, T // Tq),
        in_specs=in_specs,
        out_specs=pl.BlockSpec((1, Tq, ATT_WIDTH), lambda b, i: (b, i, 0)),
        out_shape=jax.ShapeDtypeStruct((B, T, ATT_WIDTH), BF16),
        scratch_shapes=[pltpu.VMEM((N_KV_HEADS, Lp, LANES), BF16),
                        pltpu.VMEM((N_KV_HEADS, Lp, LANES), BF16),
                        pltpu.VMEM((Lp, LANES), BF16),
                        pltpu.VMEM((GROUP * Tq, Lp), F32),
                        pltpu.VMEM((N_HEADS, Tq, 2 * LANES), F32),
                        pltpu.VMEM((Tq, Lp), F32)],
        compiler_params=pltpu.CompilerParams(dimension_semantics=("arbitrary", "arbitrary"),
                                             vmem_limit_bytes=VMEM_LIMIT),
        name="dsa_attention",
    )(*args)


ATT_COLS = 2 * ATT_WIDTH + 2 * KV_WIDTH + LANES
_C_Q, _C_QI, _C_K, _C_V, _C_KI = 0, 512, 1024, 1152, 1280
_C_WI = _C_KI + IDX_DIM


def _rms(x, g):
    return x * lax.rsqrt(jnp.mean(x * x, axis=-1, keepdims=True) + NORM_EPS) * g


def _inproj_kernel(x_ref, g_ref, wa_ref, wr_ref, q_ref, qi_ref, k_ref, v_ref, ki_ref, wi_ref, prw_ref):
    h = _rms(x_ref[...], g_ref[...]).astype(BF16)
    pa = jnp.dot(h, wa_ref[...], preferred_element_type=F32)
    q_ref[...] = (pa[:, _C_Q:_C_Q + ATT_WIDTH] * (HEAD_DIM ** -0.5)).astype(BF16)
    qi_ref[...] = (pa[:, _C_QI:_C_QI + ATT_WIDTH] * (IDX_DIM ** -0.5)).astype(BF16)
    k_ref[...] = pa[:, _C_K:_C_K + KV_WIDTH]
    v_ref[...] = pa[:, _C_V:_C_V + KV_WIDTH]
    ki_ref[...] = pa[:, _C_KI:_C_KI + IDX_DIM]
    wi_ref[...] = pa[:, _C_WI:_C_WI + N_IDX_HEADS] * (N_IDX_HEADS ** -0.5)
    prw_ref[...] = jnp.dot(h, wr_ref[...], preferred_element_type=F32)


def _const_spec(shape):
    return pl.BlockSpec(shape, lambda i: (0,) * len(shape), pipeline_mode=pl.Buffered(1))


def _inproj_call(x2d, g, wa, wr, tm):
    N = x2d.shape[0]
    assert N % tm == 0
    rows = lambda w: pl.BlockSpec((tm, w), lambda i: (i, 0))
    widths = [ATT_WIDTH, ATT_WIDTH, KV_WIDTH, KV_WIDTH, IDX_DIM, N_IDX_HEADS, RW_COLS]
    dtypes = [BF16, BF16, F32, F32, F32, F32, F32]
    return pl.pallas_call(
        _inproj_kernel,
        grid=(N // tm,),
        in_specs=[rows(D_MODEL), _const_spec((1, D_MODEL)), _const_spec((D_MODEL, ATT_COLS)),
                  _const_spec((D_MODEL, RW_COLS))],
        out_specs=[rows(w) for w in widths],
        out_shape=[jax.ShapeDtypeStruct((N, w), d) for w, d in zip(widths, dtypes)],
        compiler_params=pltpu.CompilerParams(dimension_semantics=("arbitrary",),
                                             vmem_limit_bytes=VMEM_LIMIT),
        name="rmsnorm_inproj",
    )(x2d, g, wa, wr)


def _tail_kernel(x_ref, att_ref, rw_ref, g1_ref, wgate_ref, wao_ref, wro_ref, wout_ref,
                 g2_ref, wff1_ref, wff2_ref, gf_ref, y_ref):
    x = x_ref[...]
    h = _rms(x, g1_ref[...]).astype(BF16)
    gates = jax.nn.sigmoid(jnp.dot(h, wgate_ref[...], preferred_element_type=F32))
    z_a = jnp.dot(att_ref[...], wao_ref[...], preferred_element_type=F32)
    z_b = jnp.dot(rw_ref[...], wro_ref[...], preferred_element_type=F32)
    mixed = gates[:, 0:D_MODEL] * z_a + gates[:, D_MODEL:2 * D_MODEL] * z_b
    x1 = x + jnp.dot(mixed.astype(BF16), wout_ref[...], preferred_element_type=F32)
    h2 = _rms(x1, g2_ref[...]).astype(BF16)
    f = jnp.maximum(jnp.dot(h2, wff1_ref[...], preferred_element_type=F32), 0.0)
    x2 = x1 + jnp.dot((f * f).astype(BF16), wff2_ref[...], preferred_element_type=F32)
    y_ref[...] = _rms(x2, gf_ref[...])


def _tail_call(x2d, att, rw, g1, wgate, wao, wro, wout, g2, wff1, wff2, gf, tm):
    N = x2d.shape[0]
    assert N % tm == 0
    rows = lambda w: pl.BlockSpec((tm, w), lambda i: (i, 0))
    return pl.pallas_call(
        _tail_kernel,
        grid=(N // tm,),
        in_specs=[rows(D_MODEL), rows(ATT_WIDTH), rows(RW_WIDTH), _const_spec((1, D_MODEL)),
                  _const_spec((D_MODEL, 2 * D_MODEL)), _const_spec((ATT_WIDTH, D_MODEL)),
                  _const_spec((RW_WIDTH, D_MODEL)), _const_spec((D_MODEL, D_MODEL)),
                  _const_spec((1, D_MODEL)), _const_spec((D_MODEL, D_FF)), _const_spec((D_FF, D_MODEL)),
                  _const_spec((1, D_MODEL))],
        out_specs=rows(D_MODEL),
        out_shape=jax.ShapeDtypeStruct((N, D_MODEL), F32),
        compiler_params=pltpu.CompilerParams(dimension_semantics=("arbitrary",),
                                             vmem_limit_bytes=VMEM_LIMIT),
        name="merge_out_mlp",
    )(x2d, att, rw, g1, wgate, wao, wro, wout, g2, wff1, wff2, gf)


_IN_SPLITS = (ATT_WIDTH, KV_WIDTH, KV_WIDTH, N_IDX_HEADS * IDX_DIM, IDX_DIM, N_IDX_HEADS,
              RW_COLS, D_MODEL, D_MODEL)


def _split_w_in(w_in):
    offs = [0]
    for s in _IN_SPLITS:
        offs.append(offs[-1] + s)
    col = lambda n: w_in[:, offs[n]:offs[n + 1]]
    pad = jnp.zeros((D_MODEL, ATT_COLS - _C_WI - N_IDX_HEADS), w_in.dtype)
    wa = jnp.concatenate([col(0), col(3), col(1), col(2), col(4), col(5), pad], axis=1).astype(BF16)
    wr = col(6).astype(BF16)
    wgate = jnp.concatenate([col(7), col(8)], axis=1).astype(BF16)
    return wa, wr, wgate


def _branch(x, hist, wkv0, shift0, rel_bias, prm, w, tm_in, tm_tail, Tq, C, nb_rw):
    B, T, _ = x.shape
    x2d = x.reshape(B * T, D_MODEL)
    q, qi, k, v, ki, wi, p_rw = _inproj_call(x2d, prm['g1'], w['wa'], w['wr'], tm_in)
    r3 = lambda a: a.reshape(B, T, a.shape[-1])
    rw, hT = _rwkv_call(r3(p_rw), shift0, _state_to_bd(wkv0), prm, C, nb_rw)
    att = _dsa_call(r3(q), r3(qi), r3(wi), hist, (r3(k), r3(v), r3(ki)), rel_bias, Tq)
    y = _tail_call(x2d, att.reshape(B * T, ATT_WIDTH), rw.reshape(B * T, RW_WIDTH), prm['g1'],
                   w['wgate'], w['wao'], w['wro'], w['wout'], prm['g2'], w['wff1'], w['wff2'],
                   prm['gf'], tm_tail)
    new_k = k.reshape(1, B, T, N_KV_HEADS, HEAD_DIM)
    new_v = v.reshape(1, B, T, N_KV_HEADS, HEAD_DIM)
    new_ki = ki.reshape(1, B, T, IDX_DIM)
    wkv = _bd_to_state(hT)[None]
    shift = r3(p_rw)[:, T - 1:T, :][None]
    return y.reshape(B, T, D_MODEL), new_k, new_v, new_ki, wkv, shift


def kernel(x_prompt, x_sample, cache_k, cache_v, cache_kidx, state_wkv, state_shift, rel_bias,
           norm1_g, w_in, w_att_out, mu_shift, w0, w_decay_up, a0, w_a_up, w_g_up, k_k, k_a, r_k,
           lnx_w, lnx_b, w_rw_out, w_out, norm2_g, w_ff1, w_ff2, norm_f_g):
    assert w_in.shape[0] == 1, "single-layer step"
    wa, wr, wgate = _split_w_in(w_in[0])
    w = dict(wa=wa, wr=wr, wgate=wgate, wao=w_att_out[0].astype(BF16), wro=w_rw_out[0].astype(BF16),
             wout=w_out[0].astype(BF16), wff1=w_ff1[0].astype(BF16), wff2=w_ff2[0].astype(BF16))
    prm = dict(mu_shift=mu_shift[0], w0=w0[0], w_decay_up=w_decay_up[0], a0=a0[0], w_a_up=w_a_up[0],
               w_g_up=w_g_up[0], k_k=k_k[0], k_a=k_a[0], r_k=r_k[0].reshape(-1), lnx_w=lnx_w[0],
               lnx_b=lnx_b[0], g1=norm1_g[0].reshape(1, -1), g2=norm2_g[0].reshape(1, -1),
               gf=norm_f_g.reshape(1, -1))
    Bp, Tp, _ = x_prompt.shape
    Bs, Ts, _ = x_sample.shape
    past = cache_k.shape[2]
    zero_state = jnp.zeros((Bp, RW_HEADS, RW_HEAD, RW_HEAD), F32)
    zero_shift = jnp.zeros((Bp, 1, RW_COLS), F32)
    outs_p = _branch(x_prompt, None, zero_state, zero_shift, rel_bias, prm, w,
                     tm_in=512, tm_tail=256, Tq=Q_BLOCK, C=CHUNK, nb_rw=2)
    hist = (cache_k[0].reshape(Bs, past, KV_WIDTH), cache_v[0].reshape(Bs, past, KV_WIDTH),
            cache_kidx[0])
    outs_s = _branch(x_sample, hist, state_wkv[0], state_shift[0], rel_bias, prm, w,
                     tm_in=Bs * Ts, tm_tail=256, Tq=Ts, C=Ts, nb_rw=4)
    return (outs_p[0], outs_s[0]) + outs_p[1:] + outs_s[1:]
```

```python
import functools
import math

import jax
import jax.numpy as jnp
import numpy as np
from jax import lax
from jax.experimental import pallas as pl
from jax.experimental.pallas import tpu as pltpu

F32 = jnp.float32
BF16 = jnp.bfloat16

D_MODEL = 1024
CHUNK = 64
N_HEADS = 8
N_KV_HEADS = 2
HEAD_DIM = 64
GROUP = N_HEADS // N_KV_HEADS
ATT_WIDTH = N_HEADS * HEAD_DIM
KV_WIDTH = N_KV_HEADS * HEAD_DIM
N_IDX_HEADS = 8
IDX_DIM = 64
MAX_TOPK = 256
Q_BLOCK = 128
N_BUCKETS = 32
RW_HEADS = 8
RW_HEAD = 64
RW_WIDTH = RW_HEADS * RW_HEAD
DECAY_LORA = 64
AAA_LORA = 64
GATE_LORA = 128
RW_COLS = 3 * RW_WIDTH + DECAY_LORA + AAA_LORA + GATE_LORA
GN_EPS = 64e-5
D_FF = 4 * D_MODEL
NORM_EPS = 1e-6
NEG = -1e30

LANES = 128
N_PAIRS = RW_HEADS // 2
VMEM_LIMIT = 56 * 1024 * 1024

CHUNK_PIECES = 1
SEG_PIECES = 2

_NN, _NT, _TN = (1, 0), (1, 1), (0, 0)


def _split(x, n):
    pieces = []
    rem = x
    for i in range(n):
        p = rem.astype(BF16)
        pieces.append(p)
        if i + 1 < n:
            rem = rem - p.astype(F32)
    return tuple(pieces)


def _mm(a, b, dims=_NN):
    n = max(len(a), len(b))
    terms = [(i, j) for i in range(len(a)) for j in range(len(b)) if i + j < n]
    ca, cb = dims
    lhs = jnp.concatenate([a[i] for i, _ in terms], axis=ca) if len(terms) > 1 else a[0]
    rhs = jnp.concatenate([b[j] for _, j in terms], axis=cb) if len(terms) > 1 else b[0]
    return lax.dot_general(lhs, rhs, (((ca,), (cb,)), ((), ())), preferred_element_type=F32)


def _lane_iota(shape):
    return lax.broadcasted_iota(jnp.int32, shape, len(shape) - 1)


def _row_iota(shape):
    return lax.broadcasted_iota(jnp.int32, shape, len(shape) - 2)


def _bdrows(x, width, heads=2):
    head = (_lane_iota(x.shape) % (heads * width)) // width
    return jnp.concatenate([jnp.where(head == h, x, 0.0) for h in range(heads)], axis=0)


def _rwkv_kernel(x_ref, shift0_ref, h0_ref, mu_ref, w0_ref, a0_ref, kk_ref, ka_ref, rk_ref,
                 lnw_ref, lnb_ref, wlora_ref, wg_ref,
                 out_ref, hT_ref, xbuf, hst, *, C, n_sq, nb):
    c = pl.program_id(1)

    @pl.when(c == 0)
    def _():
        for n in range(nb):
            xbuf[n, 7:8, :] = shift0_ref[n]
        hst[...] = h0_ref[...]

    W = RW_WIDTH
    R = nb * C
    prevs = []
    for n in range(nb):
        xn = x_ref[n]
        xbuf[n, 8:8 + C, :] = xn
        prevs.append(xbuf[n, 7:7 + C, :])
        xbuf[n, 7:8, :] = xn[C - 1:C, :]
    x = x_ref[...].reshape(R, RW_COLS)
    prev = jnp.concatenate(prevs, axis=0)
    xs = x + (prev - x) * mu_ref[...]
    r = xs[:, 0:W]
    k = xs[:, W:2 * W]
    v = xs[:, 2 * W:3 * W]
    da = xs[:, 3 * W:3 * W + LANES]
    gd = xs[:, 3 * W + LANES:3 * W + 2 * LANES]

    t_in = jnp.where(_lane_iota(da.shape) < DECAY_LORA, jnp.tanh(da), da)
    lo = jnp.dot(t_in.astype(BF16), wlora_ref[...], preferred_element_type=F32)
    ld = -math.exp(-0.5) * jax.nn.sigmoid(w0_ref[...] + lo[:, 0:W])
    a = jax.nn.sigmoid(a0_ref[...] + lo[:, W:2 * W])
    g = jnp.dot(jax.nn.sigmoid(gd).astype(BF16), wg_ref[...], preferred_element_type=F32)

    seg = (_row_iota((LANES, LANES)) // RW_HEAD == _lane_iota((LANES, LANES)) // RW_HEAD).astype(BF16)

    def segsum(t):
        return jnp.concatenate(
            [_mm(_split(t[:, p * LANES:(p + 1) * LANES], SEG_PIECES), (seg,)) for p in range(N_PAIRS)],
            axis=1)

    kk0 = k * kk_ref[...]
    kk = kk0 * jnp.minimum(lax.rsqrt(segsum(kk0 * kk0)), 1e12)
    k2 = k * (1.0 + (a - 1.0) * ka_ref[...])
    bb = kk * a

    ti, si = _row_iota((R, R)), _lane_iota((R, R))
    tri = jnp.logical_and(si <= ti, si // C == ti // C).astype(BF16)
    lp = _mm((tri,), _split(ld, 3))
    lpp = lp - ld
    rows_of = lambda i: jnp.concatenate(
        [jnp.broadcast_to(lp[n * C + i:n * C + i + 1, :], (C, W)) for n in range(nb)], axis=0)
    lpc = rows_of(C - 1)
    mid = rows_of(C // 2 - 1)
    r_abs = r * jnp.exp(lp)
    kap_abs = kk * jnp.exp(lpp)
    r_mid = r * jnp.exp(lp - mid)
    kap_mid = kk * jnp.exp(lpp - mid)
    e_mid = jnp.exp(mid - lp)
    b_mid = bb * e_mid
    k_mid = k2 * e_mid
    e_end = jnp.exp(lpc - lp)
    b_end = bb * e_end
    k_end = k2 * e_end
    p_end = jnp.exp(lpc)

    tt = _row_iota((C, 2 * C))
    ss = _lane_iota((C, 2 * C)) % C
    strict = ss < tt
    incl = ss <= tt
    eye_p = (ss == tt).astype(F32)
    jrow = _row_iota((LANES, LANES))
    jlane = _lane_iota((LANES, LANES))
    same_head = (jrow // RW_HEAD) == (jlane // RW_HEAD)
    eye_j = jrow == jlane

    sp = lambda t: _split(t, CHUNK_PIECES)
    bd = lambda pieces, half: tuple(_bdrows(t, half) for t in pieces)

    chains = [(n, p) for n in range(nb) for p in range(N_PAIRS)]
    blk = lambda t, ch: t[ch[0] * C:(ch[0] + 1) * C, ch[1] * LANES:(ch[1] + 1) * LANES]
    each = lambda f: [f(i, ch) for i, ch in enumerate(chains)]

    A = each(lambda i, ch: _mm(
        sp(jnp.concatenate([blk(kap_mid, ch), blk(r_mid, ch)], axis=0)),
        sp(jnp.concatenate([_bdrows(blk(b_mid, ch), RW_HEAD), _bdrows(blk(k_mid, ch), RW_HEAD)],
                           axis=0)), _NT))
    n_kb = each(lambda i, ch: jnp.where(strict, A[i][0:C, 0:2 * C], 0.0))
    a_kk = each(lambda i, ch: jnp.where(strict, A[i][0:C, 2 * C:4 * C], 0.0))
    a_rb = each(lambda i, ch: jnp.where(incl, A[i][C:2 * C, 0:2 * C], 0.0))
    a_rk = each(lambda i, ch: jnp.where(incl, A[i][C:2 * C, 2 * C:4 * C], 0.0))
    T = each(lambda i, ch: eye_p - n_kb[i])
    pw = each(lambda i, ch: sp(n_kb[i]))
    for _ in range(n_sq):
        pw = each(lambda i, ch: sp(_mm(pw[i], bd(pw[i], C))))
        T = each(lambda i, ch: T[i] + _mm(sp(T[i]), bd(pw[i], C)))
    v_p = each(lambda i, ch: sp(blk(v, ch)))
    av = each(lambda i, ch: _mm(sp(jnp.concatenate([a_kk[i], a_rk[i]], axis=0)),
                                bd(v_p[i], RW_HEAD)))
    wu = each(lambda i, ch: sp(_mm(
        sp(T[i]), bd(sp(jnp.concatenate([blk(kap_abs, ch), av[i][0:C]], axis=1)), RW_HEAD))))
    qy = each(lambda i, ch: jnp.concatenate([blk(r_abs, ch), av[i][C:2 * C]], axis=1)
              - _mm(sp(a_rb[i]), bd(wu[i], RW_HEAD)))
    bw = each(lambda i, ch: _mm(sp(blk(b_end, ch)), wu[i], _TN))
    kv = each(lambda i, ch: _mm(_split(blk(k_end, ch), 2), _split(blk(v, ch), 2), _TN))
    G = each(lambda i, ch: jnp.where(same_head, kv[i] - bw[i][:, LANES:2 * LANES], 0.0))
    M = each(lambda i, ch: jnp.where(same_head, bw[i][:, 0:LANES], 0.0))
    p_col = each(lambda i, ch: jnp.sum(jnp.where(eye_j, blk(p_end, ch)[0:1], 0.0), axis=1,
                                       keepdims=True))
    h0f = each(lambda i, ch: hst[ch[0], ch[1]])
    h0 = each(lambda i, ch: sp(h0f[i]))
    yc_ = each(lambda i, ch: _mm(sp(qy[i][:, 0:LANES]), h0[i]) + qy[i][:, LANES:2 * LANES])
    for i, ch in enumerate(chains):
        hst[ch[0], ch[1]] = p_col[i] * h0f[i] - _mm(sp(M[i]), h0[i]) + G[i]
    y = jnp.concatenate([jnp.concatenate(yc_[n * N_PAIRS:(n + 1) * N_PAIRS], axis=1)
                         for n in range(nb)], axis=0)

    mean = segsum(y) * (1.0 / RW_HEAD)
    yc = y - mean
    var = segsum(yc * yc) * (1.0 / RW_HEAD)
    yn = yc * lax.rsqrt(var + GN_EPS) * lnw_ref[...] + lnb_ref[...]
    bonus = segsum(r * k2 * rk_ref[...]) * v
    out_ref[...] = ((yn + bonus) * g).astype(out_ref.dtype).reshape(nb, C, W)

    @pl.when(c == pl.num_programs(1) - 1)
    def _():
        hT_ref[...] = hst[...]


def _rwkv_call(p_rw, shift0, h0_bd, prm, C, nb):
    B, T, _ = p_rw.shape
    n_sq = int(math.log2(C)) - 1
    assert 2 ** (n_sq + 1) == C and T % C == 0 and B % nb == 0
    row = lambda a: a.reshape(1, -1).astype(F32)
    vecs = [row(prm['mu_shift']), row(prm['w0']), row(prm['a0']), row(prm['k_k']), row(prm['k_a']),
            row(prm['r_k']), row(prm['lnx_w']), row(prm['lnx_b'])]
    wlora = jnp.zeros((LANES, 2 * RW_WIDTH), F32)
    wlora = wlora.at[0:DECAY_LORA, 0:RW_WIDTH].set(prm['w_decay_up'])
    wlora = wlora.at[DECAY_LORA:, RW_WIDTH:].set(prm['w_a_up']).astype(BF16)
    wg = prm['w_g_up'].astype(BF16)
    const = lambda shape: pl.BlockSpec(shape, lambda b, c: (0,) * len(shape))
    kern = functools.partial(_rwkv_kernel, C=C, n_sq=n_sq, nb=nb)
    return pl.pallas_call(
        kern,
        grid=(B // nb, T // C),
        in_specs=[pl.BlockSpec((nb, C, RW_COLS), lambda b, c: (b, c, 0)),
                  pl.BlockSpec((nb, 1, RW_COLS), lambda b, c: (b, 0, 0)),
                  pl.BlockSpec((nb, N_PAIRS, LANES, LANES), lambda b, c: (b, 0, 0, 0)),
                  const((1, RW_COLS))] + [const((1, RW_WIDTH))] * 7
                 + [const((LANES, 2 * RW_WIDTH)), const((GATE_LORA, RW_WIDTH))],
        out_specs=[pl.BlockSpec((nb, C, RW_WIDTH), lambda b, c: (b, c, 0)),
                   pl.BlockSpec((nb, N_PAIRS, LANES, LANES), lambda b, c: (b, 0, 0, 0))],
        out_shape=[jax.ShapeDtypeStruct((B, T, RW_WIDTH), BF16),
                   jax.ShapeDtypeStruct((B, N_PAIRS, LANES, LANES), F32)],
        scratch_shapes=[pltpu.VMEM((nb, 8 + C, RW_COLS), F32),
                        pltpu.VMEM((nb, N_PAIRS, LANES, LANES), F32)],
        compiler_params=pltpu.CompilerParams(dimension_semantics=("arbitrary", "arbitrary"),
                                             vmem_limit_bytes=VMEM_LIMIT),
        name="rwkv7_chunked",
    )(p_rw, shift0, h0_bd, *vecs, wlora, wg)


def _state_to_bd(wkv):
    B = wkv.shape[0]
    h = jnp.swapaxes(wkv, -1, -2).reshape(B, N_PAIRS, 2, RW_HEAD, RW_HEAD)
    z = jnp.zeros_like(h[:, :, 0])
    top = jnp.concatenate([h[:, :, 0], z], axis=-1)
    bot = jnp.concatenate([z, h[:, :, 1]], axis=-1)
    return jnp.concatenate([top, bot], axis=-2)


def _bd_to_state(hbd):
    B = hbd.shape[0]
    h0 = hbd[:, :, 0:RW_HEAD, 0:RW_HEAD]
    h1 = hbd[:, :, RW_HEAD:, RW_HEAD:]
    h = jnp.stack([h0, h1], axis=2).reshape(B, RW_HEADS, RW_HEAD, RW_HEAD)
    return jnp.swapaxes(h, -1, -2)


_INT_MIN = -2 ** 31
_NEG_KEY = int(np.float32(NEG).view(np.int32)) ^ 0x7FFFFFFF
_T5_STEPS = ((91, 14), (64, 13), (46, 12), (32, 11), (23, 10), (16, 9), (12, 8), (8, 7), (7, 6),
             (6, 5), (5, 4), (4, 3), (3, 2), (2, 1), (1, 0))
_T5_FAR = 15


def _half_masked(x, half):
    low = (_lane_iota(x.shape) % LANES) < HEAD_DIM
    keep = low if half == 0 else jnp.logical_not(low)
    return jnp.where(keep, x, jnp.zeros_like(x))


def _dup_half(x, half):
    xr = pltpu.roll(x, HEAD_DIM, axis=1)
    low = _lane_iota(x.shape) < HEAD_DIM
    return jnp.where(low, x, xr) if half == 0 else jnp.where(low, xr, x)


def _dot_nt(a, b):
    return lax.dot_general(a, b, (((1,), (1,)), ((), ())), preferred_element_type=F32)


def _dsa_kernel(*refs, Tq, P, Tn, Lp, topk, n_cls):
    if P > 0:
        (q_ref, qi_ref, wi_ref, hk_ref, hv_ref, hki_ref, nk_ref, nv_ref, nki_ref, rb_ref,
         att_ref, kd, vd, kid, lg, tbl, sbias) = refs
    else:
        (q_ref, qi_ref, wi_ref, nk_ref, nv_ref, nki_ref, rb_ref,
         att_ref, kd, vd, kid, lg, tbl, sbias) = refs
    b = pl.program_id(0)
    i = pl.program_id(1)
    L = P + Tn

    @pl.when(jnp.logical_and(b == 0, i == 0))
    def _():
        shape = (Tq, 2 * LANES)
        rel = _lane_iota(shape) - LANES - _row_iota(shape)
        n = jnp.abs(rel)
        for h in range(N_HEADS):
            far = rb_ref[_T5_FAR, h]
            neg = jnp.full(shape, far, F32)
            pos = jnp.full(shape, rb_ref[_T5_FAR + N_BUCKETS // 2, h], F32)
            for bound, bucket in _T5_STEPS:
                neg = jnp.where(n < bound, rb_ref[bucket, h], neg)
                pos = jnp.where(n < bound, rb_ref[bucket + N_BUCKETS // 2, h], pos)
            tbl[h] = jnp.where(rel > 0, pos, neg) - far

    @pl.when(i == 0)
    def _():
        def stage(k_src, v_src, ki_src, r0, n):
            kf = k_src[0]
            vf = v_src[0]
            kif = ki_src[0]
            for j in range(N_KV_HEADS):
                kd[j, r0:r0 + n, :] = _dup_half(kf, j).astype(BF16)
                vd[j, r0:r0 + n, :] = _dup_half(vf, j).astype(BF16)
            kid[r0:r0 + n, :] = jnp.concatenate([kif, kif], axis=1).astype(BF16)

        if P > 0:
            stage(hk_ref, hv_ref, hki_ref, 0, P)
        stage(nk_ref, nv_ref, nki_ref, P, Tn)
        if Lp > L:
            z = jnp.zeros((Lp - L, LANES), BF16)
            for j in range(N_KV_HEADS):
                kd[j, L:Lp, :] = z
                vd[j, L:Lp, :] = z
            kid[L:Lp, :] = z

    stride = Lp // n_cls
    k_hi = P + (i + 1) * Tq
    cls = (k_hi - 1) // stride
    for c in range(n_cls):
        pl.when(cls == c)(functools.partial(
            _dsa_block, q_ref, qi_ref, wi_ref, att_ref, kd, vd, kid, lg, tbl, sbias,
            i=i, Tq=Tq, P=P, L=L, Lk=(c + 1) * stride, topk=topk))


def _dsa_block(q_ref, qi_ref, wi_ref, att_ref, kd, vd, kid, lg, tbl, sbias, *, i, Tq, P, L, Lk, topk):
    q = q_ref[0]
    qi = qi_ref[0]
    key_major = Tq % LANES == 0
    kax = 0 if key_major else 1
    shape = (Lk, Tq) if key_major else (Tq, Lk)
    qvec = (1, Tq) if key_major else (Tq, 1)
    wi = wi_ref[0]

    kidv = kid[0:Lk, :]
    head_q = lambda h: _half_masked(qi[:, (h // 2) * LANES:(h // 2 + 1) * LANES], h % 2)
    score = jnp.zeros(shape, F32)
    if key_major:
        for pr in range(N_IDX_HEADS // 2):
            s2 = _dot_nt(kidv, jnp.concatenate([head_q(2 * pr), head_q(2 * pr + 1)], axis=0))
            for u in range(2):
                h = 2 * pr + u
                score = score + wi[h:h + 1, :] * jnp.maximum(s2[:, u * Tq:(u + 1) * Tq], 0.0)
    else:
        for h in range(N_IDX_HEADS):
            score = score + wi[:, h:h + 1] * jnp.maximum(_dot_nt(head_q(h), kidv), 0.0)

    s_pos = lax.broadcasted_iota(jnp.int32, shape, kax)
    q_pos = P + i * Tq + lax.broadcasted_iota(jnp.int32, shape, 1 - kax)
    adm = jnp.logical_and(s_pos // CHUNK <= q_pos // CHUNK, s_pos < L)
    score = jnp.where(adm, score, NEG)

    def count(mask):
        ones = jnp.where(mask, 1.0, 0.0)
        if key_major:
            ones = jnp.sum(ones.reshape(Lk // LANES, LANES, Tq), axis=0)
        return jnp.sum(ones, axis=kax, keepdims=True)

    def key_to_float(key):
        return pltpu.bitcast(jnp.where(key < 0, key ^ jnp.int32(0x7FFFFFFF), key), F32)

    kf = float(topk)
    ans = jnp.where(count(score >= 0.0) >= kf, jnp.int32(0), jnp.int32(_INT_MIN))

    def bit_step(it, ans):
        cand = ans | (jnp.int32(1) << (30 - it))
        return jnp.where(count(score >= key_to_float(cand)) >= kf, cand, ans)

    def two_bit_step(it, ans):
        shift = 28 - 2 * it
        for m in (1, 2, 3):
            cand = ans | (jnp.int32(m) << shift)
            ans_m = jnp.where(count(score >= key_to_float(cand)) >= kf, cand, ans)
            best = ans_m if m == 1 else jnp.maximum(best, ans_m)
        return best

    if key_major:
        thr_key = lax.fori_loop(0, 31, bit_step, ans)
    else:
        thr_key = lax.fori_loop(0, 15, two_bit_step, bit_step(0, ans))
    thr_key = jnp.maximum(thr_key, jnp.int32(_NEG_KEY))
    thr = key_to_float(thr_key)
    gt = score > thr
    eq = score == thr
    need = kf - count(gt)
    tied = jnp.logical_and(count(eq) > need, thr > NEG)
    any_tied = jnp.max(jnp.where(tied, 1.0, 0.0)) > 0.0

    n_idx_bits = (Lk - 1).bit_length()

    def idx_search():
        def idx_step(it, x):
            cand = x | (jnp.int32(1) << (n_idx_bits - 1 - it))
            return jnp.where(count(jnp.logical_and(eq, s_pos < cand)) < need, cand, x)
        return lax.fori_loop(0, n_idx_bits, idx_step, jnp.zeros(qvec, jnp.int32))

    last = lax.cond(any_tied, idx_search, lambda: jnp.full(qvec, Lk, jnp.int32))
    sel = jnp.logical_and(jnp.logical_or(gt, jnp.logical_and(eq, s_pos <= last)), adm)
    sb = jnp.where(sel, 0.0, NEG)
    sbias[:, 0:Lk] = sb.T if key_major else sb

    blk_end = (P + i * Tq) // LANES + 1
    two_blocks = blk_end >= 2
    start = pl.multiple_of(jnp.maximum(blk_end - 2, 0) * LANES, LANES)
    for j in range(N_KV_HEADS):
        rows = []
        for g in range(GROUP):
            h = j * GROUP + g
            rows.append(_half_masked(q[:, (h // 2) * LANES:(h // 2 + 1) * LANES], h % 2))
        lhs = jnp.concatenate(rows, axis=0)
        lg[:, 0:Lk] = _dot_nt(lhs, kd[j, 0:Lk, :])

        @pl.when(two_blocks)
        def _():
            for g in range(GROUP):
                r = slice(g * Tq, (g + 1) * Tq)
                lg[r, pl.ds(start, 2 * LANES)] = lg[r, pl.ds(start, 2 * LANES)] + tbl[j * GROUP + g]

        @pl.when(jnp.logical_not(two_blocks))
        def _():
            for g in range(GROUP):
                r = slice(g * Tq, (g + 1) * Tq)
                lg[r, 0:LANES] = lg[r, 0:LANES] + tbl[j * GROUP + g][:, LANES:2 * LANES]

        sb = sbias[:, 0:Lk]
        os_ = []
        for g in range(GROUP):
            logits = lg[g * Tq:(g + 1) * Tq, 0:Lk] + sb
            m = jnp.max(logits, axis=1, keepdims=True)
            p = jnp.exp(logits - m)
            l = jnp.sum(p, axis=1, keepdims=True)
            os_.append(jnp.dot(p.astype(BF16), vd[j, 0:Lk, :], preferred_element_type=F32) / l)
        low = _lane_iota((Tq, LANES)) < HEAD_DIM
        for pp in range(GROUP // 2):
            c0 = (j * GROUP // 2 + pp) * LANES
            att_ref[0, :, c0:c0 + LANES] = jnp.where(low, os_[2 * pp], os_[2 * pp + 1]).astype(att_ref.dtype)


def _dsa_call(q, qi, wi, hist, new, rel_bias, Tq):
    B, T, _ = q.shape
    Tn = new[0].shape[1]
    P = 0 if hist is None else hist[0].shape[1]
    L = P + Tn
    Lp = -(-L // LANES) * LANES
    topk = min(MAX_TOPK, L // 4)
    assert T % Tq == 0 and Tn == T and P % LANES == 0 and (Tq % LANES == 0 or T == Tq)
    qspec = lambda w: pl.BlockSpec((1, Tq, w), lambda b, i: (b, i, 0))
    full = lambda n, w: pl.BlockSpec((1, n, w), lambda b, i: (b, 0, 0))
    if Tq % LANES == 0:
        wi = jnp.swapaxes(wi, 1, 2)
        wi_spec = pl.BlockSpec((1, N_IDX_HEADS, Tq), lambda b, i: (b, 0, i))
    else:
        wi_spec = qspec(N_IDX_HEADS)
    in_specs = [qspec(ATT_WIDTH), qspec(N_IDX_HEADS * IDX_DIM), wi_spec]
    args = [q, qi, wi]
    if P > 0:
        in_specs += [full(P, KV_WIDTH), full(P, KV_WIDTH), full(P, IDX_DIM)]
        args += list(hist)
    in_specs += [full(Tn, KV_WIDTH), full(Tn, KV_WIDTH), full(Tn, IDX_DIM),
                 pl.BlockSpec(memory_space=pltpu.SMEM)]
    args += list(new) + [rel_bias.astype(F32)]
    n_cls = Lp // 256 if (P == 0 and Lp % 256 == 0) else 1
    kern = functools.partial(_dsa_kernel, Tq=Tq, P=P, Tn=Tn, Lp=Lp, topk=topk, n_cls=n_cls)
    return pl.pallas_call(
        kern,
        grid=(B, T // Tq),
        in_specs=in_specs,
        out_specs=pl.BlockSpec((1, Tq, ATT_WIDTH), lambda b, i: (b, i, 0)),
        out_shape=jax.ShapeDtypeStruct((B, T, ATT_WIDTH), BF16),
        scratch_shapes=[pltpu.VMEM((N_KV_HEADS, Lp, LANES), BF16),
                        pltpu.VMEM((N_KV_HEADS, Lp, LANES), BF16),
                        pltpu.VMEM((Lp, LANES), BF16),
                        pltpu.VMEM((GROUP * Tq, Lp), F32),
                        pltpu.VMEM((N_HEADS, Tq, 2 * LANES), F32),
                        pltpu.VMEM((Tq, Lp), F32)],
        compiler_params=pltpu.CompilerParams(dimension_semantics=("arbitrary", "arbitrary"),
                                             vmem_limit_bytes=VMEM_LIMIT),
        name="dsa_attention",
    )(*args)


ATT_COLS = 2 * ATT_WIDTH + 2 * KV_WIDTH + LANES
_C_Q, _C_QI, _C_K, _C_V, _C_KI = 0, 512, 1024, 1152, 1280
_C_WI = _C_KI + IDX_DIM


def _rms(x, g):
    return x * lax.rsqrt(jnp.mean(x * x, axis=-1, keepdims=True) + NORM_EPS) * g


def _inproj_kernel(x_ref, g_ref, wa_ref, wr_ref, q_ref, qi_ref, k_ref, v_ref, ki_ref, wi_ref, prw_ref):
    h = _rms(x_ref[...], g_ref[...]).astype(BF16)
    pa = jnp.dot(h, wa_ref[...], preferred_element_type=F32)
    q_ref[...] = (pa[:, _C_Q:_C_Q + ATT_WIDTH] * (HEAD_DIM ** -0.5)).astype(BF16)
    qi_ref[...] = (pa[:, _C_QI:_C_QI + ATT_WIDTH] * (IDX_DIM ** -0.5)).astype(BF16)
    k_ref[...] = pa[:, _C_K:_C_K + KV_WIDTH]
    v_ref[...] = pa[:, _C_V:_C_V + KV_WIDTH]
    ki_ref[...] = pa[:, _C_KI:_C_KI + IDX_DIM]
    wi_ref[...] = pa[:, _C_WI:_C_WI + N_IDX_HEADS] * (N_IDX_HEADS ** -0.5)
    prw_ref[...] = jnp.dot(h, wr_ref[...], preferred_element_type=F32)


def _const_spec(shape):
    return pl.BlockSpec(shape, lambda i: (0,) * len(shape), pipeline_mode=pl.Buffered(1))


def _inproj_call(x2d, g, wa, wr, tm):
    N = x2d.shape[0]
    assert N % tm == 0
    rows = lambda w: pl.BlockSpec((tm, w), lambda i: (i, 0))
    widths = [ATT_WIDTH, ATT_WIDTH, KV_WIDTH, KV_WIDTH, IDX_DIM, N_IDX_HEADS, RW_COLS]
    dtypes = [BF16, BF16, F32, F32, F32, F32, F32]
    return pl.pallas_call(
        _inproj_kernel,
        grid=(N // tm,),
        in_specs=[rows(D_MODEL), _const_spec((1, D_MODEL)), _const_spec((D_MODEL, ATT_COLS)),
                  _const_spec((D_MODEL, RW_COLS))],
        out_specs=[rows(w) for w in widths],
        out_shape=[jax.ShapeDtypeStruct((N, w), d) for w, d in zip(widths, dtypes)],
        compiler_params=pltpu.CompilerParams(dimension_semantics=("arbitrary",),
                                             vmem_limit_bytes=VMEM_LIMIT),
        name="rmsnorm_inproj",
    )(x2d, g, wa, wr)


def _tail_kernel(x_ref, att_ref, rw_ref, g1_ref, wgate_ref, wao_ref, wro_ref, wout_ref,
                 g2_ref, wff1_ref, wff2_ref, gf_ref, y_ref):
    x = x_ref[...]
    h = _rms(x, g1_ref[...]).astype(BF16)
    gates = jax.nn.sigmoid(jnp.dot(h, wgate_ref[...], preferred_element_type=F32))
    z_a = jnp.dot(att_ref[...], wao_ref[...], preferred_element_type=F32)
    z_b = jnp.dot(rw_ref[...], wro_ref[...], preferred_element_type=F32)
    mixed = gates[:, 0:D_MODEL] * z_a + gates[:, D_MODEL:2 * D_MODEL] * z_b
    x1 = x + jnp.dot(mixed.astype(BF16), wout_ref[...], preferred_element_type=F32)
    h2 = _rms(x1, g2_ref[...]).astype(BF16)
    f = jnp.maximum(jnp.dot(h2, wff1_ref[...], preferred_element_type=F32), 0.0)
    x2 = x1 + jnp.dot((f * f).astype(BF16), wff2_ref[...], preferred_element_type=F32)
    y_ref[...] = _rms(x2, gf_ref[...])


def _tail_call(x2d, att, rw, g1, wgate, wao, wro, wout, g2, wff1, wff2, gf, tm):
    N = x2d.shape[0]
    assert N % tm == 0
    rows = lambda w: pl.BlockSpec((tm, w), lambda i: (i, 0))
    return pl.pallas_call(
        _tail_kernel,
        grid=(N // tm,),
        in_specs=[rows(D_MODEL), rows(ATT_WIDTH), rows(RW_WIDTH), _const_spec((1, D_MODEL)),
                  _const_spec((D_MODEL, 2 * D_MODEL)), _const_spec((ATT_WIDTH, D_MODEL)),
                  _const_spec((RW_WIDTH, D_MODEL)), _const_spec((D_MODEL, D_MODEL)),
                  _const_spec((1, D_MODEL)), _const_spec((D_MODEL, D_FF)), _const_spec((D_FF, D_MODEL)),
                  _const_spec((1, D_MODEL))],
        out_specs=rows(D_MODEL),
        out_shape=jax.ShapeDtypeStruct((N, D_MODEL), F32),
        compiler_params=pltpu.CompilerParams(dimension_semantics=("arbitrary",),
                                             vmem_limit_bytes=VMEM_LIMIT),
        name="merge_out_mlp",
    )(x2d, att, rw, g1, wgate, wao, wro, wout, g2, wff1, wff2, gf)


_IN_SPLITS = (ATT_WIDTH, KV_WIDTH, KV_WIDTH, N_IDX_HEADS * IDX_DIM, IDX_DIM, N_IDX_HEADS,
              RW_COLS, D_MODEL, D_MODEL)


def _split_w_in(w_in):
    offs = [0]
    for s in _IN_SPLITS:
        offs.append(offs[-1] + s)
    col = lambda n: w_in[:, offs[n]:offs[n + 1]]
    pad = jnp.zeros((D_MODEL, ATT_COLS - _C_WI - N_IDX_HEADS), w_in.dtype)
    wa = jnp.concatenate([col(0), col(3), col(1), col(2), col(4), col(5), pad], axis=1).astype(BF16)
    wr = col(6).astype(BF16)
    wgate = jnp.concatenate([col(7), col(8)], axis=1).astype(BF16)
    return wa, wr, wgate


def _branch(x, hist, wkv0, shift0, rel_bias, prm, w, tm_in, tm_tail, Tq, C, nb_rw):
    B, T, _ = x.shape
    x2d = x.reshape(B * T, D_MODEL)
    q, qi, k, v, ki, wi, p_rw = _inproj_call(x2d, prm['g1'], w['wa'], w['wr'], tm_in)
    r3 = lambda a: a.reshape(B, T, a.shape[-1])
    rw, hT = _rwkv_call(r3(p_rw), shift0, _state_to_bd(wkv0), prm, C, nb_rw)
    att = _dsa_call(r3(q), r3(qi), r3(wi), hist, (r3(k), r3(v), r3(ki)), rel_bias, Tq)
    y = _tail_call(x2d, att.reshape(B * T, ATT_WIDTH), rw.reshape(B * T, RW_WIDTH), prm['g1'],
                   w['wgate'], w['wao'], w['wro'], w['wout'], prm['g2'], w['wff1'], w['wff2'],
                   prm['gf'], tm_tail)
    new_k = k.reshape(1, B, T, N_KV_HEADS, HEAD_DIM)
    new_v = v.reshape(1, B, T, N_KV_HEADS, HEAD_DIM)
    new_ki = ki.reshape(1, B, T, IDX_DIM)
    wkv = _bd_to_state(hT)[None]
    shift = r3(p_rw)[:, T - 1:T, :][None]
    return y.reshape(B, T, D_MODEL), new_k, new_v, new_ki, wkv, shift


def kernel(x_prompt, x_sample, cache_k, cache_v, cache_kidx, state_wkv, state_shift, rel_bias,
           norm1_g, w_in, w_att_out, mu_shift, w0, w_decay_up, a0, w_a_up, w_g_up, k_k, k_a, r_k,
           lnx_w, lnx_b, w_rw_out, w_out, norm2_g, w_ff1, w_ff2, norm_f_g):
    assert w_in.shape[0] == 1, "single-layer step"
    wa, wr, wgate = _split_w_in(w_in[0])
    w = dict(wa=wa, wr=wr, wgate=wgate, wao=w_att_out[0].astype(BF16), wro=w_rw_out[0].astype(BF16),
             wout=w_out[0].astype(BF16), wff1=w_ff1[0].astype(BF16), wff2=w_ff2[0].astype(BF16))
    prm = dict(mu_shift=mu_shift[0], w0=w0[0], w_decay_up=w_decay_up[0], a0=a0[0], w_a_up=w_a_up[0],
               w_g_up=w_g_up[0], k_k=k_k[0], k_a=k_a[0], r_k=r_k[0].reshape(-1), lnx_w=lnx_w[0],
               lnx_b=lnx_b[0], g1=norm1_g[0].reshape(1, -1), g2=norm2_g[0].reshape(1, -1),
               gf=norm_f_g.reshape(1, -1))
    Bp, Tp, _ = x_prompt.shape
    Bs, Ts, _ = x_sample.shape
    past = cache_k.shape[2]
    zero_state = jnp.zeros((Bp, RW_HEADS, RW_HEAD, RW_HEAD), F32)
    zero_shift = jnp.zeros((Bp, 1, RW_COLS), F32)
    outs_p = _branch(x_prompt, None, zero_state, zero_shift, rel_bias, prm, w,
                     tm_in=512, tm_tail=256, Tq=Q_BLOCK, C=CHUNK, nb_rw=4)
    hist = (cache_k[0].reshape(Bs, past, KV_WIDTH), cache_v[0].reshape(Bs, past, KV_WIDTH),
            cache_kidx[0])
    outs_s = _branch(x_sample, hist, state_wkv[0], state_shift[0], rel_bias, prm, w,
                     tm_in=Bs * Ts, tm_tail=256, Tq=Ts, C=Ts, nb_rw=4)
    return (outs_p[0], outs_s[0]) + outs_p[1:] + outs_s[1:]
```

```python
import functools
import math

import jax
import jax.numpy as jnp
import numpy as np
from jax import lax
from jax.experimental import pallas as pl
from jax.experimental.pallas import tpu as pltpu

F32 = jnp.float32
BF16 = jnp.bfloat16

D_MODEL = 1024
CHUNK = 64
N_HEADS = 8
N_KV_HEADS = 2
HEAD_DIM = 64
GROUP = N_HEADS // N_KV_HEADS
ATT_WIDTH = N_HEADS * HEAD_DIM
KV_WIDTH = N_KV_HEADS * HEAD_DIM
N_IDX_HEADS = 8
IDX_DIM = 64
MAX_TOPK = 256
Q_BLOCK = 128
N_BUCKETS = 32
RW_HEADS = 8
RW_HEAD = 64
RW_WIDTH = RW_HEADS * RW_HEAD
DECAY_LORA = 64
AAA_LORA = 64
GATE_LORA = 128
RW_COLS = 3 * RW_WIDTH + DECAY_LORA + AAA_LORA + GATE_LORA
GN_EPS = 64e-5
D_FF = 4 * D_MODEL
NORM_EPS = 1e-6
NEG = -1e30

LANES = 128
N_PAIRS = RW_HEADS // 2
VMEM_LIMIT = 56 * 1024 * 1024

CHUNK_PIECES = 1
SEG_PIECES = 2

_NN, _NT, _TN = (1, 0), (1, 1), (0, 0)


def _split(x, n):
    pieces = []
    rem = x
    for i in range(n):
        p = rem.astype(BF16)
        pieces.append(p)
        if i + 1 < n:
            rem = rem - p.astype(F32)
    return tuple(pieces)


def _mm(a, b, dims=_NN):
    n = max(len(a), len(b))
    terms = [(i, j) for i in range(len(a)) for j in range(len(b)) if i + j < n]
    ca, cb = dims
    lhs = jnp.concatenate([a[i] for i, _ in terms], axis=ca) if len(terms) > 1 else a[0]
    rhs = jnp.concatenate([b[j] for _, j in terms], axis=cb) if len(terms) > 1 else b[0]
    return lax.dot_general(lhs, rhs, (((ca,), (cb,)), ((), ())), preferred_element_type=F32)


def _lane_iota(shape):
    return lax.broadcasted_iota(jnp.int32, shape, len(shape) - 1)


def _row_iota(shape):
    return lax.broadcasted_iota(jnp.int32, shape, len(shape) - 2)


def _bdrows(x, width, heads=2):
    head = (_lane_iota(x.shape) % (heads * width)) // width
    return jnp.concatenate([jnp.where(head == h, x, 0.0) for h in range(heads)], axis=0)


def _rwkv_kernel(x_ref, shift0_ref, h0_ref, mu_ref, w0_ref, a0_ref, kk_ref, ka_ref, rk_ref,
                 lnw_ref, lnb_ref, wlora_ref, wg_ref,
                 out_ref, hT_ref, xbuf, hst, *, C, n_sq, nb):
    c = pl.program_id(1)

    @pl.when(c == 0)
    def _():
        for n in range(nb):
            xbuf[n, 7:8, :] = shift0_ref[n]
        hst[...] = h0_ref[...]

    W = RW_WIDTH
    R = nb * C
    prevs = []
    for n in range(nb):
        xn = x_ref[n]
        xbuf[n, 8:8 + C, :] = xn
        prevs.append(xbuf[n, 7:7 + C, :])
        xbuf[n, 7:8, :] = xn[C - 1:C, :]
    x = x_ref[...].reshape(R, RW_COLS)
    prev = jnp.concatenate(prevs, axis=0)
    xs = x + (prev - x) * mu_ref[...]
    r = xs[:, 0:W]
    k = xs[:, W:2 * W]
    v = xs[:, 2 * W:3 * W]
    da = xs[:, 3 * W:3 * W + LANES]
    gd = xs[:, 3 * W + LANES:3 * W + 2 * LANES]

    t_in = jnp.where(_lane_iota(da.shape) < DECAY_LORA, jnp.tanh(da), da)
    lo = jnp.dot(t_in.astype(BF16), wlora_ref[...], preferred_element_type=F32)
    ld = -math.exp(-0.5) * jax.nn.sigmoid(w0_ref[...] + lo[:, 0:W])
    a = jax.nn.sigmoid(a0_ref[...] + lo[:, W:2 * W])
    g = jnp.dot(jax.nn.sigmoid(gd).astype(BF16), wg_ref[...], preferred_element_type=F32)

    seg = (_row_iota((LANES, LANES)) // RW_HEAD == _lane_iota((LANES, LANES)) // RW_HEAD).astype(BF16)

    def segsum(t):
        return jnp.concatenate(
            [_mm(_split(t[:, p * LANES:(p + 1) * LANES], SEG_PIECES), (seg,)) for p in range(N_PAIRS)],
            axis=1)

    kk0 = k * kk_ref[...]
    kk = kk0 * jnp.minimum(lax.rsqrt(segsum(kk0 * kk0)), 1e12)
    k2 = k * (1.0 + (a - 1.0) * ka_ref[...])
    bb = kk * a

    ti, si = _row_iota((R, R)), _lane_iota((R, R))
    tri = jnp.logical_and(si <= ti, si // C == ti // C).astype(BF16)
    lp = _mm((tri,), _split(ld, 3))
    lpp = lp - ld
    rows_of = lambda i: jnp.concatenate(
        [jnp.broadcast_to(lp[n * C + i:n * C + i + 1, :], (C, W)) for n in range(nb)], axis=0)
    lpc = rows_of(C - 1)
    mid = rows_of(C // 2 - 1)
    r_abs = r * jnp.exp(lp)
    kap_abs = kk * jnp.exp(lpp)
    r_mid = r * jnp.exp(lp - mid)
    kap_mid = kk * jnp.exp(lpp - mid)
    e_mid = jnp.exp(mid - lp)
    b_mid = bb * e_mid
    k_mid = k2 * e_mid
    e_end = jnp.exp(lpc - lp)
    b_end = bb * e_end
    k_end = k2 * e_end
    p_end = jnp.exp(lpc)

    tt = _row_iota((C, 2 * C))
    ss = _lane_iota((C, 2 * C)) % C
    strict = ss < tt
    incl = ss <= tt
    eye_p = (ss == tt).astype(F32)
    jrow = _row_iota((LANES, LANES))
    jlane = _lane_iota((LANES, LANES))
    same_head = (jrow // RW_HEAD) == (jlane // RW_HEAD)
    eye_j = jrow == jlane

    sp = lambda t: _split(t, CHUNK_PIECES)
    bd = lambda pieces, half: tuple(_bdrows(t, half) for t in pieces)

    chains = [(n, p) for n in range(nb) for p in range(N_PAIRS)]
    blk = lambda t, ch: t[ch[0] * C:(ch[0] + 1) * C, ch[1] * LANES:(ch[1] + 1) * LANES]
    each = lambda f: [f(i, ch) for i, ch in enumerate(chains)]

    A = each(lambda i, ch: _mm(
        sp(jnp.concatenate([blk(kap_mid, ch), blk(r_mid, ch)], axis=0)),
        sp(jnp.concatenate([_bdrows(blk(b_mid, ch), RW_HEAD), _bdrows(blk(k_mid, ch), RW_HEAD)],
                           axis=0)), _NT))
    n_kb = each(lambda i, ch: jnp.where(strict, A[i][0:C, 0:2 * C], 0.0))
    a_kk = each(lambda i, ch: jnp.where(strict, A[i][0:C, 2 * C:4 * C], 0.0))
    a_rb = each(lambda i, ch: jnp.where(incl, A[i][C:2 * C, 0:2 * C], 0.0))
    a_rk = each(lambda i, ch: jnp.where(incl, A[i][C:2 * C, 2 * C:4 * C], 0.0))
    T = each(lambda i, ch: eye_p - n_kb[i])
    pw = each(lambda i, ch: sp(n_kb[i]))
    for _ in range(n_sq):
        pw = each(lambda i, ch: sp(_mm(pw[i], bd(pw[i], C))))
        T = each(lambda i, ch: T[i] + _mm(sp(T[i]), bd(pw[i], C)))
    v_p = each(lambda i, ch: sp(blk(v, ch)))
    av = each(lambda i, ch: _mm(sp(jnp.concatenate([a_kk[i], a_rk[i]], axis=0)),
                                bd(v_p[i], RW_HEAD)))
    wu = each(lambda i, ch: sp(_mm(
        sp(T[i]), bd(sp(jnp.concatenate([blk(kap_abs, ch), av[i][0:C]], axis=1)), RW_HEAD))))
    qy = each(lambda i, ch: jnp.concatenate([blk(r_abs, ch), av[i][C:2 * C]], axis=1)
              - _mm(sp(a_rb[i]), bd(wu[i], RW_HEAD)))
    bw = each(lambda i, ch: _mm(sp(blk(b_end, ch)), wu[i], _TN))
    kv = each(lambda i, ch: _mm(_split(blk(k_end, ch), 2), _split(blk(v, ch), 2), _TN))
    G = each(lambda i, ch: jnp.where(same_head, kv[i] - bw[i][:, LANES:2 * LANES], 0.0))
    M = each(lambda i, ch: jnp.where(same_head, bw[i][:, 0:LANES], 0.0))
    p_col = each(lambda i, ch: jnp.sum(jnp.where(eye_j, blk(p_end, ch)[0:1], 0.0), axis=1,
                                       keepdims=True))
    h0f = each(lambda i, ch: hst[ch[0], ch[1]])
    h0 = each(lambda i, ch: sp(h0f[i]))
    yc_ = each(lambda i, ch: _mm(sp(qy[i][:, 0:LANES]), h0[i]) + qy[i][:, LANES:2 * LANES])
    for i, ch in enumerate(chains):
        hst[ch[0], ch[1]] = p_col[i] * h0f[i] - _mm(sp(M[i]), h0[i]) + G[i]
    y = jnp.concatenate([jnp.concatenate(yc_[n * N_PAIRS:(n + 1) * N_PAIRS], axis=1)
                         for n in range(nb)], axis=0)

    mean = segsum(y) * (1.0 / RW_HEAD)
    yc = y - mean
    var = segsum(yc * yc) * (1.0 / RW_HEAD)
    yn = yc * lax.rsqrt(var + GN_EPS) * lnw_ref[...] + lnb_ref[...]
    bonus = segsum(r * k2 * rk_ref[...]) * v
    out_ref[...] = ((yn + bonus) * g).astype(out_ref.dtype).reshape(nb, C, W)

    @pl.when(c == pl.num_programs(1) - 1)
    def _():
        hT_ref[...] = hst[...]


def _rwkv_call(p_rw, shift0, h0_bd, prm, C, nb):
    B, T, _ = p_rw.shape
    n_sq = int(math.log2(C)) - 1
    assert 2 ** (n_sq + 1) == C and T % C == 0 and B % nb == 0
    row = lambda a: a.reshape(1, -1).astype(F32)
    vecs = [row(prm['mu_shift']), row(prm['w0']), row(prm['a0']), row(prm['k_k']), row(prm['k_a']),
            row(prm['r_k']), row(prm['lnx_w']), row(prm['lnx_b'])]
    wlora = jnp.zeros((LANES, 2 * RW_WIDTH), F32)
    wlora = wlora.at[0:DECAY_LORA, 0:RW_WIDTH].set(prm['w_decay_up'])
    wlora = wlora.at[DECAY_LORA:, RW_WIDTH:].set(prm['w_a_up']).astype(BF16)
    wg = prm['w_g_up'].astype(BF16)
    const = lambda shape: pl.BlockSpec(shape, lambda b, c: (0,) * len(shape))
    kern = functools.partial(_rwkv_kernel, C=C, n_sq=n_sq, nb=nb)
    return pl.pallas_call(
        kern,
        grid=(B // nb, T // C),
        in_specs=[pl.BlockSpec((nb, C, RW_COLS), lambda b, c: (b, c, 0)),
                  pl.BlockSpec((nb, 1, RW_COLS), lambda b, c: (b, 0, 0)),
                  pl.BlockSpec((nb, N_PAIRS, LANES, LANES), lambda b, c: (b, 0, 0, 0)),
                  const((1, RW_COLS))] + [const((1, RW_WIDTH))] * 7
                 + [const((LANES, 2 * RW_WIDTH)), const((GATE_LORA, RW_WIDTH))],
        out_specs=[pl.BlockSpec((nb, C, RW_WIDTH), lambda b, c: (b, c, 0)),
                   pl.BlockSpec((nb, N_PAIRS, LANES, LANES), lambda b, c: (b, 0, 0, 0))],
        out_shape=[jax.ShapeDtypeStruct((B, T, RW_WIDTH), BF16),
                   jax.ShapeDtypeStruct((B, N_PAIRS, LANES, LANES), F32)],
        scratch_shapes=[pltpu.VMEM((nb, 8 + C, RW_COLS), F32),
                        pltpu.VMEM((nb, N_PAIRS, LANES, LANES), F32)],
        compiler_params=pltpu.CompilerParams(dimension_semantics=("arbitrary", "arbitrary"),
                                             vmem_limit_bytes=VMEM_LIMIT),
        name="rwkv7_chunked",
    )(p_rw, shift0, h0_bd, *vecs, wlora, wg)


def _state_to_bd(wkv):
    B = wkv.shape[0]
    h = jnp.swapaxes(wkv, -1, -2).reshape(B, N_PAIRS, 2, RW_HEAD, RW_HEAD)
    z = jnp.zeros_like(h[:, :, 0])
    top = jnp.concatenate([h[:, :, 0], z], axis=-1)
    bot = jnp.concatenate([z, h[:, :, 1]], axis=-1)
    return jnp.concatenate([top, bot], axis=-2)


def _bd_to_state(hbd):
    B = hbd.shape[0]
    h0 = hbd[:, :, 0:RW_HEAD, 0:RW_HEAD]
    h1 = hbd[:, :, RW_HEAD:, RW_HEAD:]
    h = jnp.stack([h0, h1], axis=2).reshape(B, RW_HEADS, RW_HEAD, RW_HEAD)
    return jnp.swapaxes(h, -1, -2)


_INT_MIN = -2 ** 31
_NEG_KEY = int(np.float32(NEG).view(np.int32)) ^ 0x7FFFFFFF
_T5_STEPS = ((91, 14), (64, 13), (46, 12), (32, 11), (23, 10), (16, 9), (12, 8), (8, 7), (7, 6),
             (6, 5), (5, 4), (4, 3), (3, 2), (2, 1), (1, 0))
_T5_FAR = 15


def _half_masked(x, half):
    low = (_lane_iota(x.shape) % LANES) < HEAD_DIM
    keep = low if half == 0 else jnp.logical_not(low)
    return jnp.where(keep, x, jnp.zeros_like(x))


def _dup_half(x, half):
    xr = pltpu.roll(x, HEAD_DIM, axis=1)
    low = _lane_iota(x.shape) < HEAD_DIM
    return jnp.where(low, x, xr) if half == 0 else jnp.where(low, xr, x)


def _dot_nt(a, b):
    return lax.dot_general(a, b, (((1,), (1,)), ((), ())), preferred_element_type=F32)


def _dsa_kernel(*refs, Tq, P, Tn, Lp, topk, n_cls):
    if P > 0:
        (q_ref, qi_ref, wi_ref, hk_ref, hv_ref, hki_ref, nk_ref, nv_ref, nki_ref, rb_ref,
         att_ref, kd, vd, kid, lg, tbl, sbias) = refs
    else:
        (q_ref, qi_ref, wi_ref, nk_ref, nv_ref, nki_ref, rb_ref,
         att_ref, kd, vd, kid, lg, tbl, sbias) = refs
    b = pl.program_id(0)
    i = pl.program_id(1)
    L = P + Tn

    @pl.when(jnp.logical_and(b == 0, i == 0))
    def _():
        shape = (Tq, 2 * LANES)
        rel = _lane_iota(shape) - LANES - _row_iota(shape)
        n = jnp.abs(rel)
        for h in range(N_HEADS):
            far = rb_ref[_T5_FAR, h]
            neg = jnp.full(shape, far, F32)
            pos = jnp.full(shape, rb_ref[_T5_FAR + N_BUCKETS // 2, h], F32)
            for bound, bucket in _T5_STEPS:
                neg = jnp.where(n < bound, rb_ref[bucket, h], neg)
                pos = jnp.where(n < bound, rb_ref[bucket + N_BUCKETS // 2, h], pos)
            tbl[h] = jnp.where(rel > 0, pos, neg) - far

    @pl.when(i == 0)
    def _():
        if P > 0:
            def put(dst, x, c0):
                xb = x.astype(BF16)
                dst[0:HEAD_DIM, c0:c0 + x.shape[1]] = xb
                dst[HEAD_DIM:LANES, c0:c0 + x.shape[1]] = xb

            zt = jnp.zeros((LANES, Lp - P), BF16)
            for j in range(N_KV_HEADS):
                for hist_ref, new_ref, dst in ((hk_ref, nk_ref, kd), (hv_ref, nv_ref, vd)):
                    put(dst.at[j], hist_ref[0, j], 0)
                    dst[j, :, P:Lp] = zt
                    put(dst.at[j], new_ref[0, j * HEAD_DIM:(j + 1) * HEAD_DIM, :], P)
            put(kid, hki_ref[0], 0)
            kid[:, P:Lp] = zt
            put(kid, nki_ref[0], P)
        else:
            kf, vf, kif = nk_ref[0], nv_ref[0], nki_ref[0]
            for j in range(N_KV_HEADS):
                kd[j, 0:Tn, :] = _dup_half(kf, j).astype(BF16)
                vd[j, 0:Tn, :] = _dup_half(vf, j).astype(BF16)
            kid[0:Tn, :] = jnp.concatenate([kif, kif], axis=1).astype(BF16)

    stride = Lp // n_cls
    k_hi = P + (i + 1) * Tq
    cls = (k_hi - 1) // stride
    for c in range(n_cls):
        pl.when(cls == c)(functools.partial(
            _dsa_block, q_ref, qi_ref, wi_ref, att_ref, kd, vd, kid, lg, tbl, sbias,
            i=i, Tq=Tq, P=P, L=L, Lk=(c + 1) * stride, topk=topk))


def _dsa_block(q_ref, qi_ref, wi_ref, att_ref, kd, vd, kid, lg, tbl, sbias, *, i, Tq, P, L, Lk, topk):
    q = q_ref[0]
    qi = qi_ref[0]
    key_major = Tq % LANES == 0
    kax = 0 if key_major else 1
    shape = (Lk, Tq) if key_major else (Tq, Lk)
    qvec = (1, Tq) if key_major else (Tq, 1)
    wi = wi_ref[0]

    lanes_keys = P > 0
    kidv = kid[:, 0:Lk] if lanes_keys else kid[0:Lk, :]
    head_q = lambda h: _half_masked(qi[:, (h // 2) * LANES:(h // 2 + 1) * LANES], h % 2)
    score = jnp.zeros(shape, F32)
    if key_major:
        for pr in range(N_IDX_HEADS // 2):
            s2 = _dot_nt(kidv, jnp.concatenate([head_q(2 * pr), head_q(2 * pr + 1)], axis=0))
            for u in range(2):
                h = 2 * pr + u
                score = score + wi[h:h + 1, :] * jnp.maximum(s2[:, u * Tq:(u + 1) * Tq], 0.0)
    else:
        lhs_all = jnp.concatenate([head_q(h) for h in range(N_IDX_HEADS)], axis=0)
        s_all = (jnp.dot(lhs_all, kidv, preferred_element_type=F32) if lanes_keys
                 else _dot_nt(lhs_all, kidv))
        for h in range(N_IDX_HEADS):
            score = score + wi[:, h:h + 1] * jnp.maximum(s_all[h * Tq:(h + 1) * Tq], 0.0)

    s_pos = lax.broadcasted_iota(jnp.int32, shape, kax)
    q_pos = P + i * Tq + lax.broadcasted_iota(jnp.int32, shape, 1 - kax)
    adm = jnp.logical_and(s_pos // CHUNK <= q_pos // CHUNK, s_pos < L)
    score = jnp.where(adm, score, NEG)

    def count(mask):
        ones = jnp.where(mask, 1.0, 0.0)
        if key_major:
            ones = jnp.sum(ones.reshape(Lk // LANES, LANES, Tq), axis=0)
        return jnp.sum(ones, axis=kax, keepdims=True)

    def key_to_float(key):
        return pltpu.bitcast(jnp.where(key < 0, key ^ jnp.int32(0x7FFFFFFF), key), F32)

    kf = float(topk)
    ans = jnp.where(count(score >= 0.0) >= kf, jnp.int32(0), jnp.int32(_INT_MIN))

    def bit_step(it, ans):
        cand = ans | (jnp.int32(1) << (30 - it))
        return jnp.where(count(score >= key_to_float(cand)) >= kf, cand, ans)

    def two_bit_step(it, ans):
        shift = 28 - 2 * it
        for m in (1, 2, 3):
            cand = ans | (jnp.int32(m) << shift)
            ans_m = jnp.where(count(score >= key_to_float(cand)) >= kf, cand, ans)
            best = ans_m if m == 1 else jnp.maximum(best, ans_m)
        return best

    if key_major:
        thr_key = lax.fori_loop(0, 31, bit_step, ans)
    else:
        thr_key = lax.fori_loop(0, 15, two_bit_step, bit_step(0, ans))
    thr_key = jnp.maximum(thr_key, jnp.int32(_NEG_KEY))
    thr = key_to_float(thr_key)
    gt = score > thr
    eq = score == thr
    need = kf - count(gt)
    tied = jnp.logical_and(count(eq) > need, thr > NEG)
    any_tied = jnp.max(jnp.where(tied, 1.0, 0.0)) > 0.0

    n_idx_bits = (Lk - 1).bit_length()

    def idx_search():
        def idx_step(it, x):
            cand = x | (jnp.int32(1) << (n_idx_bits - 1 - it))
            return jnp.where(count(jnp.logical_and(eq, s_pos < cand)) < need, cand, x)
        return lax.fori_loop(0, n_idx_bits, idx_step, jnp.zeros(qvec, jnp.int32))

    last = lax.cond(any_tied, idx_search, lambda: jnp.full(qvec, Lk, jnp.int32))
    sel = jnp.logical_and(jnp.logical_or(gt, jnp.logical_and(eq, s_pos <= last)), adm)
    sb = jnp.where(sel, 0.0, NEG)
    sbias[:, 0:Lk] = sb.T if key_major else sb

    blk_end = (P + i * Tq) // LANES + 1
    two_blocks = blk_end >= 2
    start = pl.multiple_of(jnp.maximum(blk_end - 2, 0) * LANES, LANES)
    for j in range(N_KV_HEADS):
        rows = []
        for g in range(GROUP):
            h = j * GROUP + g
            rows.append(_half_masked(q[:, (h // 2) * LANES:(h // 2 + 1) * LANES], h % 2))
        lhs = jnp.concatenate(rows, axis=0)
        lg[:, 0:Lk] = (jnp.dot(lhs, kd[j, :, 0:Lk], preferred_element_type=F32) if lanes_keys
                       else _dot_nt(lhs, kd[j, 0:Lk, :]))

        @pl.when(two_blocks)
        def _():
            for g in range(GROUP):
                r = slice(g * Tq, (g + 1) * Tq)
                lg[r, pl.ds(start, 2 * LANES)] = lg[r, pl.ds(start, 2 * LANES)] + tbl[j * GROUP + g]

        @pl.when(jnp.logical_not(two_blocks))
        def _():
            for g in range(GROUP):
                r = slice(g * Tq, (g + 1) * Tq)
                lg[r, 0:LANES] = lg[r, 0:LANES] + tbl[j * GROUP + g][:, LANES:2 * LANES]

        sb = sbias[:, 0:Lk]
        ps, ls = [], []
        for g in range(GROUP):
            logits = lg[g * Tq:(g + 1) * Tq, 0:Lk] + sb
            m = jnp.max(logits, axis=1, keepdims=True)
            p = jnp.exp(logits - m)
            ls.append(jnp.sum(p, axis=1, keepdims=True))
            ps.append(p.astype(BF16))
        pv = lambda p: (_dot_nt(p, vd[j, :, 0:Lk]) if lanes_keys
                        else jnp.dot(p, vd[j, 0:Lk, :], preferred_element_type=F32))
        if key_major:
            os_ = [pv(ps[g]) / ls[g] for g in range(GROUP)]
        else:
            o = pv(jnp.concatenate(ps, axis=0))
            os_ = [o[g * Tq:(g + 1) * Tq] / ls[g] for g in range(GROUP)]
        low = _lane_iota((Tq, LANES)) < HEAD_DIM
        for pp in range(GROUP // 2):
            c0 = (j * GROUP // 2 + pp) * LANES
            att_ref[0, :, c0:c0 + LANES] = jnp.where(low, os_[2 * pp], os_[2 * pp + 1]).astype(att_ref.dtype)


def _dsa_call(q, qi, wi, hist, new, rel_bias, Tq):
    B, T, _ = q.shape
    P = 0 if hist is None else hist[0].shape[-1]
    Tn = new[0].shape[2 if P > 0 else 1]
    L = P + Tn
    Lp = -(-L // LANES) * LANES
    topk = min(MAX_TOPK, L // 4)
    assert T % Tq == 0 and Tn == T and P % LANES == 0 and (Tq % LANES == 0 or T == Tq)
    assert (P == 0 and Lp == L) or (P > 0 and Tq % LANES != 0)
    qspec = lambda w: pl.BlockSpec((1, Tq, w), lambda b, i: (b, i, 0))
    full = lambda *dims: pl.BlockSpec((1,) + dims, lambda b, i: (b,) + (0,) * len(dims))
    if Tq % LANES == 0:
        wi = jnp.swapaxes(wi, 1, 2)
        wi_spec = pl.BlockSpec((1, N_IDX_HEADS, Tq), lambda b, i: (b, 0, i))
    else:
        wi_spec = qspec(N_IDX_HEADS)
    in_specs = [qspec(ATT_WIDTH), qspec(N_IDX_HEADS * IDX_DIM), wi_spec]
    args = [q, qi, wi]
    if P > 0:
        in_specs += [full(N_KV_HEADS, HEAD_DIM, P), full(N_KV_HEADS, HEAD_DIM, P), full(IDX_DIM, P),
                     full(KV_WIDTH, Tn), full(KV_WIDTH, Tn), full(IDX_DIM, Tn)]
        args += list(hist)
        kv_scratch, ki_scratch = (N_KV_HEADS, LANES, Lp), (LANES, Lp)
    else:
        in_specs += [full(Tn, KV_WIDTH), full(Tn, KV_WIDTH), full(Tn, IDX_DIM)]
        kv_scratch, ki_scratch = (N_KV_HEADS, Lp, LANES), (Lp, LANES)
    in_specs.append(pl.BlockSpec(memory_space=pltpu.SMEM))
    args += list(new) + [rel_bias.astype(F32)]
    n_cls = Lp // 256 if (P == 0 and Lp % 256 == 0) else 1
    kern = functools.partial(_dsa_kernel, Tq=Tq, P=P, Tn=Tn, Lp=Lp, topk=topk, n_cls=n_cls)
    return pl.pallas_call(
        kern,
        grid=(B, T // Tq),
        in_specs=in_specs,
        out_specs=pl.BlockSpec((1, Tq, ATT_WIDTH), lambda b, i: (b, i, 0)),
        out_shape=jax.ShapeDtypeStruct((B, T, ATT_WIDTH), BF16),
        scratch_shapes=[pltpu.VMEM(kv_scratch, BF16),
                        pltpu.VMEM(kv_scratch, BF16),
                        pltpu.VMEM(ki_scratch, BF16),
                        pltpu.VMEM((GROUP * Tq, Lp), F32),
                        pltpu.VMEM((N_HEADS, Tq, 2 * LANES), F32),
                        pltpu.VMEM((Tq, Lp), F32)],
        compiler_params=pltpu.CompilerParams(dimension_semantics=("arbitrary", "arbitrary"),
                                             vmem_limit_bytes=VMEM_LIMIT),
        name="dsa_attention",
    )(*args)


ATT_COLS = 2 * ATT_WIDTH + 2 * KV_WIDTH + LANES
_C_Q, _C_QI, _C_K, _C_V, _C_KI = 0, 512, 1024, 1152, 1280
_C_WI = _C_KI + IDX_DIM


def _rms(x, g):
    return x * lax.rsqrt(jnp.mean(x * x, axis=-1, keepdims=True) + NORM_EPS) * g


def _inproj_kernel(x_ref, g_ref, wa_ref, wr_ref, q_ref, qi_ref, k_ref, v_ref, ki_ref, wi_ref, prw_ref,
                   *t_refs):
    h = _rms(x_ref[...], g_ref[...]).astype(BF16)
    pa = jnp.dot(h, wa_ref[...], preferred_element_type=F32)
    q_ref[...] = (pa[:, _C_Q:_C_Q + ATT_WIDTH] * (HEAD_DIM ** -0.5)).astype(BF16)
    qi_ref[...] = (pa[:, _C_QI:_C_QI + ATT_WIDTH] * (IDX_DIM ** -0.5)).astype(BF16)
    k_ref[...] = pa[:, _C_K:_C_K + KV_WIDTH]
    v_ref[...] = pa[:, _C_V:_C_V + KV_WIDTH]
    ki_ref[...] = pa[:, _C_KI:_C_KI + IDX_DIM]
    wi_ref[...] = pa[:, _C_WI:_C_WI + N_IDX_HEADS] * (N_IDX_HEADS ** -0.5)
    prw_ref[...] = jnp.dot(h, wr_ref[...], preferred_element_type=F32)
    if t_refs:
        kT_ref, vT_ref, kiT_ref = t_refs
        kT_ref[0] = pa[:, _C_K:_C_K + KV_WIDTH].T
        vT_ref[0] = pa[:, _C_V:_C_V + KV_WIDTH].T
        kiT_ref[0] = pa[:, _C_KI:_C_KI + LANES].T[0:IDX_DIM]


def _const_spec(shape):
    return pl.BlockSpec(shape, lambda i: (0,) * len(shape), pipeline_mode=pl.Buffered(1))


def _inproj_call(x2d, g, wa, wr, tm, t_per_batch=None):
    N = x2d.shape[0]
    assert N % tm == 0
    rows = lambda w: pl.BlockSpec((tm, w), lambda i: (i, 0))
    widths = [ATT_WIDTH, ATT_WIDTH, KV_WIDTH, KV_WIDTH, IDX_DIM, N_IDX_HEADS, RW_COLS]
    dtypes = [BF16, BF16, F32, F32, F32, F32, F32]
    out_specs = [rows(w) for w in widths]
    out_shape = [jax.ShapeDtypeStruct((N, w), d) for w, d in zip(widths, dtypes)]
    if t_per_batch is not None:
        assert t_per_batch % tm == 0
        tpb = t_per_batch // tm
        for feat in (KV_WIDTH, KV_WIDTH, IDX_DIM):
            out_specs.append(pl.BlockSpec((1, feat, tm), lambda i: (i // tpb, 0, i % tpb)))
            out_shape.append(jax.ShapeDtypeStruct((N // t_per_batch, feat, t_per_batch), F32))
    return pl.pallas_call(
        _inproj_kernel,
        grid=(N // tm,),
        in_specs=[rows(D_MODEL), _const_spec((1, D_MODEL)), _const_spec((D_MODEL, ATT_COLS)),
                  _const_spec((D_MODEL, RW_COLS))],
        out_specs=out_specs,
        out_shape=out_shape,
        compiler_params=pltpu.CompilerParams(dimension_semantics=("arbitrary",),
                                             vmem_limit_bytes=VMEM_LIMIT),
        name="rmsnorm_inproj",
    )(x2d, g, wa, wr)


def _tail_kernel(x_ref, att_ref, rw_ref, g1_ref, wgate_ref, wao_ref, wro_ref, wout_ref,
                 g2_ref, wff1_ref, wff2_ref, gf_ref, y_ref):
    x = x_ref[...]
    h = _rms(x, g1_ref[...]).astype(BF16)
    gates = jax.nn.sigmoid(jnp.dot(h, wgate_ref[...], preferred_element_type=F32))
    z_a = jnp.dot(att_ref[...], wao_ref[...], preferred_element_type=F32)
    z_b = jnp.dot(rw_ref[...], wro_ref[...], preferred_element_type=F32)
    mixed = gates[:, 0:D_MODEL] * z_a + gates[:, D_MODEL:2 * D_MODEL] * z_b
    x1 = x + jnp.dot(mixed.astype(BF16), wout_ref[...], preferred_element_type=F32)
    h2 = _rms(x1, g2_ref[...]).astype(BF16)
    f = jnp.maximum(jnp.dot(h2, wff1_ref[...], preferred_element_type=F32), 0.0)
    x2 = x1 + jnp.dot((f * f).astype(BF16), wff2_ref[...], preferred_element_type=F32)
    y_ref[...] = _rms(x2, gf_ref[...])


def _tail_call(x2d, att, rw, g1, wgate, wao, wro, wout, g2, wff1, wff2, gf, tm):
    N = x2d.shape[0]
    assert N % tm == 0
    rows = lambda w: pl.BlockSpec((tm, w), lambda i: (i, 0))
    return pl.pallas_call(
        _tail_kernel,
        grid=(N // tm,),
        in_specs=[rows(D_MODEL), rows(ATT_WIDTH), rows(RW_WIDTH), _const_spec((1, D_MODEL)),
                  _const_spec((D_MODEL, 2 * D_MODEL)), _const_spec((ATT_WIDTH, D_MODEL)),
                  _const_spec((RW_WIDTH, D_MODEL)), _const_spec((D_MODEL, D_MODEL)),
                  _const_spec((1, D_MODEL)), _const_spec((D_MODEL, D_FF)), _const_spec((D_FF, D_MODEL)),
                  _const_spec((1, D_MODEL))],
        out_specs=rows(D_MODEL),
        out_shape=jax.ShapeDtypeStruct((N, D_MODEL), F32),
        compiler_params=pltpu.CompilerParams(dimension_semantics=("arbitrary",),
                                             vmem_limit_bytes=VMEM_LIMIT),
        name="merge_out_mlp",
    )(x2d, att, rw, g1, wgate, wao, wro, wout, g2, wff1, wff2, gf)


_IN_SPLITS = (ATT_WIDTH, KV_WIDTH, KV_WIDTH, N_IDX_HEADS * IDX_DIM, IDX_DIM, N_IDX_HEADS,
              RW_COLS, D_MODEL, D_MODEL)


def _split_w_in(w_in):
    offs = [0]
    for s in _IN_SPLITS:
        offs.append(offs[-1] + s)
    col = lambda n: w_in[:, offs[n]:offs[n + 1]]
    pad = jnp.zeros((D_MODEL, ATT_COLS - _C_WI - N_IDX_HEADS), w_in.dtype)
    wa = jnp.concatenate([col(0), col(3), col(1), col(2), col(4), col(5), pad], axis=1).astype(BF16)
    wr = col(6).astype(BF16)
    wgate = jnp.concatenate([col(7), col(8)], axis=1).astype(BF16)
    return wa, wr, wgate


def _branch(x, hist, wkv0, shift0, rel_bias, prm, w, tm_in, tm_tail, Tq, C, nb_rw):
    B, T, _ = x.shape
    x2d = x.reshape(B * T, D_MODEL)
    r3 = lambda a: a.reshape(B, T, a.shape[-1])
    heads_last = lambda aT: jnp.transpose(aT.reshape(B, N_KV_HEADS, HEAD_DIM, T), (0, 3, 1, 2))[None]
    if hist is None:
        q, qi, k, v, ki, wi, p_rw, kT, vT, kiT = _inproj_call(x2d, prm['g1'], w['wa'], w['wr'], tm_in, T)
        new = (r3(k), r3(v), r3(ki))
        new_k, new_v, new_ki = heads_last(kT), heads_last(vT), jnp.swapaxes(kiT, 1, 2)[None]
    else:
        q, qi, k, v, ki, wi, p_rw = _inproj_call(x2d, prm['g1'], w['wa'], w['wr'], tm_in)
        new = tuple(jnp.swapaxes(r3(a), 1, 2) for a in (k, v, ki))
        new_k = k.reshape(1, B, T, N_KV_HEADS, HEAD_DIM)
        new_v = v.reshape(1, B, T, N_KV_HEADS, HEAD_DIM)
        new_ki = ki.reshape(1, B, T, IDX_DIM)
    rw, hT = _rwkv_call(r3(p_rw), shift0, _state_to_bd(wkv0), prm, C, nb_rw)
    att = _dsa_call(r3(q), r3(qi), r3(wi), hist, new, rel_bias, Tq)
    y = _tail_call(x2d, att.reshape(B * T, ATT_WIDTH), rw.reshape(B * T, RW_WIDTH), prm['g1'],
                   w['wgate'], w['wao'], w['wro'], w['wout'], prm['g2'], w['wff1'], w['wff2'],
                   prm['gf'], tm_tail)
    wkv = _bd_to_state(hT)[None]
    shift = r3(p_rw)[:, T - 1:T, :][None]
    return y.reshape(B, T, D_MODEL), new_k, new_v, new_ki, wkv, shift


def kernel(x_prompt, x_sample, cache_k, cache_v, cache_kidx, state_wkv, state_shift, rel_bias,
           norm1_g, w_in, w_att_out, mu_shift, w0, w_decay_up, a0, w_a_up, w_g_up, k_k, k_a, r_k,
           lnx_w, lnx_b, w_rw_out, w_out, norm2_g, w_ff1, w_ff2, norm_f_g):
    assert w_in.shape[0] == 1, "single-layer step"
    wa, wr, wgate = _split_w_in(w_in[0])
    w = dict(wa=wa, wr=wr, wgate=wgate, wao=w_att_out[0].astype(BF16), wro=w_rw_out[0].astype(BF16),
             wout=w_out[0].astype(BF16), wff1=w_ff1[0].astype(BF16), wff2=w_ff2[0].astype(BF16))
    prm = dict(mu_shift=mu_shift[0], w0=w0[0], w_decay_up=w_decay_up[0], a0=a0[0], w_a_up=w_a_up[0],
               w_g_up=w_g_up[0], k_k=k_k[0], k_a=k_a[0], r_k=r_k[0].reshape(-1), lnx_w=lnx_w[0],
               lnx_b=lnx_b[0], g1=norm1_g[0].reshape(1, -1), g2=norm2_g[0].reshape(1, -1),
               gf=norm_f_g.reshape(1, -1))
    Bp, Tp, _ = x_prompt.shape
    Bs, Ts, _ = x_sample.shape
    past = cache_k.shape[2]
    zero_state = jnp.zeros((Bp, RW_HEADS, RW_HEAD, RW_HEAD), F32)
    zero_shift = jnp.zeros((Bp, 1, RW_COLS), F32)
    outs_p = _branch(x_prompt, None, zero_state, zero_shift, rel_bias, prm, w,
                     tm_in=512, tm_tail=256, Tq=Q_BLOCK, C=CHUNK, nb_rw=4)
    hist = (jnp.transpose(cache_k[0], (0, 2, 3, 1)), jnp.transpose(cache_v[0], (0, 2, 3, 1)),
            jnp.swapaxes(cache_kidx[0], 1, 2))
    outs_s = _branch(x_sample, hist, state_wkv[0], state_shift[0], rel_bias, prm, w,
                     tm_in=Bs * Ts, tm_tail=256, Tq=Ts, C=Ts, nb_rw=4)
    return (outs_p[0], outs_s[0]) + outs_p[1:] + outs_s[1:]
```

```python
import functools
import math

import jax
import jax.numpy as jnp
import numpy as np
from jax import lax
from jax.experimental import pallas as pl
from jax.experimental.pallas import tpu as pltpu

F32 = jnp.float32
BF16 = jnp.bfloat16

D_MODEL = 1024
CHUNK = 64
N_HEADS = 8
N_KV_HEADS = 2
HEAD_DIM = 64
GROUP = N_HEADS // N_KV_HEADS
ATT_WIDTH = N_HEADS * HEAD_DIM
KV_WIDTH = N_KV_HEADS * HEAD_DIM
N_IDX_HEADS = 8
IDX_DIM = 64
MAX_TOPK = 256
Q_BLOCK = 128
N_BUCKETS = 32
RW_HEADS = 8
RW_HEAD = 64
RW_WIDTH = RW_HEADS * RW_HEAD
DECAY_LORA = 64
AAA_LORA = 64
GATE_LORA = 128
RW_COLS = 3 * RW_WIDTH + DECAY_LORA + AAA_LORA + GATE_LORA
GN_EPS = 64e-5
D_FF = 4 * D_MODEL
NORM_EPS = 1e-6
NEG = -1e30

LANES = 128
N_PAIRS = RW_HEADS // 2
VMEM_LIMIT = 56 * 1024 * 1024

CHUNK_PIECES = 1
SEG_PIECES = 2

_NN, _NT, _TN = (1, 0), (1, 1), (0, 0)


def _split(x, n):
    pieces = []
    rem = x
    for i in range(n):
        p = rem.astype(BF16)
        pieces.append(p)
        if i + 1 < n:
            rem = rem - p.astype(F32)
    return tuple(pieces)


def _mm(a, b, dims=_NN):
    n = max(len(a), len(b))
    terms = [(i, j) for i in range(len(a)) for j in range(len(b)) if i + j < n]
    ca, cb = dims
    lhs = jnp.concatenate([a[i] for i, _ in terms], axis=ca) if len(terms) > 1 else a[0]
    rhs = jnp.concatenate([b[j] for _, j in terms], axis=cb) if len(terms) > 1 else b[0]
    return lax.dot_general(lhs, rhs, (((ca,), (cb,)), ((), ())), preferred_element_type=F32)


def _lane_iota(shape):
    return lax.broadcasted_iota(jnp.int32, shape, len(shape) - 1)


def _row_iota(shape):
    return lax.broadcasted_iota(jnp.int32, shape, len(shape) - 2)


def _bdrows(x, width, heads=2):
    head = (_lane_iota(x.shape) % (heads * width)) // width
    return jnp.concatenate([jnp.where(head == h, x, 0.0) for h in range(heads)], axis=0)


def _rwkv_kernel(x_ref, shift0_ref, h0_ref, mu_ref, w0_ref, a0_ref, kk_ref, ka_ref, rk_ref,
                 lnw_ref, lnb_ref, wlora_ref, wg_ref,
                 out_ref, hT_ref, xbuf, hst, *, C, n_sq, nb):
    c = pl.program_id(1)

    @pl.when(c == 0)
    def _():
        for n in range(nb):
            xbuf[n, 7:8, :] = shift0_ref[n]
        hst[...] = h0_ref[...]

    W = RW_WIDTH
    R = nb * C
    prevs = []
    for n in range(nb):
        xn = x_ref[n]
        xbuf[n, 8:8 + C, :] = xn
        prevs.append(xbuf[n, 7:7 + C, :])
        xbuf[n, 7:8, :] = xn[C - 1:C, :]
    x = x_ref[...].reshape(R, RW_COLS)
    prev = jnp.concatenate(prevs, axis=0)
    xs = x + (prev - x) * mu_ref[...]
    r = xs[:, 0:W]
    k = xs[:, W:2 * W]
    v = xs[:, 2 * W:3 * W]
    da = xs[:, 3 * W:3 * W + LANES]
    gd = xs[:, 3 * W + LANES:3 * W + 2 * LANES]

    t_in = jnp.where(_lane_iota(da.shape) < DECAY_LORA, jnp.tanh(da), da)
    lo = jnp.dot(t_in.astype(BF16), wlora_ref[...], preferred_element_type=F32)
    ld = -math.exp(-0.5) * jax.nn.sigmoid(w0_ref[...] + lo[:, 0:W])
    a = jax.nn.sigmoid(a0_ref[...] + lo[:, W:2 * W])
    g = jnp.dot(jax.nn.sigmoid(gd).astype(BF16), wg_ref[...], preferred_element_type=F32)

    seg = (_row_iota((LANES, LANES)) // RW_HEAD == _lane_iota((LANES, LANES)) // RW_HEAD).astype(BF16)

    def segsum(t):
        return jnp.concatenate(
            [_mm(_split(t[:, p * LANES:(p + 1) * LANES], SEG_PIECES), (seg,)) for p in range(N_PAIRS)],
            axis=1)

    kk0 = k * kk_ref[...]
    kk = kk0 * jnp.minimum(lax.rsqrt(segsum(kk0 * kk0)), 1e12)
    k2 = k * (1.0 + (a - 1.0) * ka_ref[...])
    bb = kk * a

    ti, si = _row_iota((R, R)), _lane_iota((R, R))
    tri = jnp.logical_and(si <= ti, si // C == ti // C).astype(BF16)
    lp = _mm((tri,), _split(ld, 3))
    lpp = lp - ld
    rows_of = lambda i: jnp.concatenate(
        [jnp.broadcast_to(lp[n * C + i:n * C + i + 1, :], (C, W)) for n in range(nb)], axis=0)
    lpc = rows_of(C - 1)
    mid = rows_of(C // 2 - 1)
    r_abs = r * jnp.exp(lp)
    kap_abs = kk * jnp.exp(lpp)
    r_mid = r * jnp.exp(lp - mid)
    kap_mid = kk * jnp.exp(lpp - mid)
    e_mid = jnp.exp(mid - lp)
    b_mid = bb * e_mid
    k_mid = k2 * e_mid
    e_end = jnp.exp(lpc - lp)
    b_end = bb * e_end
    k_end = k2 * e_end
    p_end = jnp.exp(lpc)

    tt = _row_iota((C, 2 * C))
    ss = _lane_iota((C, 2 * C)) % C
    strict = ss < tt
    incl = ss <= tt
    eye_p = (ss == tt).astype(F32)
    jrow = _row_iota((LANES, LANES))
    jlane = _lane_iota((LANES, LANES))
    same_head = (jrow // RW_HEAD) == (jlane // RW_HEAD)
    eye_j = jrow == jlane

    sp = lambda t: _split(t, CHUNK_PIECES)
    bd = lambda pieces, half: tuple(_bdrows(t, half) for t in pieces)

    chains = [(n, p) for n in range(nb) for p in range(N_PAIRS)]
    blk = lambda t, ch: t[ch[0] * C:(ch[0] + 1) * C, ch[1] * LANES:(ch[1] + 1) * LANES]
    each = lambda f: [f(i, ch) for i, ch in enumerate(chains)]

    A = each(lambda i, ch: _mm(
        sp(jnp.concatenate([blk(kap_mid, ch), blk(r_mid, ch)], axis=0)),
        sp(jnp.concatenate([_bdrows(blk(b_mid, ch), RW_HEAD), _bdrows(blk(k_mid, ch), RW_HEAD)],
                           axis=0)), _NT))
    n_kb = each(lambda i, ch: jnp.where(strict, A[i][0:C, 0:2 * C], 0.0))
    a_kk = each(lambda i, ch: jnp.where(strict, A[i][0:C, 2 * C:4 * C], 0.0))
    a_rb = each(lambda i, ch: jnp.where(incl, A[i][C:2 * C, 0:2 * C], 0.0))
    a_rk = each(lambda i, ch: jnp.where(incl, A[i][C:2 * C, 2 * C:4 * C], 0.0))
    T = each(lambda i, ch: eye_p - n_kb[i])
    pw = each(lambda i, ch: sp(n_kb[i]))
    for _ in range(n_sq):
        pw = each(lambda i, ch: sp(_mm(pw[i], bd(pw[i], C))))
        T = each(lambda i, ch: T[i] + _mm(sp(T[i]), bd(pw[i], C)))
    v_p = each(lambda i, ch: sp(blk(v, ch)))
    av = each(lambda i, ch: _mm(sp(jnp.concatenate([a_kk[i], a_rk[i]], axis=0)),
                                bd(v_p[i], RW_HEAD)))
    wu = each(lambda i, ch: sp(_mm(
        sp(T[i]), bd(sp(jnp.concatenate([blk(kap_abs, ch), av[i][0:C]], axis=1)), RW_HEAD))))
    qy = each(lambda i, ch: jnp.concatenate([blk(r_abs, ch), av[i][C:2 * C]], axis=1)
              - _mm(sp(a_rb[i]), bd(wu[i], RW_HEAD)))
    bw = each(lambda i, ch: _mm(sp(blk(b_end, ch)), wu[i], _TN))
    kv = each(lambda i, ch: _mm(_split(blk(k_end, ch), 2), _split(blk(v, ch), 2), _TN))
    G = each(lambda i, ch: jnp.where(same_head, kv[i] - bw[i][:, LANES:2 * LANES], 0.0))
    M = each(lambda i, ch: jnp.where(same_head, bw[i][:, 0:LANES], 0.0))
    p_col = each(lambda i, ch: jnp.sum(jnp.where(eye_j, blk(p_end, ch)[0:1], 0.0), axis=1,
                                       keepdims=True))
    h0f = each(lambda i, ch: hst[ch[0], ch[1]])
    h0 = each(lambda i, ch: sp(h0f[i]))
    yc_ = each(lambda i, ch: _mm(sp(qy[i][:, 0:LANES]), h0[i]) + qy[i][:, LANES:2 * LANES])
    for i, ch in enumerate(chains):
        hst[ch[0], ch[1]] = p_col[i] * h0f[i] - _mm(sp(M[i]), h0[i]) + G[i]
    y = jnp.concatenate([jnp.concatenate(yc_[n * N_PAIRS:(n + 1) * N_PAIRS], axis=1)
                         for n in range(nb)], axis=0)

    mean = segsum(y) * (1.0 / RW_HEAD)
    yc = y - mean
    var = segsum(yc * yc) * (1.0 / RW_HEAD)
    yn = yc * lax.rsqrt(var + GN_EPS) * lnw_ref[...] + lnb_ref[...]
    bonus = segsum(r * k2 * rk_ref[...]) * v
    out_ref[...] = ((yn + bonus) * g).astype(out_ref.dtype).reshape(nb, C, W)

    @pl.when(c == pl.num_programs(1) - 1)
    def _():
        hT_ref[...] = hst[...]


def _rwkv_call(p_rw, shift0, h0_bd, prm, C, nb):
    B, T, _ = p_rw.shape
    n_sq = int(math.log2(C)) - 1
    assert 2 ** (n_sq + 1) == C and T % C == 0 and B % nb == 0
    row = lambda a: a.reshape(1, -1).astype(F32)
    vecs = [row(prm['mu_shift']), row(prm['w0']), row(prm['a0']), row(prm['k_k']), row(prm['k_a']),
            row(prm['r_k']), row(prm['lnx_w']), row(prm['lnx_b'])]
    wlora = jnp.zeros((LANES, 2 * RW_WIDTH), F32)
    wlora = wlora.at[0:DECAY_LORA, 0:RW_WIDTH].set(prm['w_decay_up'])
    wlora = wlora.at[DECAY_LORA:, RW_WIDTH:].set(prm['w_a_up']).astype(BF16)
    wg = prm['w_g_up'].astype(BF16)
    const = lambda shape: pl.BlockSpec(shape, lambda b, c: (0,) * len(shape))
    kern = functools.partial(_rwkv_kernel, C=C, n_sq=n_sq, nb=nb)
    return pl.pallas_call(
        kern,
        grid=(B // nb, T // C),
        in_specs=[pl.BlockSpec((nb, C, RW_COLS), lambda b, c: (b, c, 0)),
                  pl.BlockSpec((nb, 1, RW_COLS), lambda b, c: (b, 0, 0)),
                  pl.BlockSpec((nb, N_PAIRS, LANES, LANES), lambda b, c: (b, 0, 0, 0)),
                  const((1, RW_COLS))] + [const((1, RW_WIDTH))] * 7
                 + [const((LANES, 2 * RW_WIDTH)), const((GATE_LORA, RW_WIDTH))],
        out_specs=[pl.BlockSpec((nb, C, RW_WIDTH), lambda b, c: (b, c, 0)),
                   pl.BlockSpec((nb, N_PAIRS, LANES, LANES), lambda b, c: (b, 0, 0, 0))],
        out_shape=[jax.ShapeDtypeStruct((B, T, RW_WIDTH), BF16),
                   jax.ShapeDtypeStruct((B, N_PAIRS, LANES, LANES), F32)],
        scratch_shapes=[pltpu.VMEM((nb, 8 + C, RW_COLS), F32),
                        pltpu.VMEM((nb, N_PAIRS, LANES, LANES), F32)],
        compiler_params=pltpu.CompilerParams(dimension_semantics=("arbitrary", "arbitrary"),
                                             vmem_limit_bytes=VMEM_LIMIT),
        name="rwkv7_chunked",
    )(p_rw, shift0, h0_bd, *vecs, wlora, wg)


def _state_to_bd(wkv):
    B = wkv.shape[0]
    h = jnp.swapaxes(wkv, -1, -2).reshape(B, N_PAIRS, 2, RW_HEAD, RW_HEAD)
    z = jnp.zeros_like(h[:, :, 0])
    top = jnp.concatenate([h[:, :, 0], z], axis=-1)
    bot = jnp.concatenate([z, h[:, :, 1]], axis=-1)
    return jnp.concatenate([top, bot], axis=-2)


def _bd_to_state(hbd):
    B = hbd.shape[0]
    h0 = hbd[:, :, 0:RW_HEAD, 0:RW_HEAD]
    h1 = hbd[:, :, RW_HEAD:, RW_HEAD:]
    h = jnp.stack([h0, h1], axis=2).reshape(B, RW_HEADS, RW_HEAD, RW_HEAD)
    return jnp.swapaxes(h, -1, -2)


_INT_MIN = -2 ** 31
_NEG_KEY = int(np.float32(NEG).view(np.int32)) ^ 0x7FFFFFFF
_T5_STEPS = ((91, 14), (64, 13), (46, 12), (32, 11), (23, 10), (16, 9), (12, 8), (8, 7), (7, 6),
             (6, 5), (5, 4), (4, 3), (3, 2), (2, 1), (1, 0))
_T5_FAR = 15
_KEY_TILE = 256


def _half_masked(x, half):
    low = (_lane_iota(x.shape) % LANES) < HEAD_DIM
    keep = low if half == 0 else jnp.logical_not(low)
    return jnp.where(keep, x, jnp.zeros_like(x))


def _dup_half(x, half):
    xr = pltpu.roll(x, HEAD_DIM, axis=1)
    low = _lane_iota(x.shape) < HEAD_DIM
    return jnp.where(low, x, xr) if half == 0 else jnp.where(low, xr, x)


def _dot_nt(a, b):
    return lax.dot_general(a, b, (((1,), (1,)), ((), ())), preferred_element_type=F32)


def _dsa_kernel(*refs, Tq, P, Tn, Lp, topk, n_cls):
    if P > 0:
        (q_ref, qi_ref, wi_ref, hk_ref, hv_ref, hki_ref, nk_ref, nv_ref, nki_ref, rb_ref,
         att_ref, kd, vd, kid, lg, tbl, sbias) = refs
    else:
        (q_ref, qi_ref, wi_ref, nk_ref, nv_ref, nki_ref, rb_ref,
         att_ref, kd, vd, kid, lg, tbl, sbias) = refs
    b = pl.program_id(0)
    i = pl.program_id(1)
    L = P + Tn

    @pl.when(jnp.logical_and(b == 0, i == 0))
    def _():
        shape = (Tq, 2 * LANES)
        rel = _lane_iota(shape) - LANES - _row_iota(shape)
        n = jnp.abs(rel)
        for h in range(N_HEADS):
            far = rb_ref[_T5_FAR, h]
            neg = jnp.full(shape, far, F32)
            pos = jnp.full(shape, rb_ref[_T5_FAR + N_BUCKETS // 2, h], F32)
            for bound, bucket in _T5_STEPS:
                neg = jnp.where(n < bound, rb_ref[bucket, h], neg)
                pos = jnp.where(n < bound, rb_ref[bucket + N_BUCKETS // 2, h], pos)
            tbl[h, :, 0:2 * LANES] = jnp.where(rel > 0, pos, neg) - far
            tbl[h, :, 2 * LANES:3 * LANES] = jnp.zeros((Tq, LANES), F32)

    @pl.when(i == 0)
    def _():
        if P > 0:
            def put(dst, x, c0):
                xb = x.astype(BF16)
                dst[0:HEAD_DIM, c0:c0 + x.shape[1]] = xb
                dst[HEAD_DIM:LANES, c0:c0 + x.shape[1]] = xb

            zt = jnp.zeros((LANES, Lp - P), BF16)
            for j in range(N_KV_HEADS):
                put(kd.at[j], hk_ref[0, j], 0)
                kd[j, :, P:Lp] = zt
                put(kd.at[j], nk_ref[0, j * HEAD_DIM:(j + 1) * HEAD_DIM, :], P)
                for u in range(2):
                    vd[2 * j + u] = jnp.ones((LANES, Lp), BF16)
                    r0 = u * HEAD_DIM
                    vd[2 * j + u, r0:r0 + HEAD_DIM, 0:P] = hv_ref[0, j].astype(BF16)
                    vd[2 * j + u, r0:r0 + HEAD_DIM, P:P + Tn] = (
                        nv_ref[0, j * HEAD_DIM:(j + 1) * HEAD_DIM, :].astype(BF16))
            put(kid, hki_ref[0], 0)
            kid[:, P:Lp] = zt
            put(kid, nki_ref[0], P)
        else:
            kf, vf, kif = nk_ref[0], nv_ref[0], nki_ref[0]
            low = _lane_iota(vf.shape) < HEAD_DIM
            for j in range(N_KV_HEADS):
                kd[j, 0:Tn, :] = _dup_half(kf, j).astype(BF16)
                vdup = _dup_half(vf, j)
                vd[2 * j, 0:Tn, :] = jnp.where(low, vdup, 1.0).astype(BF16)
                vd[2 * j + 1, 0:Tn, :] = jnp.where(low, 1.0, vdup).astype(BF16)
            kid[0:Tn, :] = jnp.concatenate([kif, kif], axis=1).astype(BF16)

    stride = Lp // n_cls
    k_hi = P + (i + 1) * Tq
    cls = (k_hi - 1) // stride
    for c in range(n_cls):
        pl.when(cls == c)(functools.partial(
            _dsa_block, q_ref, qi_ref, wi_ref, att_ref, kd, vd, kid, lg, tbl, sbias,
            i=i, Tq=Tq, P=P, L=L, Lk=(c + 1) * stride, topk=topk))


def _dsa_block(q_ref, qi_ref, wi_ref, att_ref, kd, vd, kid, lg, tbl, sbias, *, i, Tq, P, L, Lk, topk):
    q = q_ref[0]
    qi = qi_ref[0]
    key_major = Tq % LANES == 0
    kax = 0 if key_major else 1
    shape = (Lk, Tq) if key_major else (Tq, Lk)
    qvec = (1, Tq) if key_major else (Tq, 1)
    wi = wi_ref[0]

    lanes_keys = P > 0
    kidv = kid[:, 0:Lk] if lanes_keys else kid[0:Lk, :]
    head_q = lambda h: _half_masked(qi[:, (h // 2) * LANES:(h // 2 + 1) * LANES], h % 2)
    score = jnp.zeros(shape, F32)
    if key_major:
        pair_q = [jnp.concatenate([head_q(2 * pr), head_q(2 * pr + 1)], axis=0)
                  for pr in range(N_IDX_HEADS // 2)]
        tiles = []
        for kt in range(Lk // _KEY_TILE):
            kid_t = kidv[kt * _KEY_TILE:(kt + 1) * _KEY_TILE]
            acc = jnp.zeros((_KEY_TILE, Tq), F32)
            for pr in range(N_IDX_HEADS // 2):
                s2 = _dot_nt(kid_t, pair_q[pr])
                for u in range(2):
                    h = 2 * pr + u
                    acc = acc + wi[h:h + 1, :] * jnp.maximum(s2[:, u * Tq:(u + 1) * Tq], 0.0)
            tiles.append(acc)
        score = jnp.concatenate(tiles, axis=0)
    else:
        lhs_all = jnp.concatenate([head_q(h) for h in range(N_IDX_HEADS)], axis=0)
        s_all = (jnp.dot(lhs_all, kidv, preferred_element_type=F32) if lanes_keys
                 else _dot_nt(lhs_all, kidv))
        for h in range(N_IDX_HEADS):
            score = score + wi[:, h:h + 1] * jnp.maximum(s_all[h * Tq:(h + 1) * Tq], 0.0)

    s_pos = lax.broadcasted_iota(jnp.int32, shape, kax)
    q_pos = P + i * Tq + lax.broadcasted_iota(jnp.int32, shape, 1 - kax)
    adm = jnp.logical_and(s_pos // CHUNK <= q_pos // CHUNK, s_pos < L)
    score = jnp.where(adm, score, NEG)

    def count(mask):
        ones = jnp.where(mask, 1.0, 0.0)
        if key_major:
            ones = jnp.sum(ones.reshape(Lk // LANES, LANES, Tq), axis=0)
        return jnp.sum(ones, axis=kax, keepdims=True)

    def key_to_float(key):
        return pltpu.bitcast(jnp.where(key < 0, key ^ jnp.int32(0x7FFFFFFF), key), F32)

    kf = float(topk)
    ans = jnp.where(count(score >= 0.0) >= kf, jnp.int32(0), jnp.int32(_INT_MIN))

    def bit_step(it, ans):
        cand = ans | (jnp.int32(1) << (30 - it))
        return jnp.where(count(score >= key_to_float(cand)) >= kf, cand, ans)

    def two_bit_step(it, ans):
        shift = 28 - 2 * it
        for m in (1, 2, 3):
            cand = ans | (jnp.int32(m) << shift)
            ans_m = jnp.where(count(score >= key_to_float(cand)) >= kf, cand, ans)
            best = ans_m if m == 1 else jnp.maximum(best, ans_m)
        return best

    def stacked_q(j):
        return jnp.concatenate(
            [_half_masked(q[:, ((j * GROUP + g) // 2) * LANES:((j * GROUP + g) // 2 + 1) * LANES], g % 2)
             for g in range(GROUP)], axis=0)

    def logits_rows(j, lhs_rows, rsl):
        lg[j, rsl, 0:Lk] = (jnp.dot(lhs_rows, kd[j, :, 0:Lk], preferred_element_type=F32) if lanes_keys
                            else _dot_nt(lhs_rows, kd[j, 0:Lk, :]))

    if key_major:
        thr_key = lax.fori_loop(0, 31, bit_step, ans)
    else:
        thr_key = lax.fori_loop(0, 15, two_bit_step, bit_step(0, ans))
    thr_key = jnp.maximum(thr_key, jnp.int32(_NEG_KEY))
    thr = key_to_float(thr_key)
    gt = score > thr
    eq = score == thr
    need = kf - count(gt)
    tied = jnp.logical_and(count(eq) > need, thr > NEG)
    any_tied = jnp.max(jnp.where(tied, 1.0, 0.0)) > 0.0

    n_idx_bits = (Lk - 1).bit_length()

    def idx_search():
        def idx_step(it, x):
            cand = x | (jnp.int32(1) << (n_idx_bits - 1 - it))
            return jnp.where(count(jnp.logical_and(eq, s_pos < cand)) < need, cand, x)
        return lax.fori_loop(0, n_idx_bits, idx_step, jnp.zeros(qvec, jnp.int32))

    last = lax.cond(any_tied, idx_search, lambda: jnp.full(qvec, Lk, jnp.int32))
    sel = jnp.logical_and(jnp.logical_or(gt, jnp.logical_and(eq, s_pos <= last)), adm)
    sb = jnp.where(sel, 0.0, NEG)
    sbias[:, 0:Lk] = sb.T if key_major else sb

    blk_end = (P + i * Tq) // LANES + 1
    start = pl.multiple_of(jnp.maximum(blk_end - 2, 0) * LANES, LANES)
    t_off = pl.multiple_of(jnp.maximum(2 - blk_end, 0) * LANES, LANES)
    for j in range(N_KV_HEADS):
        logits_rows(j, stacked_q(j), slice(0, GROUP * Tq))
    for j in range(N_KV_HEADS):
        for g in range(GROUP):
            r = slice(g * Tq, (g + 1) * Tq)
            lg[j, r, pl.ds(start, 2 * LANES)] = (lg[j, r, pl.ds(start, 2 * LANES)]
                                                 + tbl[j * GROUP + g, :, pl.ds(t_off, 2 * LANES)])
    for j in range(N_KV_HEADS):
        sb = sbias[:, 0:Lk]
        ps = []
        for g in range(GROUP):
            lgv = lg[j, g * Tq:(g + 1) * Tq, 0:Lk]
            m = jnp.max(lgv + sb, axis=1, keepdims=True)
            ps.append(jnp.exp((lgv - m) + sb).astype(BF16))
        pv = lambda p, u: (_dot_nt(p, vd[2 * j + u, :, 0:Lk]) if lanes_keys
                           else jnp.dot(p, vd[2 * j + u, 0:Lk, :], preferred_element_type=F32))
        if key_major:
            os_ = [pv(ps[g], g % 2) for g in range(GROUP)]
        else:
            stacked = [pv(jnp.concatenate(ps[u::2], axis=0), u) for u in range(2)]
            os_ = [stacked[g % 2][(g // 2) * Tq:(g // 2 + 1) * Tq] for g in range(GROUP)]
        low = _lane_iota((Tq, LANES)) < HEAD_DIM
        for pp in range(GROUP // 2):
            c0 = (j * GROUP // 2 + pp) * LANES
            o_even, o_odd = os_[2 * pp], os_[2 * pp + 1]
            num = jnp.where(low, o_even, o_odd)
            den = jnp.where(low, pltpu.roll(o_even, HEAD_DIM, axis=1), pltpu.roll(o_odd, HEAD_DIM, axis=1))
            att_ref[0, :, c0:c0 + LANES] = (num / den).astype(att_ref.dtype)


def _dsa_call(q, qi, wi, hist, new, rel_bias, Tq):
    B, T, _ = q.shape
    P = 0 if hist is None else hist[0].shape[-1]
    Tn = new[0].shape[2 if P > 0 else 1]
    L = P + Tn
    Lp = -(-L // LANES) * LANES
    topk = min(MAX_TOPK, L // 4)
    assert T % Tq == 0 and Tn == T and P % LANES == 0 and (Tq % LANES == 0 or T == Tq)
    assert (P == 0 and Lp == L) or (P > 0 and Tq % LANES != 0)
    qspec = lambda w: pl.BlockSpec((1, Tq, w), lambda b, i: (b, i, 0))
    full = lambda *dims: pl.BlockSpec((1,) + dims, lambda b, i: (b,) + (0,) * len(dims))
    if Tq % LANES == 0:
        wi = jnp.swapaxes(wi, 1, 2)
        wi_spec = pl.BlockSpec((1, N_IDX_HEADS, Tq), lambda b, i: (b, 0, i))
    else:
        wi_spec = qspec(N_IDX_HEADS)
    in_specs = [qspec(ATT_WIDTH), qspec(N_IDX_HEADS * IDX_DIM), wi_spec]
    args = [q, qi, wi]
    if P > 0:
        in_specs += [full(N_KV_HEADS, HEAD_DIM, P), full(N_KV_HEADS, HEAD_DIM, P), full(IDX_DIM, P),
                     full(KV_WIDTH, Tn), full(KV_WIDTH, Tn), full(IDX_DIM, Tn)]
        args += list(hist)
        kv_scratch, ki_scratch = (N_KV_HEADS, LANES, Lp), (LANES, Lp)
    else:
        in_specs += [full(Tn, KV_WIDTH), full(Tn, KV_WIDTH), full(Tn, IDX_DIM)]
        kv_scratch, ki_scratch = (N_KV_HEADS, Lp, LANES), (Lp, LANES)
    in_specs.append(pl.BlockSpec(memory_space=pltpu.SMEM))
    args += list(new) + [rel_bias.astype(F32)]
    n_cls = Lp // 256 if (P == 0 and Lp % 256 == 0) else 1
    kern = functools.partial(_dsa_kernel, Tq=Tq, P=P, Tn=Tn, Lp=Lp, topk=topk, n_cls=n_cls)
    return pl.pallas_call(
        kern,
        grid=(B, T // Tq),
        in_specs=in_specs,
        out_specs=pl.BlockSpec((1, Tq, ATT_WIDTH), lambda b, i: (b, i, 0)),
        out_shape=jax.ShapeDtypeStruct((B, T, ATT_WIDTH), BF16),
        scratch_shapes=[pltpu.VMEM(kv_scratch, BF16),
                        pltpu.VMEM((2 * N_KV_HEADS,) + kv_scratch[1:], BF16),
                        pltpu.VMEM(ki_scratch, BF16),
                        pltpu.VMEM((N_KV_HEADS, GROUP * Tq, Lp), F32),
                        pltpu.VMEM((N_HEADS, Tq, 3 * LANES), F32),
                        pltpu.VMEM((Tq, Lp), F32)],
        compiler_params=pltpu.CompilerParams(dimension_semantics=("arbitrary", "arbitrary"),
                                             vmem_limit_bytes=VMEM_LIMIT),
        name="dsa_attention",
    )(*args)


ATT_COLS = 2 * ATT_WIDTH + 2 * KV_WIDTH + LANES
_C_Q, _C_QI, _C_K, _C_V, _C_KI = 0, 512, 1024, 1152, 1280
_C_WI = _C_KI + IDX_DIM


def _rms(x, g):
    return x * lax.rsqrt(jnp.mean(x * x, axis=-1, keepdims=True) + NORM_EPS) * g


def _inproj_kernel(x_ref, g_ref, wa_ref, wr_ref, q_ref, qi_ref, k_ref, v_ref, ki_ref, wi_ref, prw_ref,
                   *t_refs):
    h = _rms(x_ref[...], g_ref[...]).astype(BF16)
    pa = jnp.dot(h, wa_ref[...], preferred_element_type=F32)
    q_ref[...] = (pa[:, _C_Q:_C_Q + ATT_WIDTH] * (HEAD_DIM ** -0.5)).astype(BF16)
    qi_ref[...] = (pa[:, _C_QI:_C_QI + ATT_WIDTH] * (IDX_DIM ** -0.5)).astype(BF16)
    k_ref[...] = pa[:, _C_K:_C_K + KV_WIDTH]
    v_ref[...] = pa[:, _C_V:_C_V + KV_WIDTH]
    ki_ref[...] = pa[:, _C_KI:_C_KI + IDX_DIM]
    wi_ref[...] = pa[:, _C_WI:_C_WI + N_IDX_HEADS] * (N_IDX_HEADS ** -0.5)
    prw_ref[...] = jnp.dot(h, wr_ref[...], preferred_element_type=F32)
    if t_refs:
        kT_ref, vT_ref, kiT_ref = t_refs
        kT_ref[0] = pa[:, _C_K:_C_K + KV_WIDTH].T
        vT_ref[0] = pa[:, _C_V:_C_V + KV_WIDTH].T
        kiT_ref[0] = pa[:, _C_KI:_C_KI + LANES].T[0:IDX_DIM]


def _const_spec(shape):
    return pl.BlockSpec(shape, lambda i: (0,) * len(shape), pipeline_mode=pl.Buffered(1))


def _inproj_call(x2d, g, wa, wr, tm, t_per_batch=None):
    N = x2d.shape[0]
    assert N % tm == 0
    rows = lambda w: pl.BlockSpec((tm, w), lambda i: (i, 0))
    widths = [ATT_WIDTH, ATT_WIDTH, KV_WIDTH, KV_WIDTH, IDX_DIM, N_IDX_HEADS, RW_COLS]
    dtypes = [BF16, BF16, F32, F32, F32, F32, F32]
    out_specs = [rows(w) for w in widths]
    out_shape = [jax.ShapeDtypeStruct((N, w), d) for w, d in zip(widths, dtypes)]
    if t_per_batch is not None:
        assert t_per_batch % tm == 0
        tpb = t_per_batch // tm
        for feat in (KV_WIDTH, KV_WIDTH, IDX_DIM):
            out_specs.append(pl.BlockSpec((1, feat, tm), lambda i: (i // tpb, 0, i % tpb)))
            out_shape.append(jax.ShapeDtypeStruct((N // t_per_batch, feat, t_per_batch), F32))
    return pl.pallas_call(
        _inproj_kernel,
        grid=(N // tm,),
        in_specs=[rows(D_MODEL), _const_spec((1, D_MODEL)), _const_spec((D_MODEL, ATT_COLS)),
                  _const_spec((D_MODEL, RW_COLS))],
        out_specs=out_specs,
        out_shape=out_shape,
        compiler_params=pltpu.CompilerParams(dimension_semantics=("arbitrary",),
                                             vmem_limit_bytes=VMEM_LIMIT),
        name="rmsnorm_inproj",
    )(x2d, g, wa, wr)


def _tail_kernel(x_ref, att_ref, rw_ref, g1_ref, wgate_ref, wao_ref, wro_ref, wout_ref,
                 g2_ref, wff1_ref, wff2_ref, gf_ref, y_ref):
    x = x_ref[...]
    h = _rms(x, g1_ref[...]).astype(BF16)
    gates = jax.nn.sigmoid(jnp.dot(h, wgate_ref[...], preferred_element_type=F32))
    z_a = jnp.dot(att_ref[...], wao_ref[...], preferred_element_type=F32)
    z_b = jnp.dot(rw_ref[...], wro_ref[...], preferred_element_type=F32)
    mixed = gates[:, 0:D_MODEL] * z_a + gates[:, D_MODEL:2 * D_MODEL] * z_b
    x1 = x + jnp.dot(mixed.astype(BF16), wout_ref[...], preferred_element_type=F32)
    h2 = _rms(x1, g2_ref[...]).astype(BF16)
    f = jnp.maximum(jnp.dot(h2, wff1_ref[...], preferred_element_type=F32), 0.0)
    x2 = x1 + jnp.dot((f * f).astype(BF16), wff2_ref[...], preferred_element_type=F32)
    y_ref[...] = _rms(x2, gf_ref[...])


def _tail_call(x2d, att, rw, g1, wgate, wao, wro, wout, g2, wff1, wff2, gf, tm):
    N = x2d.shape[0]
    assert N % tm == 0
    rows = lambda w: pl.BlockSpec((tm, w), lambda i: (i, 0))
    return pl.pallas_call(
        _tail_kernel,
        grid=(N // tm,),
        in_specs=[rows(D_MODEL), rows(ATT_WIDTH), rows(RW_WIDTH), _const_spec((1, D_MODEL)),
                  _const_spec((D_MODEL, 2 * D_MODEL)), _const_spec((ATT_WIDTH, D_MODEL)),
                  _const_spec((RW_WIDTH, D_MODEL)), _const_spec((D_MODEL, D_MODEL)),
                  _const_spec((1, D_MODEL)), _const_spec((D_MODEL, D_FF)), _const_spec((D_FF, D_MODEL)),
                  _const_spec((1, D_MODEL))],
        out_specs=rows(D_MODEL),
        out_shape=jax.ShapeDtypeStruct((N, D_MODEL), F32),
        compiler_params=pltpu.CompilerParams(dimension_semantics=("arbitrary",),
                                             vmem_limit_bytes=VMEM_LIMIT),
        name="merge_out_mlp",
    )(x2d, att, rw, g1, wgate, wao, wro, wout, g2, wff1, wff2, gf)


_IN_SPLITS = (ATT_WIDTH, KV_WIDTH, KV_WIDTH, N_IDX_HEADS * IDX_DIM, IDX_DIM, N_IDX_HEADS,
              RW_COLS, D_MODEL, D_MODEL)


def _split_w_in(w_in):
    offs = [0]
    for s in _IN_SPLITS:
        offs.append(offs[-1] + s)
    col = lambda n: w_in[:, offs[n]:offs[n + 1]]
    pad = jnp.zeros((D_MODEL, ATT_COLS - _C_WI - N_IDX_HEADS), w_in.dtype)
    wa = jnp.concatenate([col(0), col(3), col(1), col(2), col(4), col(5), pad], axis=1).astype(BF16)
    wr = col(6).astype(BF16)
    wgate = jnp.concatenate([col(7), col(8)], axis=1).astype(BF16)
    return wa, wr, wgate


def _branch(x, hist, wkv0, shift0, rel_bias, prm, w, tm_in, tm_tail, Tq, C, nb_rw):
    B, T, _ = x.shape
    x2d = x.reshape(B * T, D_MODEL)
    r3 = lambda a: a.reshape(B, T, a.shape[-1])
    heads_last = lambda aT: jnp.transpose(aT.reshape(B, N_KV_HEADS, HEAD_DIM, T), (0, 3, 1, 2))[None]
    if hist is None:
        q, qi, k, v, ki, wi, p_rw, kT, vT, kiT = _inproj_call(x2d, prm['g1'], w['wa'], w['wr'], tm_in, T)
        new = (r3(k), r3(v), r3(ki))
        new_k, new_v, new_ki = heads_last(kT), heads_last(vT), jnp.swapaxes(kiT, 1, 2)[None]
    else:
        q, qi, k, v, ki, wi, p_rw = _inproj_call(x2d, prm['g1'], w['wa'], w['wr'], tm_in)
        new = tuple(jnp.swapaxes(r3(a), 1, 2) for a in (k, v, ki))
        new_k = k.reshape(1, B, T, N_KV_HEADS, HEAD_DIM)
        new_v = v.reshape(1, B, T, N_KV_HEADS, HEAD_DIM)
        new_ki = ki.reshape(1, B, T, IDX_DIM)
    rw, hT = _rwkv_call(r3(p_rw), shift0, _state_to_bd(wkv0), prm, C, nb_rw)
    att = _dsa_call(r3(q), r3(qi), r3(wi), hist, new, rel_bias, Tq)
    y = _tail_call(x2d, att.reshape(B * T, ATT_WIDTH), rw.reshape(B * T, RW_WIDTH), prm['g1'],
                   w['wgate'], w['wao'], w['wro'], w['wout'], prm['g2'], w['wff1'], w['wff2'],
                   prm['gf'], tm_tail)
    wkv = _bd_to_state(hT)[None]
    shift = r3(p_rw)[:, T - 1:T, :][None]
    return y.reshape(B, T, D_MODEL), new_k, new_v, new_ki, wkv, shift


def kernel(x_prompt, x_sample, cache_k, cache_v, cache_kidx, state_wkv, state_shift, rel_bias,
           norm1_g, w_in, w_att_out, mu_shift, w0, w_decay_up, a0, w_a_up, w_g_up, k_k, k_a, r_k,
           lnx_w, lnx_b, w_rw_out, w_out, norm2_g, w_ff1, w_ff2, norm_f_g):
    assert w_in.shape[0] == 1, "single-layer step"
    wa, wr, wgate = _split_w_in(w_in[0])
    w = dict(wa=wa, wr=wr, wgate=wgate, wao=w_att_out[0].astype(BF16), wro=w_rw_out[0].astype(BF16),
             wout=w_out[0].astype(BF16), wff1=w_ff1[0].astype(BF16), wff2=w_ff2[0].astype(BF16))
    prm = dict(mu_shift=mu_shift[0], w0=w0[0], w_decay_up=w_decay_up[0], a0=a0[0], w_a_up=w_a_up[0],
               w_g_up=w_g_up[0], k_k=k_k[0], k_a=k_a[0], r_k=r_k[0].reshape(-1), lnx_w=lnx_w[0],
               lnx_b=lnx_b[0], g1=norm1_g[0].reshape(1, -1), g2=norm2_g[0].reshape(1, -1),
               gf=norm_f_g.reshape(1, -1))
    Bp, Tp, _ = x_prompt.shape
    Bs, Ts, _ = x_sample.shape
    past = cache_k.shape[2]
    zero_state = jnp.zeros((Bp, RW_HEADS, RW_HEAD, RW_HEAD), F32)
    zero_shift = jnp.zeros((Bp, 1, RW_COLS), F32)
    outs_p = _branch(x_prompt, None, zero_state, zero_shift, rel_bias, prm, w,
                     tm_in=512, tm_tail=256, Tq=Q_BLOCK, C=CHUNK, nb_rw=4)
    hist = (jnp.transpose(cache_k[0], (0, 2, 3, 1)), jnp.transpose(cache_v[0], (0, 2, 3, 1)),
            jnp.swapaxes(cache_kidx[0], 1, 2))
    outs_s = _branch(x_sample, hist, state_wkv[0], state_shift[0], rel_bias, prm, w,
                     tm_in=Bs * Ts, tm_tail=256, Tq=Ts, C=Ts, nb_rw=4)
    return (outs_p[0], outs_s[0]) + outs_p[1:] + outs_s[1:]
```

```python
import functools
import math

import jax
import jax.numpy as jnp
import numpy as np
from jax import lax
from jax.experimental import pallas as pl
from jax.experimental.pallas import tpu as pltpu

F32 = jnp.float32
BF16 = jnp.bfloat16

D_MODEL = 1024
CHUNK = 64
N_HEADS = 8
N_KV_HEADS = 2
HEAD_DIM = 64
GROUP = N_HEADS // N_KV_HEADS
ATT_WIDTH = N_HEADS * HEAD_DIM
KV_WIDTH = N_KV_HEADS * HEAD_DIM
N_IDX_HEADS = 8
IDX_DIM = 64
MAX_TOPK = 256
Q_BLOCK = 128
N_BUCKETS = 32
RW_HEADS = 8
RW_HEAD = 64
RW_WIDTH = RW_HEADS * RW_HEAD
DECAY_LORA = 64
AAA_LORA = 64
GATE_LORA = 128
RW_COLS = 3 * RW_WIDTH + DECAY_LORA + AAA_LORA + GATE_LORA
GN_EPS = 64e-5
D_FF = 4 * D_MODEL
NORM_EPS = 1e-6
NEG = -1e30

LANES = 128
N_PAIRS = RW_HEADS // 2
VMEM_LIMIT = 56 * 1024 * 1024

CHUNK_PIECES = 1
SEG_PIECES = 2

_NN, _NT, _TN = (1, 0), (1, 1), (0, 0)


def _split(x, n):
    pieces = []
    rem = x
    for i in range(n):
        p = rem.astype(BF16)
        pieces.append(p)
        if i + 1 < n:
            rem = rem - p.astype(F32)
    return tuple(pieces)


def _mm(a, b, dims=_NN):
    n = max(len(a), len(b))
    terms = [(i, j) for i in range(len(a)) for j in range(len(b)) if i + j < n]
    ca, cb = dims
    lhs = jnp.concatenate([a[i] for i, _ in terms], axis=ca) if len(terms) > 1 else a[0]
    rhs = jnp.concatenate([b[j] for _, j in terms], axis=cb) if len(terms) > 1 else b[0]
    return lax.dot_general(lhs, rhs, (((ca,), (cb,)), ((), ())), preferred_element_type=F32)


def _lane_iota(shape):
    return lax.broadcasted_iota(jnp.int32, shape, len(shape) - 1)


def _row_iota(shape):
    return lax.broadcasted_iota(jnp.int32, shape, len(shape) - 2)


def _bdrows(x, width, heads=2):
    head = (_lane_iota(x.shape) % (heads * width)) // width
    return jnp.concatenate([jnp.where(head == h, x, 0.0) for h in range(heads)], axis=0)


def _rwkv_kernel(x_ref, shift0_ref, h0_ref, mu_ref, w0_ref, a0_ref, kk_ref, ka_ref, rk_ref,
                 lnw_ref, lnb_ref, wlora_ref, wg_ref,
                 out_ref, hT_ref, xbuf, hst, *, C, n_sq, nb):
    c = pl.program_id(1)

    @pl.when(c == 0)
    def _():
        for n in range(nb):
            xbuf[n, 7:8, :] = shift0_ref[n]
        z = jnp.zeros((RW_HEAD, RW_HEAD), F32)
        for n in range(nb):
            for p in range(N_PAIRS):
                blockdiag = jnp.concatenate(
                    [jnp.concatenate([h0_ref[n, 2 * p], z], axis=1),
                     jnp.concatenate([z, h0_ref[n, 2 * p + 1]], axis=1)], axis=0)
                hst[n, p] = blockdiag.T

    W = RW_WIDTH
    R = nb * C
    prevs = []
    for n in range(nb):
        xn = x_ref[n]
        xbuf[n, 8:8 + C, :] = xn
        prevs.append(xbuf[n, 7:7 + C, :])
        xbuf[n, 7:8, :] = xn[C - 1:C, :]
    x = x_ref[...].reshape(R, RW_COLS)
    prev = jnp.concatenate(prevs, axis=0)
    xs = x + (prev - x) * mu_ref[...]
    r = xs[:, 0:W]
    k = xs[:, W:2 * W]
    v = xs[:, 2 * W:3 * W]
    da = xs[:, 3 * W:3 * W + LANES]
    gd = xs[:, 3 * W + LANES:3 * W + 2 * LANES]

    t_in = jnp.where(_lane_iota(da.shape) < DECAY_LORA, jnp.tanh(da), da)
    lo = jnp.dot(t_in.astype(BF16), wlora_ref[...], preferred_element_type=F32)
    ld = -math.exp(-0.5) * jax.nn.sigmoid(w0_ref[...] + lo[:, 0:W])
    a = jax.nn.sigmoid(a0_ref[...] + lo[:, W:2 * W])
    g = jnp.dot(jax.nn.sigmoid(gd).astype(BF16), wg_ref[...], preferred_element_type=F32)

    seg = (_row_iota((LANES, LANES)) // RW_HEAD == _lane_iota((LANES, LANES)) // RW_HEAD).astype(BF16)

    def segsum(t):
        return jnp.concatenate(
            [_mm(_split(t[:, p * LANES:(p + 1) * LANES], SEG_PIECES), (seg,)) for p in range(N_PAIRS)],
            axis=1)

    kk0 = k * kk_ref[...]
    kk = kk0 * jnp.minimum(lax.rsqrt(segsum(kk0 * kk0)), 1e12)
    k2 = k * (1.0 + (a - 1.0) * ka_ref[...])
    bb = kk * a

    ti, si = _row_iota((R, R)), _lane_iota((R, R))
    tri = jnp.logical_and(si <= ti, si // C == ti // C).astype(BF16)
    lp = _mm((tri,), _split(ld, 3))
    lpp = lp - ld
    rows_of = lambda i: jnp.concatenate(
        [jnp.broadcast_to(lp[n * C + i:n * C + i + 1, :], (C, W)) for n in range(nb)], axis=0)
    lpc = rows_of(C - 1)
    mid = rows_of(C // 2 - 1)
    r_abs = r * jnp.exp(lp)
    kap_abs = kk * jnp.exp(lpp)
    r_mid = r * jnp.exp(lp - mid)
    kap_mid = kk * jnp.exp(lpp - mid)
    e_mid = jnp.exp(mid - lp)
    b_mid = bb * e_mid
    k_mid = k2 * e_mid
    e_end = jnp.exp(lpc - lp)
    b_end = bb * e_end
    k_end = k2 * e_end
    p_end = jnp.exp(lpc)

    tt = _row_iota((C, 2 * C))
    ss = _lane_iota((C, 2 * C)) % C
    strict = ss < tt
    incl = ss <= tt
    eye_p = (ss == tt).astype(F32)
    jrow = _row_iota((LANES, LANES))
    jlane = _lane_iota((LANES, LANES))
    same_head = (jrow // RW_HEAD) == (jlane // RW_HEAD)
    eye_j = jrow == jlane

    sp = lambda t: _split(t, CHUNK_PIECES)
    bd = lambda pieces, half: tuple(_bdrows(t, half) for t in pieces)

    chains = [(n, p) for n in range(nb) for p in range(N_PAIRS)]
    blk = lambda t, ch: t[ch[0] * C:(ch[0] + 1) * C, ch[1] * LANES:(ch[1] + 1) * LANES]
    each = lambda f: [f(i, ch) for i, ch in enumerate(chains)]

    A = each(lambda i, ch: _mm(
        sp(jnp.concatenate([blk(kap_mid, ch), blk(r_mid, ch)], axis=0)),
        sp(jnp.concatenate([_bdrows(blk(b_mid, ch), RW_HEAD), _bdrows(blk(k_mid, ch), RW_HEAD)],
                           axis=0)), _NT))
    n_kb = each(lambda i, ch: jnp.where(strict, A[i][0:C, 0:2 * C], 0.0))
    a_kk = each(lambda i, ch: jnp.where(strict, A[i][0:C, 2 * C:4 * C], 0.0))
    a_rb = each(lambda i, ch: jnp.where(incl, A[i][C:2 * C, 0:2 * C], 0.0))
    a_rk = each(lambda i, ch: jnp.where(incl, A[i][C:2 * C, 2 * C:4 * C], 0.0))
    T = each(lambda i, ch: eye_p - n_kb[i])
    pw = each(lambda i, ch: sp(n_kb[i]))
    for _ in range(n_sq):
        pw = each(lambda i, ch: sp(_mm(pw[i], bd(pw[i], C))))
        T = each(lambda i, ch: T[i] + _mm(sp(T[i]), bd(pw[i], C)))
    v_p = each(lambda i, ch: sp(blk(v, ch)))
    av = each(lambda i, ch: _mm(sp(jnp.concatenate([a_kk[i], a_rk[i]], axis=0)),
                                bd(v_p[i], RW_HEAD)))
    wu = each(lambda i, ch: sp(_mm(
        sp(T[i]), bd(sp(jnp.concatenate([blk(kap_abs, ch), av[i][0:C]], axis=1)), RW_HEAD))))
    qy = each(lambda i, ch: jnp.concatenate([blk(r_abs, ch), av[i][C:2 * C]], axis=1)
              - _mm(sp(a_rb[i]), bd(wu[i], RW_HEAD)))
    bw = each(lambda i, ch: _mm(sp(blk(b_end, ch)), wu[i], _TN))
    kv = each(lambda i, ch: _mm(_split(blk(k_end, ch), 2), _split(blk(v, ch), 2), _TN))
    G = each(lambda i, ch: jnp.where(same_head, kv[i] - bw[i][:, LANES:2 * LANES], 0.0))
    M = each(lambda i, ch: jnp.where(same_head, bw[i][:, 0:LANES], 0.0))
    p_col = each(lambda i, ch: jnp.sum(jnp.where(eye_j, blk(p_end, ch)[0:1], 0.0), axis=1,
                                       keepdims=True))
    h0f = each(lambda i, ch: hst[ch[0], ch[1]])
    h0 = each(lambda i, ch: sp(h0f[i]))
    yc_ = each(lambda i, ch: _mm(sp(qy[i][:, 0:LANES]), h0[i]) + qy[i][:, LANES:2 * LANES])
    for i, ch in enumerate(chains):
        hst[ch[0], ch[1]] = p_col[i] * h0f[i] - _mm(sp(M[i]), h0[i]) + G[i]
    y = jnp.concatenate([jnp.concatenate(yc_[n * N_PAIRS:(n + 1) * N_PAIRS], axis=1)
                         for n in range(nb)], axis=0)

    mean = segsum(y) * (1.0 / RW_HEAD)
    yc = y - mean
    var = segsum(yc * yc) * (1.0 / RW_HEAD)
    yn = yc * lax.rsqrt(var + GN_EPS) * lnw_ref[...] + lnb_ref[...]
    bonus = segsum(r * k2 * rk_ref[...]) * v
    out_ref[...] = ((yn + bonus) * g).astype(out_ref.dtype).reshape(nb, C, W)

    @pl.when(c == pl.num_programs(1) - 1)
    def _():
        for n in range(nb):
            for p in range(N_PAIRS):
                blockdiag = hst[n, p].T
                hT_ref[n, 2 * p] = blockdiag[0:RW_HEAD, 0:RW_HEAD]
                hT_ref[n, 2 * p + 1] = blockdiag[RW_HEAD:LANES, RW_HEAD:LANES]


def _rwkv_call(p_rw, shift0, wkv0, prm, C, nb):
    B, T, _ = p_rw.shape
    n_sq = int(math.log2(C)) - 1
    assert 2 ** (n_sq + 1) == C and T % C == 0 and B % nb == 0
    row = lambda a: a.reshape(1, -1).astype(F32)
    vecs = [row(prm['mu_shift']), row(prm['w0']), row(prm['a0']), row(prm['k_k']), row(prm['k_a']),
            row(prm['r_k']), row(prm['lnx_w']), row(prm['lnx_b'])]
    wlora = jnp.zeros((LANES, 2 * RW_WIDTH), F32)
    wlora = wlora.at[0:DECAY_LORA, 0:RW_WIDTH].set(prm['w_decay_up'])
    wlora = wlora.at[DECAY_LORA:, RW_WIDTH:].set(prm['w_a_up']).astype(BF16)
    wg = prm['w_g_up'].astype(BF16)
    const = lambda shape: pl.BlockSpec(shape, lambda b, c: (0,) * len(shape))
    kern = functools.partial(_rwkv_kernel, C=C, n_sq=n_sq, nb=nb)
    return pl.pallas_call(
        kern,
        grid=(B // nb, T // C),
        in_specs=[pl.BlockSpec((nb, C, RW_COLS), lambda b, c: (b, c, 0)),
                  pl.BlockSpec((nb, 1, RW_COLS), lambda b, c: (b, 0, 0)),
                  pl.BlockSpec((nb, RW_HEADS, RW_HEAD, RW_HEAD), lambda b, c: (b, 0, 0, 0)),
                  const((1, RW_COLS))] + [const((1, RW_WIDTH))] * 7
                 + [const((LANES, 2 * RW_WIDTH)), const((GATE_LORA, RW_WIDTH))],
        out_specs=[pl.BlockSpec((nb, C, RW_WIDTH), lambda b, c: (b, c, 0)),
                   pl.BlockSpec((nb, RW_HEADS, RW_HEAD, RW_HEAD), lambda b, c: (b, 0, 0, 0))],
        out_shape=[jax.ShapeDtypeStruct((B, T, RW_WIDTH), BF16),
                   jax.ShapeDtypeStruct((B, RW_HEADS, RW_HEAD, RW_HEAD), F32)],
        scratch_shapes=[pltpu.VMEM((nb, 8 + C, RW_COLS), F32),
                        pltpu.VMEM((nb, N_PAIRS, LANES, LANES), F32)],
        compiler_params=pltpu.CompilerParams(dimension_semantics=("arbitrary", "arbitrary"),
                                             vmem_limit_bytes=VMEM_LIMIT),
        name="rwkv7_chunked",
    )(p_rw, shift0, wkv0, *vecs, wlora, wg)


_INT_MIN = -2 ** 31
_NEG_KEY = int(np.float32(NEG).view(np.int32)) ^ 0x7FFFFFFF
_T5_STEPS = ((91, 14), (64, 13), (46, 12), (32, 11), (23, 10), (16, 9), (12, 8), (8, 7), (7, 6),
             (6, 5), (5, 4), (4, 3), (3, 2), (2, 1), (1, 0))
_T5_FAR = 15
_KEY_TILE = 256


def _half_masked(x, half):
    low = (_lane_iota(x.shape) % LANES) < HEAD_DIM
    keep = low if half == 0 else jnp.logical_not(low)
    return jnp.where(keep, x, jnp.zeros_like(x))


def _dup_half(x, half):
    xr = pltpu.roll(x, HEAD_DIM, axis=1)
    low = _lane_iota(x.shape) < HEAD_DIM
    return jnp.where(low, x, xr) if half == 0 else jnp.where(low, xr, x)


def _dot_nt(a, b):
    return lax.dot_general(a, b, (((1,), (1,)), ((), ())), preferred_element_type=F32)


def _dsa_kernel(*refs, Tq, P, Tn, Lp, topk, n_cls):
    if P > 0:
        (q_ref, qi_ref, wi_ref, hk_ref, hv_ref, hki_ref, nk_ref, nv_ref, nki_ref, rb_ref,
         att_ref, kd, vd, kid, lg, tbl, sbias) = refs
    else:
        (q_ref, qi_ref, wi_ref, nk_ref, nv_ref, nki_ref, rb_ref,
         att_ref, kd, vd, kid, lg, tbl, sbias) = refs
    b = pl.program_id(0)
    i = pl.program_id(1)
    L = P + Tn

    @pl.when(jnp.logical_and(b == 0, i == 0))
    def _():
        shape = (Tq, 2 * LANES)
        rel = _lane_iota(shape) - LANES - _row_iota(shape)
        n = jnp.abs(rel)
        for h in range(N_HEADS):
            far = rb_ref[_T5_FAR, h]
            neg = jnp.full(shape, far, F32)
            pos = jnp.full(shape, rb_ref[_T5_FAR + N_BUCKETS // 2, h], F32)
            for bound, bucket in _T5_STEPS:
                neg = jnp.where(n < bound, rb_ref[bucket, h], neg)
                pos = jnp.where(n < bound, rb_ref[bucket + N_BUCKETS // 2, h], pos)
            tbl[h, :, 0:2 * LANES] = jnp.where(rel > 0, pos, neg) - far
            tbl[h, :, 2 * LANES:3 * LANES] = jnp.zeros((Tq, LANES), F32)

    @pl.when(i == 0)
    def _():
        if P > 0:
            def put(dst, x, c0):
                xb = x.astype(BF16)
                dst[0:HEAD_DIM, c0:c0 + x.shape[1]] = xb
                dst[HEAD_DIM:LANES, c0:c0 + x.shape[1]] = xb

            zt = jnp.zeros((LANES, Lp - P), BF16)
            for j in range(N_KV_HEADS):
                put(kd.at[j], hk_ref[0, j], 0)
                kd[j, :, P:Lp] = zt
                put(kd.at[j], nk_ref[0, j * HEAD_DIM:(j + 1) * HEAD_DIM, :], P)
                for u in range(2):
                    vd[2 * j + u] = jnp.ones((LANES, Lp), BF16)
                    r0 = u * HEAD_DIM
                    vd[2 * j + u, r0:r0 + HEAD_DIM, 0:P] = hv_ref[0, j].astype(BF16)
                    vd[2 * j + u, r0:r0 + HEAD_DIM, P:P + Tn] = (
                        nv_ref[0, j * HEAD_DIM:(j + 1) * HEAD_DIM, :].astype(BF16))
            put(kid, hki_ref[0], 0)
            kid[:, P:Lp] = zt
            put(kid, nki_ref[0], P)
        else:
            kf, vf, kif = nk_ref[0], nv_ref[0], nki_ref[0]
            low = _lane_iota(vf.shape) < HEAD_DIM
            for j in range(N_KV_HEADS):
                kd[j, 0:Tn, :] = _dup_half(kf, j).astype(BF16)
                vdup = _dup_half(vf, j)
                vd[2 * j, 0:Tn, :] = jnp.where(low, vdup, 1.0).astype(BF16)
                vd[2 * j + 1, 0:Tn, :] = jnp.where(low, 1.0, vdup).astype(BF16)
            kid[0:Tn, :] = jnp.concatenate([kif, kif], axis=1).astype(BF16)

    stride = Lp // n_cls
    k_hi = P + (i + 1) * Tq
    cls = (k_hi - 1) // stride
    for c in range(n_cls):
        pl.when(cls == c)(functools.partial(
            _dsa_block, q_ref, qi_ref, wi_ref, att_ref, kd, vd, kid, lg, tbl, sbias,
            i=i, Tq=Tq, P=P, L=L, Lk=(c + 1) * stride, topk=topk))


def _dsa_block(q_ref, qi_ref, wi_ref, att_ref, kd, vd, kid, lg, tbl, sbias, *, i, Tq, P, L, Lk, topk):
    q = q_ref[0]
    qi = qi_ref[0]
    key_major = Tq % LANES == 0
    kax = 0 if key_major else 1
    shape = (Lk, Tq) if key_major else (Tq, Lk)
    qvec = (1, Tq) if key_major else (Tq, 1)
    wi = wi_ref[0]

    lanes_keys = P > 0
    kidv = kid[:, 0:Lk] if lanes_keys else kid[0:Lk, :]
    head_q = lambda h: _half_masked(qi[:, (h // 2) * LANES:(h // 2 + 1) * LANES], h % 2)
    score = jnp.zeros(shape, F32)
    if key_major:
        pair_q = [jnp.concatenate([head_q(2 * pr), head_q(2 * pr + 1)], axis=0)
                  for pr in range(N_IDX_HEADS // 2)]
        tiles = []
        tile = _KEY_TILE if Lk % _KEY_TILE == 0 else LANES
        for kt in range(Lk // tile):
            kid_t = kidv[kt * tile:(kt + 1) * tile]
            acc = jnp.zeros((tile, Tq), F32)
            for pr in range(N_IDX_HEADS // 2):
                s2 = _dot_nt(kid_t, pair_q[pr])
                for u in range(2):
                    h = 2 * pr + u
                    acc = acc + wi[h:h + 1, :] * jnp.maximum(s2[:, u * Tq:(u + 1) * Tq], 0.0)
            tiles.append(acc)
        score = jnp.concatenate(tiles, axis=0)
    else:
        lhs_all = jnp.concatenate([head_q(h) for h in range(N_IDX_HEADS)], axis=0)
        s_all = (jnp.dot(lhs_all, kidv, preferred_element_type=F32) if lanes_keys
                 else _dot_nt(lhs_all, kidv))
        for h in range(N_IDX_HEADS):
            score = score + wi[:, h:h + 1] * jnp.maximum(s_all[h * Tq:(h + 1) * Tq], 0.0)

    s_pos = lax.broadcasted_iota(jnp.int32, shape, kax)
    q_pos = P + i * Tq + lax.broadcasted_iota(jnp.int32, shape, 1 - kax)
    adm = jnp.logical_and(s_pos // CHUNK <= q_pos // CHUNK, s_pos < L)
    score = jnp.where(adm, score, NEG)

    def count(mask):
        ones = jnp.where(mask, 1.0, 0.0)
        if key_major:
            ones = jnp.sum(ones.reshape(Lk // LANES, LANES, Tq), axis=0)
        return jnp.sum(ones, axis=kax, keepdims=True)

    def key_to_float(key):
        return pltpu.bitcast(jnp.where(key < 0, key ^ jnp.int32(0x7FFFFFFF), key), F32)

    kf = float(topk)
    ans = jnp.where(count(score >= 0.0) >= kf, jnp.int32(0), jnp.int32(_INT_MIN))

    def bit_step(it, ans):
        cand = ans | (jnp.int32(1) << (30 - it))
        return jnp.where(count(score >= key_to_float(cand)) >= kf, cand, ans)

    def two_bit_step(it, ans):
        shift = 28 - 2 * it
        for m in (1, 2, 3):
            cand = ans | (jnp.int32(m) << shift)
            ans_m = jnp.where(count(score >= key_to_float(cand)) >= kf, cand, ans)
            best = ans_m if m == 1 else jnp.maximum(best, ans_m)
        return best

    def stacked_q(j):
        return jnp.concatenate(
            [_half_masked(q[:, ((j * GROUP + g) // 2) * LANES:((j * GROUP + g) // 2 + 1) * LANES], g % 2)
             for g in range(GROUP)], axis=0)

    def logits_rows(j, lhs_rows, rsl):
        lg[j, rsl, 0:Lk] = (jnp.dot(lhs_rows, kd[j, :, 0:Lk], preferred_element_type=F32) if lanes_keys
                            else _dot_nt(lhs_rows, kd[j, 0:Lk, :]))

    if key_major:
        thr_key = lax.fori_loop(0, 31, bit_step, ans)
    else:
        thr_key = lax.fori_loop(0, 15, two_bit_step, bit_step(0, ans))
    thr_key = jnp.maximum(thr_key, jnp.int32(_NEG_KEY))
    thr = key_to_float(thr_key)
    gt = score > thr
    eq = score == thr
    need = kf - count(gt)
    tied = jnp.logical_and(count(eq) > need, thr > NEG)
    any_tied = jnp.max(jnp.where(tied, 1.0, 0.0)) > 0.0

    n_idx_bits = (Lk - 1).bit_length()

    def idx_search():
        def idx_step(it, x):
            cand = x | (jnp.int32(1) << (n_idx_bits - 1 - it))
            return jnp.where(count(jnp.logical_and(eq, s_pos < cand)) < need, cand, x)
        return lax.fori_loop(0, n_idx_bits, idx_step, jnp.zeros(qvec, jnp.int32))

    last = lax.cond(any_tied, idx_search, lambda: jnp.full(qvec, Lk, jnp.int32))
    sel = jnp.logical_and(jnp.logical_or(gt, jnp.logical_and(eq, s_pos <= last)), adm)
    sb = jnp.where(sel, 0.0, NEG)
    sbias[:, 0:Lk] = sb.T if key_major else sb

    blk_end = (P + i * Tq) // LANES + 1
    start = pl.multiple_of(jnp.maximum(blk_end - 2, 0) * LANES, LANES)
    t_off = pl.multiple_of(jnp.maximum(2 - blk_end, 0) * LANES, LANES)
    for j in range(N_KV_HEADS):
        logits_rows(j, stacked_q(j), slice(0, GROUP * Tq))
    for j in range(N_KV_HEADS):
        for g in range(GROUP):
            r = slice(g * Tq, (g + 1) * Tq)
            if Lk < 2 * LANES:
                lg[j, r, 0:LANES] = lg[j, r, 0:LANES] + tbl[j * GROUP + g, :, LANES:2 * LANES]
            else:
                lg[j, r, pl.ds(start, 2 * LANES)] = (lg[j, r, pl.ds(start, 2 * LANES)]
                                                     + tbl[j * GROUP + g, :, pl.ds(t_off, 2 * LANES)])
    for j in range(N_KV_HEADS):
        sb = sbias[:, 0:Lk]
        ps = []
        for g in range(GROUP):
            lgv = lg[j, g * Tq:(g + 1) * Tq, 0:Lk]
            m = jnp.max(lgv + sb, axis=1, keepdims=True)
            ps.append(jnp.exp((lgv - m) + sb).astype(BF16))
        pv = lambda p, u: (_dot_nt(p, vd[2 * j + u, :, 0:Lk]) if lanes_keys
                           else jnp.dot(p, vd[2 * j + u, 0:Lk, :], preferred_element_type=F32))
        if key_major:
            os_ = [pv(ps[g], g % 2) for g in range(GROUP)]
        else:
            stacked = [pv(jnp.concatenate(ps[u::2], axis=0), u) for u in range(2)]
            os_ = [stacked[g % 2][(g // 2) * Tq:(g // 2 + 1) * Tq] for g in range(GROUP)]
        low = _lane_iota((Tq, LANES)) < HEAD_DIM
        for pp in range(GROUP // 2):
            c0 = (j * GROUP // 2 + pp) * LANES
            o_even, o_odd = os_[2 * pp], os_[2 * pp + 1]
            num = jnp.where(low, o_even, o_odd)
            den = jnp.where(low, pltpu.roll(o_even, HEAD_DIM, axis=1), pltpu.roll(o_odd, HEAD_DIM, axis=1))
            att_ref[0, :, c0:c0 + LANES] = (num / den).astype(att_ref.dtype)


def _dsa_call(q, qi, wi, hist, new, rel_bias, Tq):
    B, T, _ = q.shape
    P = 0 if hist is None else hist[0].shape[-1]
    Tn = new[0].shape[2 if P > 0 else 1]
    L = P + Tn
    Lp = -(-L // LANES) * LANES
    topk = min(MAX_TOPK, L // 4)
    assert T % Tq == 0 and Tn == T and P % LANES == 0 and (Tq % LANES == 0 or T == Tq)
    assert (P == 0 and Lp == L) or (P > 0 and Tq % LANES != 0)
    qspec = lambda w: pl.BlockSpec((1, Tq, w), lambda b, i: (b, i, 0))
    full = lambda *dims: pl.BlockSpec((1,) + dims, lambda b, i: (b,) + (0,) * len(dims))
    if Tq % LANES == 0:
        wi = jnp.swapaxes(wi, 1, 2)
        wi_spec = pl.BlockSpec((1, N_IDX_HEADS, Tq), lambda b, i: (b, 0, i))
    else:
        wi_spec = qspec(N_IDX_HEADS)
    in_specs = [qspec(ATT_WIDTH), qspec(N_IDX_HEADS * IDX_DIM), wi_spec]
    args = [q, qi, wi]
    if P > 0:
        in_specs += [full(N_KV_HEADS, HEAD_DIM, P), full(N_KV_HEADS, HEAD_DIM, P), full(IDX_DIM, P),
                     full(KV_WIDTH, Tn), full(KV_WIDTH, Tn), full(IDX_DIM, Tn)]
        args += list(hist)
        kv_scratch, ki_scratch = (N_KV_HEADS, LANES, Lp), (LANES, Lp)
    else:
        in_specs += [full(Tn, KV_WIDTH), full(Tn, KV_WIDTH), full(Tn, IDX_DIM)]
        kv_scratch, ki_scratch = (N_KV_HEADS, Lp, LANES), (Lp, LANES)
    in_specs.append(pl.BlockSpec(memory_space=pltpu.SMEM))
    args += list(new) + [rel_bias.astype(F32)]
    n_cls = Lp // Tq if (P == 0 and Tq % LANES == 0 and Lp % Tq == 0) else 1
    kern = functools.partial(_dsa_kernel, Tq=Tq, P=P, Tn=Tn, Lp=Lp, topk=topk, n_cls=n_cls)
    return pl.pallas_call(
        kern,
        grid=(B, T // Tq),
        in_specs=in_specs,
        out_specs=pl.BlockSpec((1, Tq, ATT_WIDTH), lambda b, i: (b, i, 0)),
        out_shape=jax.ShapeDtypeStruct((B, T, ATT_WIDTH), BF16),
        scratch_shapes=[pltpu.VMEM(kv_scratch, BF16),
                        pltpu.VMEM((2 * N_KV_HEADS,) + kv_scratch[1:], BF16),
                        pltpu.VMEM(ki_scratch, BF16),
                        pltpu.VMEM((N_KV_HEADS, GROUP * Tq, Lp), F32),
                        pltpu.VMEM((N_HEADS, Tq, 3 * LANES), F32),
                        pltpu.VMEM((Tq, Lp), F32)],
        compiler_params=pltpu.CompilerParams(dimension_semantics=("arbitrary", "arbitrary"),
                                             vmem_limit_bytes=VMEM_LIMIT),
        name="dsa_attention",
    )(*args)


ATT_COLS = 2 * ATT_WIDTH + 2 * KV_WIDTH + LANES
_C_Q, _C_QI, _C_K, _C_V, _C_KI = 0, 512, 1024, 1152, 1280
_C_WI = _C_KI + IDX_DIM


def _rms(x, g):
    return x * lax.rsqrt(jnp.mean(x * x, axis=-1, keepdims=True) + NORM_EPS) * g


def _inproj_kernel(x_ref, g_ref, wa_ref, wr_ref, q_ref, qi_ref, k_ref, v_ref, ki_ref, wi_ref, prw_ref,
                   *t_refs):
    h = _rms(x_ref[...], g_ref[...]).astype(BF16)
    pa = jnp.dot(h, wa_ref[...], preferred_element_type=F32)
    q_ref[...] = (pa[:, _C_Q:_C_Q + ATT_WIDTH] * (HEAD_DIM ** -0.5)).astype(BF16)
    qi_ref[...] = (pa[:, _C_QI:_C_QI + ATT_WIDTH] * (IDX_DIM ** -0.5)).astype(BF16)
    k_ref[...] = pa[:, _C_K:_C_K + KV_WIDTH]
    v_ref[...] = pa[:, _C_V:_C_V + KV_WIDTH]
    ki_ref[...] = pa[:, _C_KI:_C_KI + IDX_DIM]
    wi_ref[...] = pa[:, _C_WI:_C_WI + N_IDX_HEADS] * (N_IDX_HEADS ** -0.5)
    prw_ref[...] = jnp.dot(h, wr_ref[...], preferred_element_type=F32)
    if t_refs:
        kT_ref, vT_ref, kiT_ref = t_refs
        kT_ref[0] = pa[:, _C_K:_C_K + KV_WIDTH].T
        vT_ref[0] = pa[:, _C_V:_C_V + KV_WIDTH].T
        kiT_ref[0] = pa[:, _C_KI:_C_KI + LANES].T[0:IDX_DIM]


def _const_spec(shape):
    return pl.BlockSpec(shape, lambda i: (0,) * len(shape), pipeline_mode=pl.Buffered(1))


def _inproj_call(x2d, g, wa, wr, tm, t_per_batch=None):
    N = x2d.shape[0]
    assert N % tm == 0
    rows = lambda w: pl.BlockSpec((tm, w), lambda i: (i, 0))
    widths = [ATT_WIDTH, ATT_WIDTH, KV_WIDTH, KV_WIDTH, IDX_DIM, N_IDX_HEADS, RW_COLS]
    dtypes = [BF16, BF16, F32, F32, F32, F32, F32]
    out_specs = [rows(w) for w in widths]
    out_shape = [jax.ShapeDtypeStruct((N, w), d) for w, d in zip(widths, dtypes)]
    if t_per_batch is not None:
        assert t_per_batch % tm == 0
        tpb = t_per_batch // tm
        for feat in (KV_WIDTH, KV_WIDTH, IDX_DIM):
            out_specs.append(pl.BlockSpec((1, feat, tm), lambda i: (i // tpb, 0, i % tpb)))
            out_shape.append(jax.ShapeDtypeStruct((N // t_per_batch, feat, t_per_batch), F32))
    return pl.pallas_call(
        _inproj_kernel,
        grid=(N // tm,),
        in_specs=[rows(D_MODEL), _const_spec((1, D_MODEL)), _const_spec((D_MODEL, ATT_COLS)),
                  _const_spec((D_MODEL, RW_COLS))],
        out_specs=out_specs,
        out_shape=out_shape,
        compiler_params=pltpu.CompilerParams(dimension_semantics=("arbitrary",),
                                             vmem_limit_bytes=VMEM_LIMIT),
        name="rmsnorm_inproj",
    )(x2d, g, wa, wr)


def _tail_kernel(x_ref, att_ref, rw_ref, g1_ref, wgate_ref, wao_ref, wro_ref, wout_ref,
                 g2_ref, wff1_ref, wff2_ref, gf_ref, y_ref):
    x = x_ref[...]
    h = _rms(x, g1_ref[...]).astype(BF16)
    gates = jax.nn.sigmoid(jnp.dot(h, wgate_ref[...], preferred_element_type=F32))
    z_a = jnp.dot(att_ref[...], wao_ref[...], preferred_element_type=F32)
    z_b = jnp.dot(rw_ref[...], wro_ref[...], preferred_element_type=F32)
    mixed = gates[:, 0:D_MODEL] * z_a + gates[:, D_MODEL:2 * D_MODEL] * z_b
    x1 = x + jnp.dot(mixed.astype(BF16), wout_ref[...], preferred_element_type=F32)
    h2 = _rms(x1, g2_ref[...]).astype(BF16)
    f = jnp.maximum(jnp.dot(h2, wff1_ref[...], preferred_element_type=F32), 0.0)
    x2 = x1 + jnp.dot((f * f).astype(BF16), wff2_ref[...], preferred_element_type=F32)
    y_ref[...] = _rms(x2, gf_ref[...])


def _tail_call(x2d, att, rw, g1, wgate, wao, wro, wout, g2, wff1, wff2, gf, tm):
    N = x2d.shape[0]
    assert N % tm == 0
    rows = lambda w: pl.BlockSpec((tm, w), lambda i: (i, 0))
    return pl.pallas_call(
        _tail_kernel,
        grid=(N // tm,),
        in_specs=[rows(D_MODEL), rows(ATT_WIDTH), rows(RW_WIDTH), _const_spec((1, D_MODEL)),
                  _const_spec((D_MODEL, 2 * D_MODEL)), _const_spec((ATT_WIDTH, D_MODEL)),
                  _const_spec((RW_WIDTH, D_MODEL)), _const_spec((D_MODEL, D_MODEL)),
                  _const_spec((1, D_MODEL)), _const_spec((D_MODEL, D_FF)), _const_spec((D_FF, D_MODEL)),
                  _const_spec((1, D_MODEL))],
        out_specs=rows(D_MODEL),
        out_shape=jax.ShapeDtypeStruct((N, D_MODEL), F32),
        compiler_params=pltpu.CompilerParams(dimension_semantics=("arbitrary",),
                                             vmem_limit_bytes=VMEM_LIMIT),
        name="merge_out_mlp",
    )(x2d, att, rw, g1, wgate, wao, wro, wout, g2, wff1, wff2, gf)


_IN_SPLITS = (ATT_WIDTH, KV_WIDTH, KV_WIDTH, N_IDX_HEADS * IDX_DIM, IDX_DIM, N_IDX_HEADS,
              RW_COLS, D_MODEL, D_MODEL)


def _split_w_in(w_in):
    offs = [0]
    for s in _IN_SPLITS:
        offs.append(offs[-1] + s)
    col = lambda n: w_in[:, offs[n]:offs[n + 1]]
    pad = jnp.zeros((D_MODEL, ATT_COLS - _C_WI - N_IDX_HEADS), w_in.dtype)
    wa = jnp.concatenate([col(0), col(3), col(1), col(2), col(4), col(5), pad], axis=1).astype(BF16)
    wr = col(6).astype(BF16)
    wgate = jnp.concatenate([col(7), col(8)], axis=1).astype(BF16)
    return wa, wr, wgate


def _branch(x, hist, wkv0, shift0, rel_bias, prm, w, tm_in, tm_tail, Tq, C, nb_rw):
    B, T, _ = x.shape
    x2d = x.reshape(B * T, D_MODEL)
    r3 = lambda a: a.reshape(B, T, a.shape[-1])
    heads_last = lambda aT: jnp.transpose(aT.reshape(B, N_KV_HEADS, HEAD_DIM, T), (0, 3, 1, 2))[None]
    if hist is None:
        q, qi, k, v, ki, wi, p_rw, kT, vT, kiT = _inproj_call(x2d, prm['g1'], w['wa'], w['wr'], tm_in, T)
        new = (r3(k), r3(v), r3(ki))
        new_k, new_v, new_ki = heads_last(kT), heads_last(vT), jnp.swapaxes(kiT, 1, 2)[None]
    else:
        q, qi, k, v, ki, wi, p_rw = _inproj_call(x2d, prm['g1'], w['wa'], w['wr'], tm_in)
        new = tuple(jnp.swapaxes(r3(a), 1, 2) for a in (k, v, ki))
        new_k = k.reshape(1, B, T, N_KV_HEADS, HEAD_DIM)
        new_v = v.reshape(1, B, T, N_KV_HEADS, HEAD_DIM)
        new_ki = ki.reshape(1, B, T, IDX_DIM)
    rw, wkv_new = _rwkv_call(r3(p_rw), shift0, wkv0, prm, C, nb_rw)
    att = _dsa_call(r3(q), r3(qi), r3(wi), hist, new, rel_bias, Tq)
    y = _tail_call(x2d, att.reshape(B * T, ATT_WIDTH), rw.reshape(B * T, RW_WIDTH), prm['g1'],
                   w['wgate'], w['wao'], w['wro'], w['wout'], prm['g2'], w['wff1'], w['wff2'],
                   prm['gf'], tm_tail)
    wkv = wkv_new[None]
    shift = r3(p_rw)[:, T - 1:T, :][None]
    return y.reshape(B, T, D_MODEL), new_k, new_v, new_ki, wkv, shift


def kernel(x_prompt, x_sample, cache_k, cache_v, cache_kidx, state_wkv, state_shift, rel_bias,
           norm1_g, w_in, w_att_out, mu_shift, w0, w_decay_up, a0, w_a_up, w_g_up, k_k, k_a, r_k,
           lnx_w, lnx_b, w_rw_out, w_out, norm2_g, w_ff1, w_ff2, norm_f_g):
    assert w_in.shape[0] == 1, "single-layer step"
    wa, wr, wgate = _split_w_in(w_in[0])
    w = dict(wa=wa, wr=wr, wgate=wgate, wao=w_att_out[0].astype(BF16), wro=w_rw_out[0].astype(BF16),
             wout=w_out[0].astype(BF16), wff1=w_ff1[0].astype(BF16), wff2=w_ff2[0].astype(BF16))
    prm = dict(mu_shift=mu_shift[0], w0=w0[0], w_decay_up=w_decay_up[0], a0=a0[0], w_a_up=w_a_up[0],
               w_g_up=w_g_up[0], k_k=k_k[0], k_a=k_a[0], r_k=r_k[0].reshape(-1), lnx_w=lnx_w[0],
               lnx_b=lnx_b[0], g1=norm1_g[0].reshape(1, -1), g2=norm2_g[0].reshape(1, -1),
               gf=norm_f_g.reshape(1, -1))
    Bp, Tp, _ = x_prompt.shape
    Bs, Ts, _ = x_sample.shape
    past = cache_k.shape[2]
    zero_state = jnp.zeros((Bp, RW_HEADS, RW_HEAD, RW_HEAD), F32)
    zero_shift = jnp.zeros((Bp, 1, RW_COLS), F32)
    outs_p = _branch(x_prompt, None, zero_state, zero_shift, rel_bias, prm, w,
                     tm_in=512, tm_tail=256, Tq=Q_BLOCK, C=CHUNK, nb_rw=4)
    hist = (jnp.transpose(cache_k[0], (0, 2, 3, 1)), jnp.transpose(cache_v[0], (0, 2, 3, 1)),
            jnp.swapaxes(cache_kidx[0], 1, 2))
    outs_s = _branch(x_sample, hist, state_wkv[0], state_shift[0], rel_bias, prm, w,
                     tm_in=Bs * Ts, tm_tail=256, Tq=Ts, C=Ts, nb_rw=4)
    return (outs_p[0], outs_s[0]) + outs_p[1:] + outs_s[1:]
```

```python
import functools
import math

import jax
import jax.numpy as jnp
import numpy as np
from jax import lax
from jax.experimental import pallas as pl
from jax.experimental.pallas import tpu as pltpu

F32 = jnp.float32
BF16 = jnp.bfloat16

D_MODEL = 1024
CHUNK = 64
N_HEADS = 8
N_KV_HEADS = 2
HEAD_DIM = 64
GROUP = N_HEADS // N_KV_HEADS
ATT_WIDTH = N_HEADS * HEAD_DIM
KV_WIDTH = N_KV_HEADS * HEAD_DIM
N_IDX_HEADS = 8
IDX_DIM = 64
MAX_TOPK = 256
Q_BLOCK = 128
N_BUCKETS = 32
RW_HEADS = 8
RW_HEAD = 64
RW_WIDTH = RW_HEADS * RW_HEAD
DECAY_LORA = 64
AAA_LORA = 64
GATE_LORA = 128
RW_COLS = 3 * RW_WIDTH + DECAY_LORA + AAA_LORA + GATE_LORA
GN_EPS = 64e-5
D_FF = 4 * D_MODEL
NORM_EPS = 1e-6
NEG = -1e30

LANES = 128
N_PAIRS = RW_HEADS // 2
VMEM_LIMIT = 56 * 1024 * 1024

CHUNK_PIECES = 1
SEG_PIECES = 2

_NN, _NT, _TN = (1, 0), (1, 1), (0, 0)


def _split(x, n):
    pieces = []
    rem = x
    for i in range(n):
        p = rem.astype(BF16)
        pieces.append(p)
        if i + 1 < n:
            rem = rem - p.astype(F32)
    return tuple(pieces)


def _mm(a, b, dims=_NN):
    n = max(len(a), len(b))
    terms = [(i, j) for i in range(len(a)) for j in range(len(b)) if i + j < n]
    ca, cb = dims
    lhs = jnp.concatenate([a[i] for i, _ in terms], axis=ca) if len(terms) > 1 else a[0]
    rhs = jnp.concatenate([b[j] for _, j in terms], axis=cb) if len(terms) > 1 else b[0]
    return lax.dot_general(lhs, rhs, (((ca,), (cb,)), ((), ())), preferred_element_type=F32)


def _lane_iota(shape):
    return lax.broadcasted_iota(jnp.int32, shape, len(shape) - 1)


def _row_iota(shape):
    return lax.broadcasted_iota(jnp.int32, shape, len(shape) - 2)


def _bdrows(x, width, heads=2):
    head = (_lane_iota(x.shape) % (heads * width)) // width
    return jnp.concatenate([jnp.where(head == h, x, 0.0) for h in range(heads)], axis=0)


def _rwkv_kernel(x_ref, shift0_ref, h0_ref, mu_ref, w0_ref, a0_ref, kk_ref, ka_ref, rk_ref,
                 lnw_ref, lnb_ref, wlora_ref, wg_ref,
                 out_ref, hT_ref, xbuf, hst, *, C, n_sq, nb):
    c = pl.program_id(1)

    @pl.when(c == 0)
    def _():
        for n in range(nb):
            xbuf[n, 7:8, :] = shift0_ref[n]
        z = jnp.zeros((RW_HEAD, RW_HEAD), F32)
        for n in range(nb):
            for p in range(N_PAIRS):
                blockdiag = jnp.concatenate(
                    [jnp.concatenate([h0_ref[n, 2 * p], z], axis=1),
                     jnp.concatenate([z, h0_ref[n, 2 * p + 1]], axis=1)], axis=0)
                hst[n, p] = blockdiag.T

    W = RW_WIDTH
    R = nb * C
    prevs = []
    for n in range(nb):
        xn = x_ref[n]
        xbuf[n, 8:8 + C, :] = xn
        prevs.append(xbuf[n, 7:7 + C, :])
        xbuf[n, 7:8, :] = xn[C - 1:C, :]
    x = x_ref[...].reshape(R, RW_COLS)
    prev = jnp.concatenate(prevs, axis=0)
    xs = x + (prev - x) * mu_ref[...]
    r = xs[:, 0:W]
    k = xs[:, W:2 * W]
    v = xs[:, 2 * W:3 * W]
    da = xs[:, 3 * W:3 * W + LANES]
    gd = xs[:, 3 * W + LANES:3 * W + 2 * LANES]

    t_in = jnp.where(_lane_iota(da.shape) < DECAY_LORA, jnp.tanh(da), da)
    lo = jnp.dot(t_in.astype(BF16), wlora_ref[...], preferred_element_type=F32)
    ld = -math.exp(-0.5) * jax.nn.sigmoid(w0_ref[...] + lo[:, 0:W])
    a = jax.nn.sigmoid(a0_ref[...] + lo[:, W:2 * W])
    g = jnp.dot(jax.nn.sigmoid(gd).astype(BF16), wg_ref[...], preferred_element_type=F32)

    seg = (_row_iota((LANES, LANES)) // RW_HEAD == _lane_iota((LANES, LANES)) // RW_HEAD).astype(BF16)

    def segsum(t):
        return jnp.concatenate(
            [_mm(_split(t[:, p * LANES:(p + 1) * LANES], SEG_PIECES), (seg,)) for p in range(N_PAIRS)],
            axis=1)

    kk0 = k * kk_ref[...]
    kk = kk0 * jnp.minimum(lax.rsqrt(segsum(kk0 * kk0)), 1e12)
    k2 = k * (1.0 + (a - 1.0) * ka_ref[...])
    bb = kk * a

    ti, si = _row_iota((R, R)), _lane_iota((R, R))
    tri = jnp.logical_and(si <= ti, si // C == ti // C).astype(BF16)
    lp = _mm((tri,), _split(ld, 3))
    lpp = lp - ld
    rows_of = lambda i: jnp.concatenate(
        [jnp.broadcast_to(lp[n * C + i:n * C + i + 1, :], (C, W)) for n in range(nb)], axis=0)
    lpc = rows_of(C - 1)
    mid = rows_of(C // 2 - 1)
    r_abs = r * jnp.exp(lp)
    kap_abs = kk * jnp.exp(lpp)
    r_mid = r * jnp.exp(lp - mid)
    kap_mid = kk * jnp.exp(lpp - mid)
    e_mid = jnp.exp(mid - lp)
    b_mid = bb * e_mid
    k_mid = k2 * e_mid
    e_end = jnp.exp(lpc - lp)
    b_end = bb * e_end
    k_end = k2 * e_end
    p_end = jnp.exp(lpc)

    tt = _row_iota((C, 2 * C))
    ss = _lane_iota((C, 2 * C)) % C
    strict = ss < tt
    incl = ss <= tt
    eye_p = (ss == tt).astype(F32)
    jrow = _row_iota((LANES, LANES))
    jlane = _lane_iota((LANES, LANES))
    same_head = (jrow // RW_HEAD) == (jlane // RW_HEAD)
    eye_j = jrow == jlane

    sp = lambda t: _split(t, CHUNK_PIECES)
    bd = lambda pieces, half: tuple(_bdrows(t, half) for t in pieces)

    chains = [(n, p) for n in range(nb) for p in range(N_PAIRS)]
    blk = lambda t, ch: t[ch[0] * C:(ch[0] + 1) * C, ch[1] * LANES:(ch[1] + 1) * LANES]
    each = lambda f: [f(i, ch) for i, ch in enumerate(chains)]

    A = each(lambda i, ch: _mm(
        sp(jnp.concatenate([blk(kap_mid, ch), blk(r_mid, ch)], axis=0)),
        sp(jnp.concatenate([_bdrows(blk(b_mid, ch), RW_HEAD), _bdrows(blk(k_mid, ch), RW_HEAD)],
                           axis=0)), _NT))
    n_kb = each(lambda i, ch: jnp.where(strict, A[i][0:C, 0:2 * C], 0.0))
    a_kk = each(lambda i, ch: jnp.where(strict, A[i][0:C, 2 * C:4 * C], 0.0))
    a_rb = each(lambda i, ch: jnp.where(incl, A[i][C:2 * C, 0:2 * C], 0.0))
    a_rk = each(lambda i, ch: jnp.where(incl, A[i][C:2 * C, 2 * C:4 * C], 0.0))
    T = each(lambda i, ch: eye_p - n_kb[i])
    pw = each(lambda i, ch: sp(n_kb[i]))
    for _ in range(n_sq):
        pw = each(lambda i, ch: sp(_mm(pw[i], bd(pw[i], C))))
        T = each(lambda i, ch: T[i] + _mm(sp(T[i]), bd(pw[i], C)))
    v_p = each(lambda i, ch: sp(blk(v, ch)))
    av = each(lambda i, ch: _mm(sp(jnp.concatenate([a_kk[i], a_rk[i]], axis=0)),
                                bd(v_p[i], RW_HEAD)))
    wu = each(lambda i, ch: sp(_mm(
        sp(T[i]), bd(sp(jnp.concatenate([blk(kap_abs, ch), av[i][0:C]], axis=1)), RW_HEAD))))
    qy = each(lambda i, ch: jnp.concatenate([blk(r_abs, ch), av[i][C:2 * C]], axis=1)
              - _mm(sp(a_rb[i]), bd(wu[i], RW_HEAD)))
    bw = each(lambda i, ch: _mm(sp(blk(b_end, ch)), wu[i], _TN))
    kv = each(lambda i, ch: _mm(_split(blk(k_end, ch), 2), _split(blk(v, ch), 2), _TN))
    G = each(lambda i, ch: jnp.where(same_head, kv[i] - bw[i][:, LANES:2 * LANES], 0.0))
    M = each(lambda i, ch: jnp.where(same_head, bw[i][:, 0:LANES], 0.0))
    p_col = each(lambda i, ch: jnp.sum(jnp.where(eye_j, blk(p_end, ch)[0:1], 0.0), axis=1,
                                       keepdims=True))
    h0f = each(lambda i, ch: hst[ch[0], ch[1]])
    h0 = each(lambda i, ch: sp(h0f[i]))
    yc_ = each(lambda i, ch: _mm(sp(qy[i][:, 0:LANES]), h0[i]) + qy[i][:, LANES:2 * LANES])
    for i, ch in enumerate(chains):
        hst[ch[0], ch[1]] = p_col[i] * h0f[i] - _mm(sp(M[i]), h0[i]) + G[i]
    y = jnp.concatenate([jnp.concatenate(yc_[n * N_PAIRS:(n + 1) * N_PAIRS], axis=1)
                         for n in range(nb)], axis=0)

    mean = segsum(y) * (1.0 / RW_HEAD)
    yc = y - mean
    var = segsum(yc * yc) * (1.0 / RW_HEAD)
    yn = yc * lax.rsqrt(var + GN_EPS) * lnw_ref[...] + lnb_ref[...]
    bonus = segsum(r * k2 * rk_ref[...]) * v
    out_ref[...] = ((yn + bonus) * g).astype(out_ref.dtype).reshape(nb, C, W)

    @pl.when(c == pl.num_programs(1) - 1)
    def _():
        for n in range(nb):
            for p in range(N_PAIRS):
                blockdiag = hst[n, p].T
                hT_ref[n, 2 * p] = blockdiag[0:RW_HEAD, 0:RW_HEAD]
                hT_ref[n, 2 * p + 1] = blockdiag[RW_HEAD:LANES, RW_HEAD:LANES]


def _rwkv_call(p_rw, shift0, wkv0, prm, C, nb):
    B, T, _ = p_rw.shape
    n_sq = int(math.log2(C)) - 1
    assert 2 ** (n_sq + 1) == C and T % C == 0 and B % nb == 0
    row = lambda a: a.reshape(1, -1).astype(F32)
    vecs = [row(prm['mu_shift']), row(prm['w0']), row(prm['a0']), row(prm['k_k']), row(prm['k_a']),
            row(prm['r_k']), row(prm['lnx_w']), row(prm['lnx_b'])]
    wlora = jnp.zeros((LANES, 2 * RW_WIDTH), F32)
    wlora = wlora.at[0:DECAY_LORA, 0:RW_WIDTH].set(prm['w_decay_up'])
    wlora = wlora.at[DECAY_LORA:, RW_WIDTH:].set(prm['w_a_up']).astype(BF16)
    wg = prm['w_g_up'].astype(BF16)
    const = lambda shape: pl.BlockSpec(shape, lambda b, c: (0,) * len(shape))
    kern = functools.partial(_rwkv_kernel, C=C, n_sq=n_sq, nb=nb)
    return pl.pallas_call(
        kern,
        grid=(B // nb, T // C),
        in_specs=[pl.BlockSpec((nb, C, RW_COLS), lambda b, c: (b, c, 0)),
                  pl.BlockSpec((nb, 1, RW_COLS), lambda b, c: (b, 0, 0)),
                  pl.BlockSpec((nb, RW_HEADS, RW_HEAD, RW_HEAD), lambda b, c: (b, 0, 0, 0)),
                  const((1, RW_COLS))] + [const((1, RW_WIDTH))] * 7
                 + [const((LANES, 2 * RW_WIDTH)), const((GATE_LORA, RW_WIDTH))],
        out_specs=[pl.BlockSpec((nb, C, RW_WIDTH), lambda b, c: (b, c, 0)),
                   pl.BlockSpec((nb, RW_HEADS, RW_HEAD, RW_HEAD), lambda b, c: (b, 0, 0, 0))],
        out_shape=[jax.ShapeDtypeStruct((B, T, RW_WIDTH), BF16),
                   jax.ShapeDtypeStruct((B, RW_HEADS, RW_HEAD, RW_HEAD), F32)],
        scratch_shapes=[pltpu.VMEM((nb, 8 + C, RW_COLS), F32),
                        pltpu.VMEM((nb, N_PAIRS, LANES, LANES), F32)],
        compiler_params=pltpu.CompilerParams(dimension_semantics=("arbitrary", "arbitrary"),
                                             vmem_limit_bytes=VMEM_LIMIT),
        name="rwkv7_chunked",
    )(p_rw, shift0, wkv0, *vecs, wlora, wg)


_INT_MIN = -2 ** 31
_NEG_KEY = int(np.float32(NEG).view(np.int32)) ^ 0x7FFFFFFF
_T5_STEPS = ((91, 14), (64, 13), (46, 12), (32, 11), (23, 10), (16, 9), (12, 8), (8, 7), (7, 6),
             (6, 5), (5, 4), (4, 3), (3, 2), (2, 1), (1, 0))
_T5_FAR = 15
_KEY_TILE = 256


def _half_masked(x, half):
    low = (_lane_iota(x.shape) % LANES) < HEAD_DIM
    keep = low if half == 0 else jnp.logical_not(low)
    return jnp.where(keep, x, jnp.zeros_like(x))


def _dup_half(x, half):
    xr = pltpu.roll(x, HEAD_DIM, axis=1)
    low = _lane_iota(x.shape) < HEAD_DIM
    return jnp.where(low, x, xr) if half == 0 else jnp.where(low, xr, x)


def _dot_nt(a, b):
    return lax.dot_general(a, b, (((1,), (1,)), ((), ())), preferred_element_type=F32)


def _dsa_kernel(*refs, Tq, P, Tn, Lp, topk, n_cls):
    if P > 0:
        (q_ref, qi_ref, wi_ref, hk_ref, hv_ref, hki_ref, nk_ref, nv_ref, nki_ref, rb_ref,
         att_ref, kd, vd, kid, lg, tbl, sbias) = refs
    else:
        (q_ref, qi_ref, wi_ref, nk_ref, nv_ref, nki_ref, rb_ref,
         att_ref, kd, vd, kid, lg, tbl, sbias) = refs
    b = pl.program_id(0)
    i = pl.program_id(1)
    L = P + Tn

    @pl.when(jnp.logical_and(b == 0, i == 0))
    def _():
        shape = (Tq, 2 * LANES)
        rel = _lane_iota(shape) - LANES - _row_iota(shape)
        n = jnp.abs(rel)
        for h in range(N_HEADS):
            far = rb_ref[_T5_FAR, h]
            neg = jnp.full(shape, far, F32)
            pos = jnp.full(shape, rb_ref[_T5_FAR + N_BUCKETS // 2, h], F32)
            for bound, bucket in _T5_STEPS:
                neg = jnp.where(n < bound, rb_ref[bucket, h], neg)
                pos = jnp.where(n < bound, rb_ref[bucket + N_BUCKETS // 2, h], pos)
            tbl[h, :, 0:2 * LANES] = jnp.where(rel > 0, pos, neg) - far
            tbl[h, :, 2 * LANES:3 * LANES] = jnp.zeros((Tq, LANES), F32)

    @pl.when(i == 0)
    def _():
        if P > 0:
            def put(dst, x, c0):
                xb = x.astype(BF16)
                dst[0:HEAD_DIM, c0:c0 + x.shape[1]] = xb
                dst[HEAD_DIM:LANES, c0:c0 + x.shape[1]] = xb

            zt = jnp.zeros((LANES, Lp - P), BF16)
            for j in range(N_KV_HEADS):
                put(kd.at[j], hk_ref[0, j], 0)
                kd[j, :, P:Lp] = zt
                put(kd.at[j], nk_ref[0, j * HEAD_DIM:(j + 1) * HEAD_DIM, :], P)
                for u in range(2):
                    vd[2 * j + u] = jnp.ones((LANES, Lp), BF16)
                    r0 = u * HEAD_DIM
                    vd[2 * j + u, r0:r0 + HEAD_DIM, 0:P] = hv_ref[0, j].astype(BF16)
                    vd[2 * j + u, r0:r0 + HEAD_DIM, P:P + Tn] = (
                        nv_ref[0, j * HEAD_DIM:(j + 1) * HEAD_DIM, :].astype(BF16))
            put(kid, hki_ref[0], 0)
            kid[:, P:Lp] = zt
            put(kid, nki_ref[0], P)
        else:
            kf, vf, kif = nk_ref[0], nv_ref[0], nki_ref[0]
            low = _lane_iota(vf.shape) < HEAD_DIM
            for j in range(N_KV_HEADS):
                kd[j, 0:Tn, :] = _dup_half(kf, j).astype(BF16)
                vdup = _dup_half(vf, j)
                vd[2 * j, 0:Tn, :] = jnp.where(low, vdup, 1.0).astype(BF16)
                vd[2 * j + 1, 0:Tn, :] = jnp.where(low, 1.0, vdup).astype(BF16)
            kid[0:Tn, :] = jnp.concatenate([kif, kif], axis=1).astype(BF16)

    stride = Lp // n_cls
    k_hi = P + (i + 1) * Tq
    cls = (k_hi - 1) // stride
    for c in range(n_cls):
        pl.when(cls == c)(functools.partial(
            _dsa_block, q_ref, qi_ref, wi_ref, att_ref, kd, vd, kid, lg, tbl, sbias,
            i=i, Tq=Tq, P=P, L=L, Lk=(c + 1) * stride, topk=topk))


def _dsa_block(q_ref, qi_ref, wi_ref, att_ref, kd, vd, kid, lg, tbl, sbias, *, i, Tq, P, L, Lk, topk):
    q = q_ref[0]
    qi = qi_ref[0]
    key_major = Tq % LANES == 0
    kax = 0 if key_major else 1
    shape = (Lk, Tq) if key_major else (Tq, Lk)
    qvec = (1, Tq) if key_major else (Tq, 1)
    wi = wi_ref[0]

    lanes_keys = P > 0
    kidv = kid[:, 0:Lk] if lanes_keys else kid[0:Lk, :]
    head_q = lambda h: _half_masked(qi[:, (h // 2) * LANES:(h // 2 + 1) * LANES], h % 2)
    score = jnp.zeros(shape, F32)
    if key_major:
        pair_q = [jnp.concatenate([head_q(2 * pr), head_q(2 * pr + 1)], axis=0)
                  for pr in range(N_IDX_HEADS // 2)]
        tiles = []
        tile = _KEY_TILE if Lk % _KEY_TILE == 0 else LANES
        for kt in range(Lk // tile):
            kid_t = kidv[kt * tile:(kt + 1) * tile]
            acc = jnp.zeros((tile, Tq), F32)
            for pr in range(N_IDX_HEADS // 2):
                s2 = _dot_nt(kid_t, pair_q[pr])
                for u in range(2):
                    h = 2 * pr + u
                    acc = acc + wi[h:h + 1, :] * jnp.maximum(s2[:, u * Tq:(u + 1) * Tq], 0.0)
            tiles.append(acc)
        score = jnp.concatenate(tiles, axis=0)
    else:
        lhs_all = jnp.concatenate([head_q(h) for h in range(N_IDX_HEADS)], axis=0)
        s_all = (jnp.dot(lhs_all, kidv, preferred_element_type=F32) if lanes_keys
                 else _dot_nt(lhs_all, kidv))
        for h in range(N_IDX_HEADS):
            score = score + wi[:, h:h + 1] * jnp.maximum(s_all[h * Tq:(h + 1) * Tq], 0.0)

    s_pos = lax.broadcasted_iota(jnp.int32, shape, kax)
    q_pos = P + i * Tq + lax.broadcasted_iota(jnp.int32, shape, 1 - kax)
    adm = jnp.logical_and(s_pos // CHUNK <= q_pos // CHUNK, s_pos < L)
    score = jnp.where(adm, score, NEG)

    def count(mask):
        ones = jnp.where(mask, 1.0, 0.0)
        if key_major:
            ones = jnp.sum(ones.reshape(Lk // LANES, LANES, Tq), axis=0)
        return jnp.sum(ones, axis=kax, keepdims=True)

    def key_to_float(key):
        return pltpu.bitcast(jnp.where(key < 0, key ^ jnp.int32(0x7FFFFFFF), key), F32)

    kf = float(topk)
    ans = jnp.where(count(score >= 0.0) >= kf, jnp.int32(0), jnp.int32(_INT_MIN))

    def bit_step(it, ans):
        cand = ans | (jnp.int32(1) << (30 - it))
        return jnp.where(count(score >= key_to_float(cand)) >= kf, cand, ans)

    def two_bit_step(it, ans):
        shift = 28 - 2 * it
        for m in (1, 2, 3):
            cand = ans | (jnp.int32(m) << shift)
            ans_m = jnp.where(count(score >= key_to_float(cand)) >= kf, cand, ans)
            best = ans_m if m == 1 else jnp.maximum(best, ans_m)
        return best

    def stacked_q(j):
        return jnp.concatenate(
            [_half_masked(q[:, ((j * GROUP + g) // 2) * LANES:((j * GROUP + g) // 2 + 1) * LANES], g % 2)
             for g in range(GROUP)], axis=0)

    def logits_rows(j, lhs_rows, rsl):
        lg[j, rsl, 0:Lk] = (jnp.dot(lhs_rows, kd[j, :, 0:Lk], preferred_element_type=F32) if lanes_keys
                            else _dot_nt(lhs_rows, kd[j, 0:Lk, :]))

    if key_major:
        thr_key = lax.fori_loop(0, 31, bit_step, ans)
    else:
        thr_key = lax.fori_loop(0, 15, two_bit_step, bit_step(0, ans))
    thr_key = jnp.maximum(thr_key, jnp.int32(_NEG_KEY))
    thr = key_to_float(thr_key)
    gt = score > thr
    eq = score == thr
    need = kf - count(gt)
    tied = jnp.logical_and(count(eq) > need, thr > NEG)
    any_tied = jnp.max(jnp.where(tied, 1.0, 0.0)) > 0.0

    n_idx_bits = (Lk - 1).bit_length()

    def idx_search():
        def idx_step(it, x):
            cand = x | (jnp.int32(1) << (n_idx_bits - 1 - it))
            return jnp.where(count(jnp.logical_and(eq, s_pos < cand)) < need, cand, x)
        return lax.fori_loop(0, n_idx_bits, idx_step, jnp.zeros(qvec, jnp.int32))

    last = lax.cond(any_tied, idx_search, lambda: jnp.full(qvec, Lk, jnp.int32))
    sel = jnp.logical_and(jnp.logical_or(gt, jnp.logical_and(eq, s_pos <= last)), adm)
    sb = jnp.where(sel, 0.0, NEG)
    sbias[:, 0:Lk] = sb.T if key_major else sb

    blk_end = (P + i * Tq) // LANES + 1
    start = pl.multiple_of(jnp.maximum(blk_end - 2, 0) * LANES, LANES)
    t_off = pl.multiple_of(jnp.maximum(2 - blk_end, 0) * LANES, LANES)
    for j in range(N_KV_HEADS):
        logits_rows(j, stacked_q(j), slice(0, GROUP * Tq))
    for j in range(N_KV_HEADS):
        for g in range(GROUP):
            r = slice(g * Tq, (g + 1) * Tq)
            if Lk < 2 * LANES:
                lg[j, r, 0:LANES] = lg[j, r, 0:LANES] + tbl[j * GROUP + g, :, LANES:2 * LANES]
            else:
                lg[j, r, pl.ds(start, 2 * LANES)] = (lg[j, r, pl.ds(start, 2 * LANES)]
                                                     + tbl[j * GROUP + g, :, pl.ds(t_off, 2 * LANES)])
    for j in range(N_KV_HEADS):
        sb = sbias[:, 0:Lk]
        ps = []
        for g in range(GROUP):
            lgv = lg[j, g * Tq:(g + 1) * Tq, 0:Lk]
            m = jnp.max(lgv + sb, axis=1, keepdims=True)
            ps.append(jnp.exp((lgv - m) + sb).astype(BF16))
        pv = lambda p, u: (_dot_nt(p, vd[2 * j + u, :, 0:Lk]) if lanes_keys
                           else jnp.dot(p, vd[2 * j + u, 0:Lk, :], preferred_element_type=F32))
        if key_major:
            os_ = [pv(ps[g], g % 2) for g in range(GROUP)]
        else:
            stacked = [pv(jnp.concatenate(ps[u::2], axis=0), u) for u in range(2)]
            os_ = [stacked[g % 2][(g // 2) * Tq:(g // 2 + 1) * Tq] for g in range(GROUP)]
        low = _lane_iota((Tq, LANES)) < HEAD_DIM
        for pp in range(GROUP // 2):
            c0 = (j * GROUP // 2 + pp) * LANES
            o_even, o_odd = os_[2 * pp], os_[2 * pp + 1]
            num = jnp.where(low, o_even, o_odd)
            den = jnp.where(low, pltpu.roll(o_even, HEAD_DIM, axis=1), pltpu.roll(o_odd, HEAD_DIM, axis=1))
            att_ref[0, :, c0:c0 + LANES] = (num / den).astype(att_ref.dtype)


def _dsa_call(q, qi, wi, hist, new, rel_bias, Tq):
    B, T, _ = q.shape
    P = 0 if hist is None else hist[0].shape[-1]
    Tn = new[0].shape[2 if P > 0 else 1]
    L = P + Tn
    Lp = -(-L // LANES) * LANES
    topk = min(MAX_TOPK, L // 4)
    assert T % Tq == 0 and Tn == T and P % LANES == 0 and (Tq % LANES == 0 or T == Tq)
    assert (P == 0 and Lp == L) or (P > 0 and Tq % LANES != 0)
    qspec = lambda w: pl.BlockSpec((1, Tq, w), lambda b, i: (b, i, 0))
    full = lambda *dims: pl.BlockSpec((1,) + dims, lambda b, i: (b,) + (0,) * len(dims))
    if Tq % LANES == 0:
        wi = jnp.swapaxes(wi, 1, 2)
        wi_spec = pl.BlockSpec((1, N_IDX_HEADS, Tq), lambda b, i: (b, 0, i))
    else:
        wi_spec = qspec(N_IDX_HEADS)
    in_specs = [qspec(ATT_WIDTH), qspec(N_IDX_HEADS * IDX_DIM), wi_spec]
    args = [q, qi, wi]
    if P > 0:
        in_specs += [full(N_KV_HEADS, HEAD_DIM, P), full(N_KV_HEADS, HEAD_DIM, P), full(IDX_DIM, P),
                     full(KV_WIDTH, Tn), full(KV_WIDTH, Tn), full(IDX_DIM, Tn)]
        args += list(hist)
        kv_scratch, ki_scratch = (N_KV_HEADS, LANES, Lp), (LANES, Lp)
    else:
        in_specs += [full(Tn, KV_WIDTH), full(Tn, KV_WIDTH), full(Tn, IDX_DIM)]
        kv_scratch, ki_scratch = (N_KV_HEADS, Lp, LANES), (Lp, LANES)
    in_specs.append(pl.BlockSpec(memory_space=pltpu.SMEM))
    args += list(new) + [rel_bias.astype(F32)]
    n_cls = Lp // 256 if (P == 0 and Lp % 256 == 0) else 1
    kern = functools.partial(_dsa_kernel, Tq=Tq, P=P, Tn=Tn, Lp=Lp, topk=topk, n_cls=n_cls)
    return pl.pallas_call(
        kern,
        grid=(B, T // Tq),
        in_specs=in_specs,
        out_specs=pl.BlockSpec((1, Tq, ATT_WIDTH), lambda b, i: (b, i, 0)),
        out_shape=jax.ShapeDtypeStruct((B, T, ATT_WIDTH), BF16),
        scratch_shapes=[pltpu.VMEM(kv_scratch, BF16),
                        pltpu.VMEM((2 * N_KV_HEADS,) + kv_scratch[1:], BF16),
                        pltpu.VMEM(ki_scratch, BF16),
                        pltpu.VMEM((N_KV_HEADS, GROUP * Tq, Lp), F32),
                        pltpu.VMEM((N_HEADS, Tq, 3 * LANES), F32),
                        pltpu.VMEM((Tq, Lp), F32)],
        compiler_params=pltpu.CompilerParams(dimension_semantics=("arbitrary", "arbitrary"),
                                             vmem_limit_bytes=VMEM_LIMIT),
        name="dsa_attention",
    )(*args)


ATT_COLS = 2 * ATT_WIDTH + 2 * KV_WIDTH + LANES
_C_Q, _C_QI, _C_K, _C_V, _C_KI = 0, 512, 1024, 1152, 1280
_C_WI = _C_KI + IDX_DIM


def _rms(x, g):
    return x * lax.rsqrt(jnp.mean(x * x, axis=-1, keepdims=True) + NORM_EPS) * g


def _inproj_kernel(x_ref, g_ref, wa_ref, wr_ref, q_ref, qi_ref, k_ref, v_ref, ki_ref, wi_ref, prw_ref,
                   *t_refs):
    h = _rms(x_ref[...], g_ref[...]).astype(BF16)
    pa = jnp.dot(h, wa_ref[...], preferred_element_type=F32)
    q_ref[...] = (pa[:, _C_Q:_C_Q + ATT_WIDTH] * (HEAD_DIM ** -0.5)).astype(BF16)
    qi_ref[...] = (pa[:, _C_QI:_C_QI + ATT_WIDTH] * (IDX_DIM ** -0.5)).astype(BF16)
    k_ref[...] = pa[:, _C_K:_C_K + KV_WIDTH]
    v_ref[...] = pa[:, _C_V:_C_V + KV_WIDTH]
    ki_ref[...] = pa[:, _C_KI:_C_KI + IDX_DIM]
    wi_ref[...] = pa[:, _C_WI:_C_WI + N_IDX_HEADS] * (N_IDX_HEADS ** -0.5)
    prw_ref[...] = jnp.dot(h, wr_ref[...], preferred_element_type=F32)
    if t_refs:
        kT_ref, vT_ref, kiT_ref = t_refs
        kT_ref[0] = pa[:, _C_K:_C_K + KV_WIDTH].T
        vT_ref[0] = pa[:, _C_V:_C_V + KV_WIDTH].T
        kiT_ref[0] = pa[:, _C_KI:_C_KI + LANES].T[0:IDX_DIM]


def _const_spec(shape):
    return pl.BlockSpec(shape, lambda i: (0,) * len(shape), pipeline_mode=pl.Buffered(1))


def _inproj_call(x2d, g, wa, wr, tm, t_per_batch=None):
    N = x2d.shape[0]
    assert N % tm == 0
    rows = lambda w: pl.BlockSpec((tm, w), lambda i: (i, 0))
    widths = [ATT_WIDTH, ATT_WIDTH, KV_WIDTH, KV_WIDTH, IDX_DIM, N_IDX_HEADS, RW_COLS]
    dtypes = [BF16, BF16, F32, F32, F32, F32, F32]
    out_specs = [rows(w) for w in widths]
    out_shape = [jax.ShapeDtypeStruct((N, w), d) for w, d in zip(widths, dtypes)]
    if t_per_batch is not None:
        assert t_per_batch % tm == 0
        tpb = t_per_batch // tm
        for feat in (KV_WIDTH, KV_WIDTH, IDX_DIM):
            out_specs.append(pl.BlockSpec((1, feat, tm), lambda i: (i // tpb, 0, i % tpb)))
            out_shape.append(jax.ShapeDtypeStruct((N // t_per_batch, feat, t_per_batch), F32))
    return pl.pallas_call(
        _inproj_kernel,
        grid=(N // tm,),
        in_specs=[rows(D_MODEL), _const_spec((1, D_MODEL)), _const_spec((D_MODEL, ATT_COLS)),
                  _const_spec((D_MODEL, RW_COLS))],
        out_specs=out_specs,
        out_shape=out_shape,
        compiler_params=pltpu.CompilerParams(dimension_semantics=("arbitrary",),
                                             vmem_limit_bytes=VMEM_LIMIT),
        name="rmsnorm_inproj",
    )(x2d, g, wa, wr)


def _tail_kernel(x_ref, att_ref, rw_ref, g1_ref, wgate_ref, wao_ref, wro_ref, wout_ref,
                 g2_ref, wff1_ref, wff2_ref, gf_ref, y_ref):
    x = x_ref[...]
    h = _rms(x, g1_ref[...]).astype(BF16)
    gates = jax.nn.sigmoid(jnp.dot(h, wgate_ref[...], preferred_element_type=F32))
    z_a = jnp.dot(att_ref[...], wao_ref[...], preferred_element_type=F32)
    z_b = jnp.dot(rw_ref[...], wro_ref[...], preferred_element_type=F32)
    mixed = gates[:, 0:D_MODEL] * z_a + gates[:, D_MODEL:2 * D_MODEL] * z_b
    x1 = x + jnp.dot(mixed.astype(BF16), wout_ref[...], preferred_element_type=F32)
    h2 = _rms(x1, g2_ref[...]).astype(BF16)
    f = jnp.maximum(jnp.dot(h2, wff1_ref[...], preferred_element_type=F32), 0.0)
    x2 = x1 + jnp.dot((f * f).astype(BF16), wff2_ref[...], preferred_element_type=F32)
    y_ref[...] = _rms(x2, gf_ref[...])


def _tail_call(x2d, att, rw, g1, wgate, wao, wro, wout, g2, wff1, wff2, gf, tm):
    N = x2d.shape[0]
    assert N % tm == 0
    rows = lambda w: pl.BlockSpec((tm, w), lambda i: (i, 0))
    return pl.pallas_call(
        _tail_kernel,
        grid=(N // tm,),
        in_specs=[rows(D_MODEL), rows(ATT_WIDTH), rows(RW_WIDTH), _const_spec((1, D_MODEL)),
                  _const_spec((D_MODEL, 2 * D_MODEL)), _const_spec((ATT_WIDTH, D_MODEL)),
                  _const_spec((RW_WIDTH, D_MODEL)), _const_spec((D_MODEL, D_MODEL)),
                  _const_spec((1, D_MODEL)), _const_spec((D_MODEL, D_FF)), _const_spec((D_FF, D_MODEL)),
                  _const_spec((1, D_MODEL))],
        out_specs=rows(D_MODEL),
        out_shape=jax.ShapeDtypeStruct((N, D_MODEL), F32),
        compiler_params=pltpu.CompilerParams(dimension_semantics=("arbitrary",),
                                             vmem_limit_bytes=VMEM_LIMIT),
        name="merge_out_mlp",
    )(x2d, att, rw, g1, wgate, wao, wro, wout, g2, wff1, wff2, gf)


_IN_SPLITS = (ATT_WIDTH, KV_WIDTH, KV_WIDTH, N_IDX_HEADS * IDX_DIM, IDX_DIM, N_IDX_HEADS,
              RW_COLS, D_MODEL, D_MODEL)


def _split_w_in(w_in):
    offs = [0]
    for s in _IN_SPLITS:
        offs.append(offs[-1] + s)
    col = lambda n: w_in[:, offs[n]:offs[n + 1]]
    pad = jnp.zeros((D_MODEL, ATT_COLS - _C_WI - N_IDX_HEADS), w_in.dtype)
    wa = jnp.concatenate([col(0), col(3), col(1), col(2), col(4), col(5), pad], axis=1).astype(BF16)
    wr = col(6).astype(BF16)
    wgate = jnp.concatenate([col(7), col(8)], axis=1).astype(BF16)
    return wa, wr, wgate


def _branch(x, hist, wkv0, shift0, rel_bias, prm, w, tm_in, tm_tail, Tq, C, nb_rw):
    B, T, _ = x.shape
    x2d = x.reshape(B * T, D_MODEL)
    r3 = lambda a: a.reshape(B, T, a.shape[-1])
    heads_last = lambda aT: jnp.transpose(aT.reshape(B, N_KV_HEADS, HEAD_DIM, T), (0, 3, 1, 2))[None]
    if hist is None:
        q, qi, k, v, ki, wi, p_rw, kT, vT, kiT = _inproj_call(x2d, prm['g1'], w['wa'], w['wr'], tm_in, T)
        new = (r3(k), r3(v), r3(ki))
        new_k, new_v, new_ki = heads_last(kT), heads_last(vT), jnp.swapaxes(kiT, 1, 2)[None]
    else:
        q, qi, k, v, ki, wi, p_rw = _inproj_call(x2d, prm['g1'], w['wa'], w['wr'], tm_in)
        new = tuple(jnp.swapaxes(r3(a), 1, 2) for a in (k, v, ki))
        new_k = k.reshape(1, B, T, N_KV_HEADS, HEAD_DIM)
        new_v = v.reshape(1, B, T, N_KV_HEADS, HEAD_DIM)
        new_ki = ki.reshape(1, B, T, IDX_DIM)
    rw, wkv_new = _rwkv_call(r3(p_rw), shift0, wkv0, prm, C, nb_rw)
    att = _dsa_call(r3(q), r3(qi), r3(wi), hist, new, rel_bias, Tq)
    y = _tail_call(x2d, att.reshape(B * T, ATT_WIDTH), rw.reshape(B * T, RW_WIDTH), prm['g1'],
                   w['wgate'], w['wao'], w['wro'], w['wout'], prm['g2'], w['wff1'], w['wff2'],
                   prm['gf'], tm_tail)
    wkv = wkv_new[None]
    shift = r3(p_rw)[:, T - 1:T, :][None]
    return y.reshape(B, T, D_MODEL), new_k, new_v, new_ki, wkv, shift


def kernel(x_prompt, x_sample, cache_k, cache_v, cache_kidx, state_wkv, state_shift, rel_bias,
           norm1_g, w_in, w_att_out, mu_shift, w0, w_decay_up, a0, w_a_up, w_g_up, k_k, k_a, r_k,
           lnx_w, lnx_b, w_rw_out, w_out, norm2_g, w_ff1, w_ff2, norm_f_g):
    assert w_in.shape[0] == 1, "single-layer step"
    wa, wr, wgate = _split_w_in(w_in[0])
    w = dict(wa=wa, wr=wr, wgate=wgate, wao=w_att_out[0].astype(BF16), wro=w_rw_out[0].astype(BF16),
             wout=w_out[0].astype(BF16), wff1=w_ff1[0].astype(BF16), wff2=w_ff2[0].astype(BF16))
    prm = dict(mu_shift=mu_shift[0], w0=w0[0], w_decay_up=w_decay_up[0], a0=a0[0], w_a_up=w_a_up[0],
               w_g_up=w_g_up[0], k_k=k_k[0], k_a=k_a[0], r_k=r_k[0].reshape(-1), lnx_w=lnx_w[0],
               lnx_b=lnx_b[0], g1=norm1_g[0].reshape(1, -1), g2=norm2_g[0].reshape(1, -1),
               gf=norm_f_g.reshape(1, -1))
    Bp, Tp, _ = x_prompt.shape
    Bs, Ts, _ = x_sample.shape
    past = cache_k.shape[2]
    zero_state = jnp.zeros((Bp, RW_HEADS, RW_HEAD, RW_HEAD), F32)
    zero_shift = jnp.zeros((Bp, 1, RW_COLS), F32)
    outs_p = _branch(x_prompt, None, zero_state, zero_shift, rel_bias, prm, w,
                     tm_in=512, tm_tail=256, Tq=Q_BLOCK, C=CHUNK, nb_rw=4)
    hist = (jnp.transpose(cache_k[0], (0, 2, 3, 1)), jnp.transpose(cache_v[0], (0, 2, 3, 1)),
            jnp.swapaxes(cache_kidx[0], 1, 2))
    outs_s = _branch(x_sample, hist, state_wkv[0], state_shift[0], rel_bias, prm, w,
                     tm_in=Bs * Ts, tm_tail=256, Tq=Ts, C=Ts, nb_rw=4)
    return (outs_p[0], outs_s[0]) + outs_p[1:] + outs_s[1:]
```

```python
import functools
import math

import jax
import jax.numpy as jnp
import numpy as np
from jax import lax
from jax.experimental import pallas as pl
from jax.experimental.pallas import tpu as pltpu

F32 = jnp.float32
BF16 = jnp.bfloat16

D_MODEL = 1024
CHUNK = 64
N_HEADS = 8
N_KV_HEADS = 2
HEAD_DIM = 64
GROUP = N_HEADS // N_KV_HEADS
ATT_WIDTH = N_HEADS * HEAD_DIM
KV_WIDTH = N_KV_HEADS * HEAD_DIM
N_IDX_HEADS = 8
IDX_DIM = 64
MAX_TOPK = 256
Q_BLOCK = 128
N_BUCKETS = 32
RW_HEADS = 8
RW_HEAD = 64
RW_WIDTH = RW_HEADS * RW_HEAD
DECAY_LORA = 64
AAA_LORA = 64
GATE_LORA = 128
RW_COLS = 3 * RW_WIDTH + DECAY_LORA + AAA_LORA + GATE_LORA
GN_EPS = 64e-5
D_FF = 4 * D_MODEL
NORM_EPS = 1e-6
NEG = -1e30

LANES = 128
N_PAIRS = RW_HEADS // 2
VMEM_LIMIT = 56 * 1024 * 1024

CHUNK_PIECES = 1
SEG_PIECES = 2

_NN, _NT, _TN = (1, 0), (1, 1), (0, 0)


def _split(x, n):
    pieces = []
    rem = x
    for i in range(n):
        p = rem.astype(BF16)
        pieces.append(p)
        if i + 1 < n:
            rem = rem - p.astype(F32)
    return tuple(pieces)


def _mm(a, b, dims=_NN):
    n = max(len(a), len(b))
    terms = [(i, j) for i in range(len(a)) for j in range(len(b)) if i + j < n]
    ca, cb = dims
    lhs = jnp.concatenate([a[i] for i, _ in terms], axis=ca) if len(terms) > 1 else a[0]
    rhs = jnp.concatenate([b[j] for _, j in terms], axis=cb) if len(terms) > 1 else b[0]
    return lax.dot_general(lhs, rhs, (((ca,), (cb,)), ((), ())), preferred_element_type=F32)


def _lane_iota(shape):
    return lax.broadcasted_iota(jnp.int32, shape, len(shape) - 1)


def _row_iota(shape):
    return lax.broadcasted_iota(jnp.int32, shape, len(shape) - 2)


def _bdrows(x, width, heads=2):
    head = (_lane_iota(x.shape) % (heads * width)) // width
    return jnp.concatenate([jnp.where(head == h, x, 0.0) for h in range(heads)], axis=0)


def _rwkv_kernel(x_ref, shift0_ref, h0_ref, mu_ref, w0_ref, a0_ref, kk_ref, ka_ref, rk_ref,
                 lnw_ref, lnb_ref, wlora_ref, wg_ref,
                 out_ref, hT_ref, xbuf, hst, *, C, n_sq, nb):
    c = pl.program_id(1)

    @pl.when(c == 0)
    def _():
        for n in range(nb):
            xbuf[n, 7:8, :] = shift0_ref[n]
        z = jnp.zeros((RW_HEAD, RW_HEAD), F32)
        for n in range(nb):
            for p in range(N_PAIRS):
                blockdiag = jnp.concatenate(
                    [jnp.concatenate([h0_ref[n, 2 * p], z], axis=1),
                     jnp.concatenate([z, h0_ref[n, 2 * p + 1]], axis=1)], axis=0)
                hst[n, p] = blockdiag.T

    W = RW_WIDTH
    R = nb * C
    prevs = []
    for n in range(nb):
        xn = x_ref[n]
        xbuf[n, 8:8 + C, :] = xn
        prevs.append(xbuf[n, 7:7 + C, :])
        xbuf[n, 7:8, :] = xn[C - 1:C, :]
    x = x_ref[...].reshape(R, RW_COLS)
    prev = jnp.concatenate(prevs, axis=0)
    xs = x + (prev - x) * mu_ref[...]
    r = xs[:, 0:W]
    k = xs[:, W:2 * W]
    v = xs[:, 2 * W:3 * W]
    da = xs[:, 3 * W:3 * W + LANES]
    gd = xs[:, 3 * W + LANES:3 * W + 2 * LANES]

    t_in = jnp.where(_lane_iota(da.shape) < DECAY_LORA, jnp.tanh(da), da)
    lo = jnp.dot(t_in.astype(BF16), wlora_ref[...], preferred_element_type=F32)
    ld = -math.exp(-0.5) * jax.nn.sigmoid(w0_ref[...] + lo[:, 0:W])
    a = jax.nn.sigmoid(a0_ref[...] + lo[:, W:2 * W])
    g = jnp.dot(jax.nn.sigmoid(gd).astype(BF16), wg_ref[...], preferred_element_type=F32)

    seg = (_row_iota((LANES, LANES)) // RW_HEAD == _lane_iota((LANES, LANES)) // RW_HEAD).astype(BF16)

    def segsum(t):
        return jnp.concatenate(
            [_mm(_split(t[:, p * LANES:(p + 1) * LANES], SEG_PIECES), (seg,)) for p in range(N_PAIRS)],
            axis=1)

    kk0 = k * kk_ref[...]
    kk = kk0 * lax.rsqrt(jnp.maximum(segsum(kk0 * kk0), 1e-24))
    k2 = k * (1.0 + (a - 1.0) * ka_ref[...])
    bb = kk * a

    ti, si = _row_iota((R, R)), _lane_iota((R, R))
    tri = jnp.logical_and(si <= ti, si // C == ti // C).astype(BF16)
    lp = _mm((tri,), _split(ld, 3))
    lpp = lp - ld
    rows_of = lambda i: jnp.concatenate(
        [jnp.broadcast_to(lp[n * C + i:n * C + i + 1, :], (C, W)) for n in range(nb)], axis=0)
    lpc = rows_of(C - 1)
    mid = rows_of(C // 2 - 1)
    r_abs = r * jnp.exp(lp)
    kap_abs = kk * jnp.exp(lpp)
    r_mid = r * jnp.exp(lp - mid)
    kap_mid = kk * jnp.exp(lpp - mid)
    e_mid = jnp.exp(mid - lp)
    b_mid = bb * e_mid
    k_mid = k2 * e_mid
    e_end = jnp.exp(lpc - lp)
    b_end = bb * e_end
    k_end = k2 * e_end
    p_end = jnp.exp(lpc)

    tt = _row_iota((C, 2 * C))
    ss = _lane_iota((C, 2 * C)) % C
    strict = ss < tt
    incl = ss <= tt
    eye_p = (ss == tt).astype(F32)
    jrow = _row_iota((LANES, LANES))
    jlane = _lane_iota((LANES, LANES))
    same_head = (jrow // RW_HEAD) == (jlane // RW_HEAD)
    eye_j = jrow == jlane

    sp = lambda t: _split(t, CHUNK_PIECES)
    bd = lambda pieces, half: tuple(_bdrows(t, half) for t in pieces)

    chains = [(n, p) for n in range(nb) for p in range(N_PAIRS)]
    blk = lambda t, ch: t[ch[0] * C:(ch[0] + 1) * C, ch[1] * LANES:(ch[1] + 1) * LANES]
    each = lambda f: [f(i, ch) for i, ch in enumerate(chains)]

    A = each(lambda i, ch: _mm(
        sp(jnp.concatenate([blk(kap_mid, ch), blk(r_mid, ch)], axis=0)),
        sp(jnp.concatenate([_bdrows(blk(b_mid, ch), RW_HEAD), _bdrows(blk(k_mid, ch), RW_HEAD)],
                           axis=0)), _NT))
    n_kb = each(lambda i, ch: jnp.where(strict, A[i][0:C, 0:2 * C], 0.0))
    a_kk = each(lambda i, ch: jnp.where(strict, A[i][0:C, 2 * C:4 * C], 0.0))
    a_rb = each(lambda i, ch: jnp.where(incl, A[i][C:2 * C, 0:2 * C], 0.0))
    a_rk = each(lambda i, ch: jnp.where(incl, A[i][C:2 * C, 2 * C:4 * C], 0.0))
    T = each(lambda i, ch: eye_p - n_kb[i])
    pw = each(lambda i, ch: sp(n_kb[i]))
    for _ in range(n_sq):
        pw = each(lambda i, ch: sp(_mm(pw[i], bd(pw[i], C))))
        T = each(lambda i, ch: T[i] + _mm(sp(T[i]), bd(pw[i], C)))
    v_p = each(lambda i, ch: sp(blk(v, ch)))
    av = each(lambda i, ch: _mm(sp(jnp.concatenate([a_kk[i], a_rk[i]], axis=0)),
                                bd(v_p[i], RW_HEAD)))
    wu = each(lambda i, ch: sp(_mm(
        sp(T[i]), bd(sp(jnp.concatenate([blk(kap_abs, ch), av[i][0:C]], axis=1)), RW_HEAD))))
    qy = each(lambda i, ch: jnp.concatenate([blk(r_abs, ch), av[i][C:2 * C]], axis=1)
              - _mm(sp(a_rb[i]), bd(wu[i], RW_HEAD)))
    bw = each(lambda i, ch: _mm(sp(blk(b_end, ch)), wu[i], _TN))
    kv = each(lambda i, ch: _mm(_split(blk(k_end, ch), 2), _split(blk(v, ch), 2), _TN))
    G = each(lambda i, ch: jnp.where(same_head, kv[i] - bw[i][:, LANES:2 * LANES], 0.0))
    M = each(lambda i, ch: jnp.where(same_head, bw[i][:, 0:LANES], 0.0))
    p_col = each(lambda i, ch: jnp.sum(jnp.where(eye_j, blk(p_end, ch)[0:1], 0.0), axis=1,
                                       keepdims=True))
    h0f = each(lambda i, ch: hst[ch[0], ch[1]])
    h0 = each(lambda i, ch: sp(h0f[i]))
    yc_ = each(lambda i, ch: _mm(sp(qy[i][:, 0:LANES]), h0[i]) + qy[i][:, LANES:2 * LANES])
    for i, ch in enumerate(chains):
        hst[ch[0], ch[1]] = p_col[i] * h0f[i] - _mm(sp(M[i]), h0[i]) + G[i]
    y = jnp.concatenate([jnp.concatenate(yc_[n * N_PAIRS:(n + 1) * N_PAIRS], axis=1)
                         for n in range(nb)], axis=0)

    mean = segsum(y) * (1.0 / RW_HEAD)
    yc = y - mean
    var = segsum(yc * yc) * (1.0 / RW_HEAD)
    yn = yc * lax.rsqrt(var + GN_EPS) * lnw_ref[...] + lnb_ref[...]
    bonus = segsum(r * k2 * rk_ref[...]) * v
    out_ref[...] = ((yn + bonus) * g).astype(out_ref.dtype).reshape(nb, C, W)

    @pl.when(c == pl.num_programs(1) - 1)
    def _():
        for n in range(nb):
            for p in range(N_PAIRS):
                blockdiag = hst[n, p].T
                hT_ref[n, 2 * p] = blockdiag[0:RW_HEAD, 0:RW_HEAD]
                hT_ref[n, 2 * p + 1] = blockdiag[RW_HEAD:LANES, RW_HEAD:LANES]


def _rwkv_call(p_rw, shift0, wkv0, prm, C, nb):
    B, T, _ = p_rw.shape
    n_sq = int(math.log2(C)) - 1
    assert 2 ** (n_sq + 1) == C and T % C == 0 and B % nb == 0
    row = lambda a: a.reshape(1, -1).astype(F32)
    vecs = [row(prm['mu_shift']), row(prm['w0']), row(prm['a0']), row(prm['k_k']), row(prm['k_a']),
            row(prm['r_k']), row(prm['lnx_w']), row(prm['lnx_b'])]
    wlora = jnp.zeros((LANES, 2 * RW_WIDTH), F32)
    wlora = wlora.at[0:DECAY_LORA, 0:RW_WIDTH].set(prm['w_decay_up'])
    wlora = wlora.at[DECAY_LORA:, RW_WIDTH:].set(prm['w_a_up']).astype(BF16)
    wg = prm['w_g_up'].astype(BF16)
    const = lambda shape: pl.BlockSpec(shape, lambda b, c: (0,) * len(shape))
    kern = functools.partial(_rwkv_kernel, C=C, n_sq=n_sq, nb=nb)
    return pl.pallas_call(
        kern,
        grid=(B // nb, T // C),
        in_specs=[pl.BlockSpec((nb, C, RW_COLS), lambda b, c: (b, c, 0)),
                  pl.BlockSpec((nb, 1, RW_COLS), lambda b, c: (b, 0, 0)),
                  pl.BlockSpec((nb, RW_HEADS, RW_HEAD, RW_HEAD), lambda b, c: (b, 0, 0, 0)),
                  const((1, RW_COLS))] + [const((1, RW_WIDTH))] * 7
                 + [const((LANES, 2 * RW_WIDTH)), const((GATE_LORA, RW_WIDTH))],
        out_specs=[pl.BlockSpec((nb, C, RW_WIDTH), lambda b, c: (b, c, 0)),
                   pl.BlockSpec((nb, RW_HEADS, RW_HEAD, RW_HEAD), lambda b, c: (b, 0, 0, 0))],
        out_shape=[jax.ShapeDtypeStruct((B, T, RW_WIDTH), BF16),
                   jax.ShapeDtypeStruct((B, RW_HEADS, RW_HEAD, RW_HEAD), F32)],
        scratch_shapes=[pltpu.VMEM((nb, 8 + C, RW_COLS), F32),
                        pltpu.VMEM((nb, N_PAIRS, LANES, LANES), F32)],
        compiler_params=pltpu.CompilerParams(dimension_semantics=("arbitrary", "arbitrary"),
                                             vmem_limit_bytes=VMEM_LIMIT),
        name="rwkv7_chunked",
    )(p_rw, shift0, wkv0, *vecs, wlora, wg)


_INT_MIN = -2 ** 31
_NEG_KEY = int(np.float32(NEG).view(np.int32)) ^ 0x7FFFFFFF
_T5_STEPS = ((91, 14), (64, 13), (46, 12), (32, 11), (23, 10), (16, 9), (12, 8), (8, 7), (7, 6),
             (6, 5), (5, 4), (4, 3), (3, 2), (2, 1), (1, 0))
_T5_FAR = 15
_KEY_TILE = 256


def _half_masked(x, half):
    low = (_lane_iota(x.shape) % LANES) < HEAD_DIM
    keep = low if half == 0 else jnp.logical_not(low)
    return jnp.where(keep, x, jnp.zeros_like(x))


def _dup_half(x, half):
    xr = pltpu.roll(x, HEAD_DIM, axis=1)
    low = _lane_iota(x.shape) < HEAD_DIM
    return jnp.where(low, x, xr) if half == 0 else jnp.where(low, xr, x)


def _dot_nt(a, b):
    return lax.dot_general(a, b, (((1,), (1,)), ((), ())), preferred_element_type=F32)


def _dsa_kernel(*refs, Tq, P, Tn, Lp, topk, n_cls):
    if P > 0:
        (q_ref, qi_ref, wi_ref, hk_ref, hv_ref, hki_ref, nk_ref, nv_ref, nki_ref, rb_ref,
         att_ref, kd, vd, kid, lg, tbl, sbias) = refs
    else:
        (q_ref, qi_ref, wi_ref, nk_ref, nv_ref, nki_ref, rb_ref,
         att_ref, kd, vd, kid, lg, tbl, sbias) = refs
    b = pl.program_id(0)
    i = pl.program_id(1)
    L = P + Tn

    @pl.when(jnp.logical_and(b == 0, i == 0))
    def _():
        shape = (Tq, 2 * LANES)
        rel = _lane_iota(shape) - LANES - _row_iota(shape)
        n = jnp.abs(rel)
        for h in range(N_HEADS):
            far = rb_ref[_T5_FAR, h]
            neg = jnp.full(shape, far, F32)
            pos = jnp.full(shape, rb_ref[_T5_FAR + N_BUCKETS // 2, h], F32)
            for bound, bucket in _T5_STEPS:
                neg = jnp.where(n < bound, rb_ref[bucket, h], neg)
                pos = jnp.where(n < bound, rb_ref[bucket + N_BUCKETS // 2, h], pos)
            tbl[h, :, 0:2 * LANES] = jnp.where(rel > 0, pos, neg) - far
            tbl[h, :, 2 * LANES:3 * LANES] = jnp.zeros((Tq, LANES), F32)

    @pl.when(i == 0)
    def _():
        if P > 0:
            def put(dst, x, c0):
                xb = x.astype(BF16)
                dst[0:HEAD_DIM, c0:c0 + x.shape[1]] = xb
                dst[HEAD_DIM:LANES, c0:c0 + x.shape[1]] = xb

            zt = jnp.zeros((LANES, Lp - P), BF16)
            for j in range(N_KV_HEADS):
                put(kd.at[j], hk_ref[0, j], 0)
                kd[j, :, P:Lp] = zt
                put(kd.at[j], nk_ref[0, j * HEAD_DIM:(j + 1) * HEAD_DIM, :], P)
                for u in range(2):
                    vd[2 * j + u] = jnp.ones((LANES, Lp), BF16)
                    r0 = u * HEAD_DIM
                    vd[2 * j + u, r0:r0 + HEAD_DIM, 0:P] = hv_ref[0, j].astype(BF16)
                    vd[2 * j + u, r0:r0 + HEAD_DIM, P:P + Tn] = (
                        nv_ref[0, j * HEAD_DIM:(j + 1) * HEAD_DIM, :].astype(BF16))
            put(kid, hki_ref[0], 0)
            kid[:, P:Lp] = zt
            put(kid, nki_ref[0], P)
        else:
            kf, vf, kif = nk_ref[0], nv_ref[0], nki_ref[0]
            low = _lane_iota(vf.shape) < HEAD_DIM
            for j in range(N_KV_HEADS):
                kd[j, 0:Tn, :] = _dup_half(kf, j).astype(BF16)
                vdup = _dup_half(vf, j)
                vd[2 * j, 0:Tn, :] = jnp.where(low, vdup, 1.0).astype(BF16)
                vd[2 * j + 1, 0:Tn, :] = jnp.where(low, 1.0, vdup).astype(BF16)
            kid[0:Tn, :] = jnp.concatenate([kif, kif], axis=1).astype(BF16)

    stride = Lp // n_cls
    k_hi = P + (i + 1) * Tq
    cls = (k_hi - 1) // stride
    for c in range(n_cls):
        pl.when(cls == c)(functools.partial(
            _dsa_block, q_ref, qi_ref, wi_ref, att_ref, kd, vd, kid, lg, tbl, sbias,
            i=i, Tq=Tq, P=P, L=L, Lk=(c + 1) * stride, topk=topk))


def _dsa_block(q_ref, qi_ref, wi_ref, att_ref, kd, vd, kid, lg, tbl, sbias, *, i, Tq, P, L, Lk, topk):
    q = q_ref[0]
    qi = qi_ref[0]
    key_major = Tq % LANES == 0
    kax = 0 if key_major else 1
    shape = (Lk, Tq) if key_major else (Tq, Lk)
    qvec = (1, Tq) if key_major else (Tq, 1)
    wi = wi_ref[0]

    lanes_keys = P > 0
    kidv = kid[:, 0:Lk] if lanes_keys else kid[0:Lk, :]
    head_q = lambda h: _half_masked(qi[:, (h // 2) * LANES:(h // 2 + 1) * LANES], h % 2)
    score = jnp.zeros(shape, F32)
    if key_major:
        pair_q = [jnp.concatenate([head_q(2 * pr), head_q(2 * pr + 1)], axis=0)
                  for pr in range(N_IDX_HEADS // 2)]
        tiles = []
        tile = _KEY_TILE if Lk % _KEY_TILE == 0 else LANES
        for kt in range(Lk // tile):
            kid_t = kidv[kt * tile:(kt + 1) * tile]
            acc = jnp.zeros((tile, Tq), F32)
            for pr in range(N_IDX_HEADS // 2):
                s2 = _dot_nt(kid_t, pair_q[pr])
                for u in range(2):
                    h = 2 * pr + u
                    acc = acc + wi[h:h + 1, :] * jnp.maximum(s2[:, u * Tq:(u + 1) * Tq], 0.0)
            tiles.append(acc)
        score = jnp.concatenate(tiles, axis=0)
    else:
        lhs_all = jnp.concatenate([head_q(h) for h in range(N_IDX_HEADS)], axis=0)
        s_all = (jnp.dot(lhs_all, kidv, preferred_element_type=F32) if lanes_keys
                 else _dot_nt(lhs_all, kidv))
        for h in range(N_IDX_HEADS):
            score = score + wi[:, h:h + 1] * jnp.maximum(s_all[h * Tq:(h + 1) * Tq], 0.0)

    s_pos = lax.broadcasted_iota(jnp.int32, shape, kax)
    q_pos = P + i * Tq + lax.broadcasted_iota(jnp.int32, shape, 1 - kax)
    adm = jnp.logical_and(s_pos // CHUNK <= q_pos // CHUNK, s_pos < L)
    score = jnp.where(adm, score, NEG)

    def count(mask):
        ones = jnp.where(mask, 1.0, 0.0)
        if key_major:
            ones = jnp.sum(ones.reshape(Lk // LANES, LANES, Tq), axis=0)
        return jnp.sum(ones, axis=kax, keepdims=True)

    def key_to_float(key):
        return pltpu.bitcast(jnp.where(key < 0, key ^ jnp.int32(0x7FFFFFFF), key), F32)

    kf = float(topk)
    c0 = count(score >= 0.0)
    state = (jnp.where(c0 >= kf, jnp.int32(0), jnp.int32(_INT_MIN)), jnp.where(c0 >= kf, c0, float(Lk)))

    def try_cand(cand, ans, cnt):
        c = count(score >= key_to_float(cand))
        ok = c >= kf
        return jnp.where(ok, cand, ans), jnp.where(ok, c, cnt)

    def bit_step(it, st):
        return try_cand(st[0] | (jnp.int32(1) << (30 - it)), *st)

    def two_bit_step(it, st):
        shift = 28 - 2 * it
        best = st
        for m in (1, 2, 3):
            best = try_cand(st[0] | (jnp.int32(m) << shift), *best)
        return best

    def stacked_q(j):
        return jnp.concatenate(
            [_half_masked(q[:, ((j * GROUP + g) // 2) * LANES:((j * GROUP + g) // 2 + 1) * LANES], g % 2)
             for g in range(GROUP)], axis=0)

    def logits_rows(j, lhs_rows, rsl):
        lg[j, rsl, 0:Lk] = (jnp.dot(lhs_rows, kd[j, :, 0:Lk], preferred_element_type=F32) if lanes_keys
                            else _dot_nt(lhs_rows, kd[j, 0:Lk, :]))

    if key_major:
        thr_key, n_ge = lax.fori_loop(0, 31, bit_step, state)
    else:
        thr_key, n_ge = lax.fori_loop(0, 15, two_bit_step, bit_step(0, state))
    thr_key = jnp.maximum(thr_key, jnp.int32(_NEG_KEY))
    thr = key_to_float(thr_key)
    gt = score > thr
    eq = score == thr
    tied = jnp.logical_and(n_ge > kf, thr > NEG)
    any_tied = jnp.max(jnp.where(tied, 1.0, 0.0)) > 0.0

    n_idx_bits = (Lk - 1).bit_length()

    def idx_search():
        need = kf - count(gt)

        def idx_step(it, x):
            cand = x | (jnp.int32(1) << (n_idx_bits - 1 - it))
            return jnp.where(count(jnp.logical_and(eq, s_pos < cand)) < need, cand, x)
        return lax.fori_loop(0, n_idx_bits, idx_step, jnp.zeros(qvec, jnp.int32))

    last = lax.cond(any_tied, idx_search, lambda: jnp.full(qvec, Lk, jnp.int32))
    sel = jnp.logical_and(jnp.logical_or(gt, jnp.logical_and(eq, s_pos <= last)), adm)
    sb = jnp.where(sel, 0.0, NEG)
    sbias[:, 0:Lk] = sb.T if key_major else sb

    blk_end = (P + i * Tq) // LANES + 1
    start = pl.multiple_of(jnp.maximum(blk_end - 2, 0) * LANES, LANES)
    t_off = pl.multiple_of(jnp.maximum(2 - blk_end, 0) * LANES, LANES)
    for j in range(N_KV_HEADS):
        logits_rows(j, stacked_q(j), slice(0, GROUP * Tq))
    for j in range(N_KV_HEADS):
        for g in range(GROUP):
            r = slice(g * Tq, (g + 1) * Tq)
            if Lk < 2 * LANES:
                lg[j, r, 0:LANES] = lg[j, r, 0:LANES] + tbl[j * GROUP + g, :, LANES:2 * LANES]
            else:
                lg[j, r, pl.ds(start, 2 * LANES)] = (lg[j, r, pl.ds(start, 2 * LANES)]
                                                     + tbl[j * GROUP + g, :, pl.ds(t_off, 2 * LANES)])
    for j in range(N_KV_HEADS):
        sb = sbias[:, 0:Lk]
        ps = []
        for g in range(GROUP):
            lgv = lg[j, g * Tq:(g + 1) * Tq, 0:Lk]
            m = jnp.max(lgv + sb, axis=1, keepdims=True)
            ps.append(jnp.exp((lgv - m) + sb).astype(BF16))
        pv = lambda p, u: (_dot_nt(p, vd[2 * j + u, :, 0:Lk]) if lanes_keys
                           else jnp.dot(p, vd[2 * j + u, 0:Lk, :], preferred_element_type=F32))
        if key_major:
            os_ = [pv(ps[g], g % 2) for g in range(GROUP)]
        else:
            stacked = [pv(jnp.concatenate(ps[u::2], axis=0), u) for u in range(2)]
            os_ = [stacked[g % 2][(g // 2) * Tq:(g // 2 + 1) * Tq] for g in range(GROUP)]
        low = _lane_iota((Tq, LANES)) < HEAD_DIM
        for pp in range(GROUP // 2):
            c0 = (j * GROUP // 2 + pp) * LANES
            o_even, o_odd = os_[2 * pp], os_[2 * pp + 1]
            num = jnp.where(low, o_even, o_odd)
            den = jnp.where(low, pltpu.roll(o_even, HEAD_DIM, axis=1), pltpu.roll(o_odd, HEAD_DIM, axis=1))
            att_ref[0, :, c0:c0 + LANES] = (num / den).astype(att_ref.dtype)


def _dsa_call(q, qi, wi, hist, new, rel_bias, Tq):
    B, T, _ = q.shape
    P = 0 if hist is None else hist[0].shape[-1]
    Tn = new[0].shape[2 if P > 0 else 1]
    L = P + Tn
    Lp = -(-L // LANES) * LANES
    topk = min(MAX_TOPK, L // 4)
    assert T % Tq == 0 and Tn == T and P % LANES == 0 and (Tq % LANES == 0 or T == Tq)
    assert (P == 0 and Lp == L) or (P > 0 and Tq % LANES != 0)
    qspec = lambda w: pl.BlockSpec((1, Tq, w), lambda b, i: (b, i, 0))
    full = lambda *dims: pl.BlockSpec((1,) + dims, lambda b, i: (b,) + (0,) * len(dims))
    if Tq % LANES == 0:
        wi = jnp.swapaxes(wi, 1, 2)
        wi_spec = pl.BlockSpec((1, N_IDX_HEADS, Tq), lambda b, i: (b, 0, i))
    else:
        wi_spec = qspec(N_IDX_HEADS)
    in_specs = [qspec(ATT_WIDTH), qspec(N_IDX_HEADS * IDX_DIM), wi_spec]
    args = [q, qi, wi]
    if P > 0:
        in_specs += [full(N_KV_HEADS, HEAD_DIM, P), full(N_KV_HEADS, HEAD_DIM, P), full(IDX_DIM, P),
                     full(KV_WIDTH, Tn), full(KV_WIDTH, Tn), full(IDX_DIM, Tn)]
        args += list(hist)
        kv_scratch, ki_scratch = (N_KV_HEADS, LANES, Lp), (LANES, Lp)
    else:
        in_specs += [full(Tn, KV_WIDTH), full(Tn, KV_WIDTH), full(Tn, IDX_DIM)]
        kv_scratch, ki_scratch = (N_KV_HEADS, Lp, LANES), (Lp, LANES)
    in_specs.append(pl.BlockSpec(memory_space=pltpu.SMEM))
    args += list(new) + [rel_bias.astype(F32)]
    n_cls = Lp // 256 if (P == 0 and Lp % 256 == 0) else 1
    kern = functools.partial(_dsa_kernel, Tq=Tq, P=P, Tn=Tn, Lp=Lp, topk=topk, n_cls=n_cls)
    return pl.pallas_call(
        kern,
        grid=(B, T // Tq),
        in_specs=in_specs,
        out_specs=pl.BlockSpec((1, Tq, ATT_WIDTH), lambda b, i: (b, i, 0)),
        out_shape=jax.ShapeDtypeStruct((B, T, ATT_WIDTH), BF16),
        scratch_shapes=[pltpu.VMEM(kv_scratch, BF16),
                        pltpu.VMEM((2 * N_KV_HEADS,) + kv_scratch[1:], BF16),
                        pltpu.VMEM(ki_scratch, BF16),
                        pltpu.VMEM((N_KV_HEADS, GROUP * Tq, Lp), F32),
                        pltpu.VMEM((N_HEADS, Tq, 3 * LANES), F32),
                        pltpu.VMEM((Tq, Lp), F32)],
        compiler_params=pltpu.CompilerParams(dimension_semantics=("arbitrary", "arbitrary"),
                                             vmem_limit_bytes=VMEM_LIMIT),
        name="dsa_attention",
    )(*args)


ATT_COLS = 2 * ATT_WIDTH + 2 * KV_WIDTH + LANES
_C_Q, _C_QI, _C_K, _C_V, _C_KI = 0, 512, 1024, 1152, 1280
_C_WI = _C_KI + IDX_DIM


def _rms(x, g):
    return x * lax.rsqrt(jnp.mean(x * x, axis=-1, keepdims=True) + NORM_EPS) * g


def _inproj_kernel(x_ref, g_ref, wa_ref, wr_ref, q_ref, qi_ref, k_ref, v_ref, ki_ref, wi_ref, prw_ref,
                   *t_refs):
    h = _rms(x_ref[...], g_ref[...]).astype(BF16)
    pa = jnp.dot(h, wa_ref[...], preferred_element_type=F32)
    q_ref[...] = (pa[:, _C_Q:_C_Q + ATT_WIDTH] * (HEAD_DIM ** -0.5)).astype(BF16)
    qi_ref[...] = (pa[:, _C_QI:_C_QI + ATT_WIDTH] * (IDX_DIM ** -0.5)).astype(BF16)
    k_ref[...] = pa[:, _C_K:_C_K + KV_WIDTH]
    v_ref[...] = pa[:, _C_V:_C_V + KV_WIDTH]
    ki_ref[...] = pa[:, _C_KI:_C_KI + IDX_DIM]
    wi_ref[...] = pa[:, _C_WI:_C_WI + N_IDX_HEADS] * (N_IDX_HEADS ** -0.5)
    prw_ref[...] = jnp.dot(h, wr_ref[...], preferred_element_type=F32)
    if t_refs:
        kT_ref, vT_ref, kiT_ref = t_refs
        kT_ref[0] = pa[:, _C_K:_C_K + KV_WIDTH].T
        vT_ref[0] = pa[:, _C_V:_C_V + KV_WIDTH].T
        kiT_ref[0] = pa[:, _C_KI:_C_KI + LANES].T[0:IDX_DIM]


def _const_spec(shape):
    return pl.BlockSpec(shape, lambda i: (0,) * len(shape), pipeline_mode=pl.Buffered(1))


def _inproj_call(x2d, g, wa, wr, tm, t_per_batch=None):
    N = x2d.shape[0]
    assert N % tm == 0
    rows = lambda w: pl.BlockSpec((tm, w), lambda i: (i, 0))
    widths = [ATT_WIDTH, ATT_WIDTH, KV_WIDTH, KV_WIDTH, IDX_DIM, N_IDX_HEADS, RW_COLS]
    dtypes = [BF16, BF16, F32, F32, F32, F32, F32]
    out_specs = [rows(w) for w in widths]
    out_shape = [jax.ShapeDtypeStruct((N, w), d) for w, d in zip(widths, dtypes)]
    if t_per_batch is not None:
        assert t_per_batch % tm == 0
        tpb = t_per_batch // tm
        for feat in (KV_WIDTH, KV_WIDTH, IDX_DIM):
            out_specs.append(pl.BlockSpec((1, feat, tm), lambda i: (i // tpb, 0, i % tpb)))
            out_shape.append(jax.ShapeDtypeStruct((N // t_per_batch, feat, t_per_batch), F32))
    return pl.pallas_call(
        _inproj_kernel,
        grid=(N // tm,),
        in_specs=[rows(D_MODEL), _const_spec((1, D_MODEL)), _const_spec((D_MODEL, ATT_COLS)),
                  _const_spec((D_MODEL, RW_COLS))],
        out_specs=out_specs,
        out_shape=out_shape,
        compiler_params=pltpu.CompilerParams(dimension_semantics=("arbitrary",),
                                             vmem_limit_bytes=VMEM_LIMIT),
        name="rmsnorm_inproj",
    )(x2d, g, wa, wr)


def _tail_kernel(x_ref, att_ref, rw_ref, g1_ref, wgate_ref, wao_ref, wro_ref, wout_ref,
                 g2_ref, wff1_ref, wff2_ref, gf_ref, y_ref):
    x = x_ref[...]
    h = _rms(x, g1_ref[...]).astype(BF16)
    gates = jax.nn.sigmoid(jnp.dot(h, wgate_ref[...], preferred_element_type=F32))
    z_a = jnp.dot(att_ref[...], wao_ref[...], preferred_element_type=F32)
    z_b = jnp.dot(rw_ref[...], wro_ref[...], preferred_element_type=F32)
    mixed = gates[:, 0:D_MODEL] * z_a + gates[:, D_MODEL:2 * D_MODEL] * z_b
    x1 = x + jnp.dot(mixed.astype(BF16), wout_ref[...], preferred_element_type=F32)
    h2 = _rms(x1, g2_ref[...]).astype(BF16)
    f = jnp.maximum(jnp.dot(h2, wff1_ref[...], preferred_element_type=F32), 0.0)
    x2 = x1 + jnp.dot((f * f).astype(BF16), wff2_ref[...], preferred_element_type=F32)
    y_ref[...] = _rms(x2, gf_ref[...])


def _tail_call(x2d, att, rw, g1, wgate, wao, wro, wout, g2, wff1, wff2, gf, tm):
    N = x2d.shape[0]
    assert N % tm == 0
    rows = lambda w: pl.BlockSpec((tm, w), lambda i: (i, 0))
    return pl.pallas_call(
        _tail_kernel,
        grid=(N // tm,),
        in_specs=[rows(D_MODEL), rows(ATT_WIDTH), rows(RW_WIDTH), _const_spec((1, D_MODEL)),
                  _const_spec((D_MODEL, 2 * D_MODEL)), _const_spec((ATT_WIDTH, D_MODEL)),
                  _const_spec((RW_WIDTH, D_MODEL)), _const_spec((D_MODEL, D_MODEL)),
                  _const_spec((1, D_MODEL)), _const_spec((D_MODEL, D_FF)), _const_spec((D_FF, D_MODEL)),
                  _const_spec((1, D_MODEL))],
        out_specs=rows(D_MODEL),
        out_shape=jax.ShapeDtypeStruct((N, D_MODEL), F32),
        compiler_params=pltpu.CompilerParams(dimension_semantics=("arbitrary",),
                                             vmem_limit_bytes=VMEM_LIMIT),
        name="merge_out_mlp",
    )(x2d, att, rw, g1, wgate, wao, wro, wout, g2, wff1, wff2, gf)


_IN_SPLITS = (ATT_WIDTH, KV_WIDTH, KV_WIDTH, N_IDX_HEADS * IDX_DIM, IDX_DIM, N_IDX_HEADS,
              RW_COLS, D_MODEL, D_MODEL)


def _split_w_in(w_in):
    offs = [0]
    for s in _IN_SPLITS:
        offs.append(offs[-1] + s)
    col = lambda n: w_in[:, offs[n]:offs[n + 1]]
    pad = jnp.zeros((D_MODEL, ATT_COLS - _C_WI - N_IDX_HEADS), w_in.dtype)
    wa = jnp.concatenate([col(0), col(3), col(1), col(2), col(4), col(5), pad], axis=1).astype(BF16)
    wr = col(6).astype(BF16)
    wgate = jnp.concatenate([col(7), col(8)], axis=1).astype(BF16)
    return wa, wr, wgate


def _branch(x, hist, wkv0, shift0, rel_bias, prm, w, tm_in, tm_tail, Tq, C, nb_rw):
    B, T, _ = x.shape
    x2d = x.reshape(B * T, D_MODEL)
    r3 = lambda a: a.reshape(B, T, a.shape[-1])
    heads_last = lambda aT: jnp.transpose(aT.reshape(B, N_KV_HEADS, HEAD_DIM, T), (0, 3, 1, 2))[None]
    if hist is None:
        q, qi, k, v, ki, wi, p_rw, kT, vT, kiT = _inproj_call(x2d, prm['g1'], w['wa'], w['wr'], tm_in, T)
        new = (r3(k), r3(v), r3(ki))
        new_k, new_v, new_ki = heads_last(kT), heads_last(vT), jnp.swapaxes(kiT, 1, 2)[None]
    else:
        q, qi, k, v, ki, wi, p_rw = _inproj_call(x2d, prm['g1'], w['wa'], w['wr'], tm_in)
        new = tuple(jnp.swapaxes(r3(a), 1, 2) for a in (k, v, ki))
        new_k = k.reshape(1, B, T, N_KV_HEADS, HEAD_DIM)
        new_v = v.reshape(1, B, T, N_KV_HEADS, HEAD_DIM)
        new_ki = ki.reshape(1, B, T, IDX_DIM)
    rw, wkv_new = _rwkv_call(r3(p_rw), shift0, wkv0, prm, C, nb_rw)
    att = _dsa_call(r3(q), r3(qi), r3(wi), hist, new, rel_bias, Tq)
    y = _tail_call(x2d, att.reshape(B * T, ATT_WIDTH), rw.reshape(B * T, RW_WIDTH), prm['g1'],
                   w['wgate'], w['wao'], w['wro'], w['wout'], prm['g2'], w['wff1'], w['wff2'],
                   prm['gf'], tm_tail)
    wkv = wkv_new[None]
    shift = r3(p_rw)[:, T - 1:T, :][None]
    return y.reshape(B, T, D_MODEL), new_k, new_v, new_ki, wkv, shift


def kernel(x_prompt, x_sample, cache_k, cache_v, cache_kidx, state_wkv, state_shift, rel_bias,
           norm1_g, w_in, w_att_out, mu_shift, w0, w_decay_up, a0, w_a_up, w_g_up, k_k, k_a, r_k,
           lnx_w, lnx_b, w_rw_out, w_out, norm2_g, w_ff1, w_ff2, norm_f_g):
    assert w_in.shape[0] == 1, "single-layer step"
    wa, wr, wgate = _split_w_in(w_in[0])
    w = dict(wa=wa, wr=wr, wgate=wgate, wao=w_att_out[0].astype(BF16), wro=w_rw_out[0].astype(BF16),
             wout=w_out[0].astype(BF16), wff1=w_ff1[0].astype(BF16), wff2=w_ff2[0].astype(BF16))
    prm = dict(mu_shift=mu_shift[0], w0=w0[0], w_decay_up=w_decay_up[0], a0=a0[0], w_a_up=w_a_up[0],
               w_g_up=w_g_up[0], k_k=k_k[0], k_a=k_a[0], r_k=r_k[0].reshape(-1), lnx_w=lnx_w[0],
               lnx_b=lnx_b[0], g1=norm1_g[0].reshape(1, -1), g2=norm2_g[0].reshape(1, -1),
               gf=norm_f_g.reshape(1, -1))
    Bp, Tp, _ = x_prompt.shape
    Bs, Ts, _ = x_sample.shape
    past = cache_k.shape[2]
    zero_state = jnp.zeros((Bp, RW_HEADS, RW_HEAD, RW_HEAD), F32)
    zero_shift = jnp.zeros((Bp, 1, RW_COLS), F32)
    outs_p = _branch(x_prompt, None, zero_state, zero_shift, rel_bias, prm, w,
                     tm_in=512, tm_tail=256, Tq=Q_BLOCK, C=CHUNK, nb_rw=4)
    hist = (jnp.transpose(cache_k[0], (0, 2, 3, 1)), jnp.transpose(cache_v[0], (0, 2, 3, 1)),
            jnp.swapaxes(cache_kidx[0], 1, 2))
    outs_s = _branch(x_sample, hist, state_wkv[0], state_shift[0], rel_bias, prm, w,
                     tm_in=Bs * Ts, tm_tail=256, Tq=Ts, C=Ts, nb_rw=4)
    return (outs_p[0], outs_s[0]) + outs_p[1:] + outs_s[1:]
```

```python
import functools
import math

import jax
import jax.numpy as jnp
import numpy as np
from jax import lax
from jax.experimental import pallas as pl
from jax.experimental.pallas import tpu as pltpu

F32 = jnp.float32
BF16 = jnp.bfloat16

D_MODEL = 1024
CHUNK = 64
N_HEADS = 8
N_KV_HEADS = 2
HEAD_DIM = 64
GROUP = N_HEADS // N_KV_HEADS
ATT_WIDTH = N_HEADS * HEAD_DIM
KV_WIDTH = N_KV_HEADS * HEAD_DIM
N_IDX_HEADS = 8
IDX_DIM = 64
MAX_TOPK = 256
Q_BLOCK = 128
N_BUCKETS = 32
RW_HEADS = 8
RW_HEAD = 64
RW_WIDTH = RW_HEADS * RW_HEAD
DECAY_LORA = 64
AAA_LORA = 64
GATE_LORA = 128
RW_COLS = 3 * RW_WIDTH + DECAY_LORA + AAA_LORA + GATE_LORA
GN_EPS = 64e-5
D_FF = 4 * D_MODEL
NORM_EPS = 1e-6
NEG = -1e30

LANES = 128
N_PAIRS = RW_HEADS // 2
VMEM_LIMIT = 56 * 1024 * 1024

CHUNK_PIECES = 1
SEG_PIECES = 2

_NN, _NT, _TN = (1, 0), (1, 1), (0, 0)


def _split(x, n):
    pieces = []
    rem = x
    for i in range(n):
        p = rem.astype(BF16)
        pieces.append(p)
        if i + 1 < n:
            rem = rem - p.astype(F32)
    return tuple(pieces)


def _mm(a, b, dims=_NN):
    n = max(len(a), len(b))
    terms = [(i, j) for i in range(len(a)) for j in range(len(b)) if i + j < n]
    ca, cb = dims
    lhs = jnp.concatenate([a[i] for i, _ in terms], axis=ca) if len(terms) > 1 else a[0]
    rhs = jnp.concatenate([b[j] for _, j in terms], axis=cb) if len(terms) > 1 else b[0]
    return lax.dot_general(lhs, rhs, (((ca,), (cb,)), ((), ())), preferred_element_type=F32)


def _lane_iota(shape):
    return lax.broadcasted_iota(jnp.int32, shape, len(shape) - 1)


def _row_iota(shape):
    return lax.broadcasted_iota(jnp.int32, shape, len(shape) - 2)


def _bdrows(x, width, heads=2):
    head = (_lane_iota(x.shape) % (heads * width)) // width
    return jnp.concatenate([jnp.where(head == h, x, 0.0) for h in range(heads)], axis=0)


def _rwkv_kernel(x_ref, shift0_ref, h0_ref, mu_ref, w0_ref, a0_ref, kk_ref, ka_ref, rk_ref,
                 lnw_ref, lnb_ref, wlora_ref, wg_ref,
                 out_ref, hT_ref, xbuf, hst, *, C, n_sq, nb):
    c = pl.program_id(1)

    @pl.when(c == 0)
    def _():
        for n in range(nb):
            xbuf[n, 7:8, :] = shift0_ref[n]
        z = jnp.zeros((RW_HEAD, RW_HEAD), F32)
        for n in range(nb):
            for p in range(N_PAIRS):
                blockdiag = jnp.concatenate(
                    [jnp.concatenate([h0_ref[n, 2 * p], z], axis=1),
                     jnp.concatenate([z, h0_ref[n, 2 * p + 1]], axis=1)], axis=0)
                hst[n, p] = blockdiag.T

    W = RW_WIDTH
    R = nb * C
    prevs = []
    for n in range(nb):
        xn = x_ref[n]
        xbuf[n, 8:8 + C, :] = xn
        prevs.append(xbuf[n, 7:7 + C, :])
        xbuf[n, 7:8, :] = xn[C - 1:C, :]
    x = x_ref[...].reshape(R, RW_COLS)
    prev = jnp.concatenate(prevs, axis=0)
    xs = x + (prev - x) * mu_ref[...]
    r = xs[:, 0:W]
    k = xs[:, W:2 * W]
    v = xs[:, 2 * W:3 * W]
    da = xs[:, 3 * W:3 * W + LANES]
    gd = xs[:, 3 * W + LANES:3 * W + 2 * LANES]

    t_in = jnp.where(_lane_iota(da.shape) < DECAY_LORA, jnp.tanh(da), da)
    lo = jnp.dot(t_in.astype(BF16), wlora_ref[...], preferred_element_type=F32)
    ld = -math.exp(-0.5) * jax.nn.sigmoid(w0_ref[...] + lo[:, 0:W])
    a = jax.nn.sigmoid(a0_ref[...] + lo[:, W:2 * W])
    g = jnp.dot(jax.nn.sigmoid(gd).astype(BF16), wg_ref[...], preferred_element_type=F32)

    seg = (_row_iota((LANES, LANES)) // RW_HEAD == _lane_iota((LANES, LANES)) // RW_HEAD).astype(BF16)

    def segsum(t, pieces=SEG_PIECES):
        return jnp.concatenate(
            [_mm(_split(t[:, p * LANES:(p + 1) * LANES], pieces), (seg,)) for p in range(N_PAIRS)],
            axis=1)

    kk0 = k * kk_ref[...]
    kk = kk0 * lax.rsqrt(jnp.maximum(segsum(kk0 * kk0), 1e-24))
    k2 = k * (1.0 + (a - 1.0) * ka_ref[...])
    bb = kk * a

    ti, si = _row_iota((R, R)), _lane_iota((R, R))
    tri = jnp.logical_and(si <= ti, si // C == ti // C).astype(BF16)
    lp = _mm((tri,), _split(ld, 3))
    lpp = lp - ld
    rows_of = lambda i: jnp.concatenate(
        [jnp.broadcast_to(lp[n * C + i:n * C + i + 1, :], (C, W)) for n in range(nb)], axis=0)
    lpc = rows_of(C - 1)
    mid = rows_of(C // 2 - 1)
    r_abs = r * jnp.exp(lp)
    kap_abs = kk * jnp.exp(lpp)
    r_mid = r * jnp.exp(lp - mid)
    kap_mid = kk * jnp.exp(lpp - mid)
    e_mid = jnp.exp(mid - lp)
    b_mid = bb * e_mid
    k_mid = k2 * e_mid
    e_end = jnp.exp(lpc - lp)
    b_end = bb * e_end
    k_end = k2 * e_end
    p_end = jnp.exp(lpc)

    tt = _row_iota((C, 2 * C))
    ss = _lane_iota((C, 2 * C)) % C
    strict = ss < tt
    incl = ss <= tt
    eye_p = (ss == tt).astype(F32)
    jrow = _row_iota((LANES, LANES))
    jlane = _lane_iota((LANES, LANES))
    same_head = (jrow // RW_HEAD) == (jlane // RW_HEAD)
    eye_j = jrow == jlane

    sp = lambda t: _split(t, CHUNK_PIECES)
    bd = lambda pieces, half: tuple(_bdrows(t, half) for t in pieces)

    chains = [(n, p) for n in range(nb) for p in range(N_PAIRS)]
    blk = lambda t, ch: t[ch[0] * C:(ch[0] + 1) * C, ch[1] * LANES:(ch[1] + 1) * LANES]
    each = lambda f: [f(i, ch) for i, ch in enumerate(chains)]

    A = each(lambda i, ch: _mm(
        sp(jnp.concatenate([blk(kap_mid, ch), blk(r_mid, ch)], axis=0)),
        sp(jnp.concatenate([_bdrows(blk(b_mid, ch), RW_HEAD), _bdrows(blk(k_mid, ch), RW_HEAD)],
                           axis=0)), _NT))
    n_kb = each(lambda i, ch: jnp.where(strict, A[i][0:C, 0:2 * C], 0.0))
    a_kk = each(lambda i, ch: jnp.where(strict, A[i][0:C, 2 * C:4 * C], 0.0))
    a_rb = each(lambda i, ch: jnp.where(incl, A[i][C:2 * C, 0:2 * C], 0.0))
    a_rk = each(lambda i, ch: jnp.where(incl, A[i][C:2 * C, 2 * C:4 * C], 0.0))
    T = each(lambda i, ch: eye_p - n_kb[i])
    pw = each(lambda i, ch: sp(n_kb[i]))
    for _ in range(n_sq):
        pw = each(lambda i, ch: sp(_mm(pw[i], bd(pw[i], C))))
        T = each(lambda i, ch: T[i] + _mm(sp(T[i]), bd(pw[i], C)))
    v_p = each(lambda i, ch: sp(blk(v, ch)))
    av = each(lambda i, ch: _mm(sp(jnp.concatenate([a_kk[i], a_rk[i]], axis=0)),
                                bd(v_p[i], RW_HEAD)))
    wu = each(lambda i, ch: sp(_mm(
        sp(T[i]), bd(sp(jnp.concatenate([blk(kap_abs, ch), av[i][0:C]], axis=1)), RW_HEAD))))
    qy = each(lambda i, ch: jnp.concatenate([blk(r_abs, ch), av[i][C:2 * C]], axis=1)
              - _mm(sp(a_rb[i]), bd(wu[i], RW_HEAD)))
    bw = each(lambda i, ch: _mm(sp(blk(b_end, ch)), wu[i], _TN))
    kv = each(lambda i, ch: _mm(_split(blk(k_end, ch), 2), _split(blk(v, ch), 2), _TN))
    G = each(lambda i, ch: jnp.where(same_head, kv[i] - bw[i][:, LANES:2 * LANES], 0.0))
    M = each(lambda i, ch: jnp.where(same_head, bw[i][:, 0:LANES], 0.0))
    p_col = each(lambda i, ch: jnp.sum(jnp.where(eye_j, blk(p_end, ch)[0:1], 0.0), axis=1,
                                       keepdims=True))
    h0f = each(lambda i, ch: hst[ch[0], ch[1]])
    h0 = each(lambda i, ch: sp(h0f[i]))
    yc_ = each(lambda i, ch: _mm(sp(qy[i][:, 0:LANES]), h0[i]) + qy[i][:, LANES:2 * LANES])
    for i, ch in enumerate(chains):
        hst[ch[0], ch[1]] = p_col[i] * h0f[i] - _mm(sp(M[i]), h0[i]) + G[i]
    y = jnp.concatenate([jnp.concatenate(yc_[n * N_PAIRS:(n + 1) * N_PAIRS], axis=1)
                         for n in range(nb)], axis=0)

    mean = segsum(y, 1) * (1.0 / RW_HEAD)
    yc = y - mean
    var = segsum(yc * yc, 1) * (1.0 / RW_HEAD)
    yn = yc * lax.rsqrt(var + GN_EPS) * lnw_ref[...] + lnb_ref[...]
    bonus = segsum(r * k2 * rk_ref[...], 1) * v
    out_ref[...] = ((yn + bonus) * g).astype(out_ref.dtype).reshape(nb, C, W)

    @pl.when(c == pl.num_programs(1) - 1)
    def _():
        for n in range(nb):
            for p in range(N_PAIRS):
                blockdiag = hst[n, p].T
                hT_ref[n, 2 * p] = blockdiag[0:RW_HEAD, 0:RW_HEAD]
                hT_ref[n, 2 * p + 1] = blockdiag[RW_HEAD:LANES, RW_HEAD:LANES]


def _rwkv_call(p_rw, shift0, wkv0, prm, C, nb):
    B, T, _ = p_rw.shape
    n_sq = int(math.log2(C)) - 1
    assert 2 ** (n_sq + 1) == C and T % C == 0 and B % nb == 0
    row = lambda a: a.reshape(1, -1).astype(F32)
    vecs = [row(prm['mu_shift']), row(prm['w0']), row(prm['a0']), row(prm['k_k']), row(prm['k_a']),
            row(prm['r_k']), row(prm['lnx_w']), row(prm['lnx_b'])]
    wlora = jnp.zeros((LANES, 2 * RW_WIDTH), F32)
    wlora = wlora.at[0:DECAY_LORA, 0:RW_WIDTH].set(prm['w_decay_up'])
    wlora = wlora.at[DECAY_LORA:, RW_WIDTH:].set(prm['w_a_up']).astype(BF16)
    wg = prm['w_g_up'].astype(BF16)
    const = lambda shape: pl.BlockSpec(shape, lambda b, c: (0,) * len(shape))
    kern = functools.partial(_rwkv_kernel, C=C, n_sq=n_sq, nb=nb)
    return pl.pallas_call(
        kern,
        grid=(B // nb, T // C),
        in_specs=[pl.BlockSpec((nb, C, RW_COLS), lambda b, c: (b, c, 0)),
                  pl.BlockSpec((nb, 1, RW_COLS), lambda b, c: (b, 0, 0)),
                  pl.BlockSpec((nb, RW_HEADS, RW_HEAD, RW_HEAD), lambda b, c: (b, 0, 0, 0)),
                  const((1, RW_COLS))] + [const((1, RW_WIDTH))] * 7
                 + [const((LANES, 2 * RW_WIDTH)), const((GATE_LORA, RW_WIDTH))],
        out_specs=[pl.BlockSpec((nb, C, RW_WIDTH), lambda b, c: (b, c, 0)),
                   pl.BlockSpec((nb, RW_HEADS, RW_HEAD, RW_HEAD), lambda b, c: (b, 0, 0, 0))],
        out_shape=[jax.ShapeDtypeStruct((B, T, RW_WIDTH), BF16),
                   jax.ShapeDtypeStruct((B, RW_HEADS, RW_HEAD, RW_HEAD), F32)],
        scratch_shapes=[pltpu.VMEM((nb, 8 + C, RW_COLS), F32),
                        pltpu.VMEM((nb, N_PAIRS, LANES, LANES), F32)],
        compiler_params=pltpu.CompilerParams(dimension_semantics=("arbitrary", "arbitrary"),
                                             vmem_limit_bytes=VMEM_LIMIT),
        name="rwkv7_chunked",
    )(p_rw, shift0, wkv0, *vecs, wlora, wg)


_INT_MIN = -2 ** 31
_NEG_KEY = int(np.float32(NEG).view(np.int32)) ^ 0x7FFFFFFF
_T5_STEPS = ((91, 14), (64, 13), (46, 12), (32, 11), (23, 10), (16, 9), (12, 8), (8, 7), (7, 6),
             (6, 5), (5, 4), (4, 3), (3, 2), (2, 1), (1, 0))
_T5_FAR = 15
_KEY_TILE = 256


def _half_masked(x, half):
    low = (_lane_iota(x.shape) % LANES) < HEAD_DIM
    keep = low if half == 0 else jnp.logical_not(low)
    return jnp.where(keep, x, jnp.zeros_like(x))


def _dup_half(x, half):
    xr = pltpu.roll(x, HEAD_DIM, axis=1)
    low = _lane_iota(x.shape) < HEAD_DIM
    return jnp.where(low, x, xr) if half == 0 else jnp.where(low, xr, x)


def _dot_nt(a, b):
    return lax.dot_general(a, b, (((1,), (1,)), ((), ())), preferred_element_type=F32)


def _dsa_kernel(*refs, Tq, P, Tn, Lp, topk, n_cls):
    if P > 0:
        (q_ref, qi_ref, wi_ref, hk_ref, hv_ref, hki_ref, nk_ref, nv_ref, nki_ref, rb_ref,
         att_ref, kd, vd, kid, lg, tbl, sbias) = refs
    else:
        (q_ref, qi_ref, wi_ref, nk_ref, nv_ref, nki_ref, rb_ref,
         att_ref, kd, vd, kid, lg, tbl, sbias) = refs
    b = pl.program_id(0)
    i = pl.program_id(1)
    L = P + Tn

    @pl.when(jnp.logical_and(b == 0, i == 0))
    def _():
        shape = (Tq, 2 * LANES)
        rel = _lane_iota(shape) - LANES - _row_iota(shape)
        n = jnp.abs(rel)
        for h in range(N_HEADS):
            far = rb_ref[_T5_FAR, h]
            neg = jnp.full(shape, far, F32)
            pos = jnp.full(shape, rb_ref[_T5_FAR + N_BUCKETS // 2, h], F32)
            for bound, bucket in _T5_STEPS:
                neg = jnp.where(n < bound, rb_ref[bucket, h], neg)
                pos = jnp.where(n < bound, rb_ref[bucket + N_BUCKETS // 2, h], pos)
            tbl[h, :, 0:2 * LANES] = jnp.where(rel > 0, pos, neg) - far
            tbl[h, :, 2 * LANES:3 * LANES] = jnp.zeros((Tq, LANES), F32)

    @pl.when(i == 0)
    def _():
        if P > 0:
            def put(dst, x, c0):
                xb = x.astype(BF16)
                dst[0:HEAD_DIM, c0:c0 + x.shape[1]] = xb
                dst[HEAD_DIM:LANES, c0:c0 + x.shape[1]] = xb

            zt = jnp.zeros((LANES, Lp - P), BF16)
            for j in range(N_KV_HEADS):
                put(kd.at[j], hk_ref[0, j], 0)
                kd[j, :, P:Lp] = zt
                put(kd.at[j], nk_ref[0, j * HEAD_DIM:(j + 1) * HEAD_DIM, :], P)
                for u in range(2):
                    vd[2 * j + u] = jnp.ones((LANES, Lp), BF16)
                    r0 = u * HEAD_DIM
                    vd[2 * j + u, r0:r0 + HEAD_DIM, 0:P] = hv_ref[0, j].astype(BF16)
                    vd[2 * j + u, r0:r0 + HEAD_DIM, P:P + Tn] = (
                        nv_ref[0, j * HEAD_DIM:(j + 1) * HEAD_DIM, :].astype(BF16))
            put(kid, hki_ref[0], 0)
            kid[:, P:Lp] = zt
            put(kid, nki_ref[0], P)
        else:
            kf, vf, kif = nk_ref[0], nv_ref[0], nki_ref[0]
            low = _lane_iota(vf.shape) < HEAD_DIM
            for j in range(N_KV_HEADS):
                kd[j, 0:Tn, :] = _dup_half(kf, j).astype(BF16)
                vdup = _dup_half(vf, j)
                vd[2 * j, 0:Tn, :] = jnp.where(low, vdup, 1.0).astype(BF16)
                vd[2 * j + 1, 0:Tn, :] = jnp.where(low, 1.0, vdup).astype(BF16)
            kid[0:Tn, :] = jnp.concatenate([kif, kif], axis=1).astype(BF16)

    stride = Lp // n_cls
    k_hi = P + (i + 1) * Tq
    cls = (k_hi - 1) // stride
    for c in range(n_cls):
        pl.when(cls == c)(functools.partial(
            _dsa_block, q_ref, qi_ref, wi_ref, att_ref, kd, vd, kid, lg, tbl, sbias,
            i=i, Tq=Tq, P=P, L=L, Lk=(c + 1) * stride, topk=topk))


def _dsa_block(q_ref, qi_ref, wi_ref, att_ref, kd, vd, kid, lg, tbl, sbias, *, i, Tq, P, L, Lk, topk):
    q = q_ref[0]
    qi = qi_ref[0]
    key_major = Tq % LANES == 0
    kax = 0 if key_major else 1
    shape = (Lk, Tq) if key_major else (Tq, Lk)
    qvec = (1, Tq) if key_major else (Tq, 1)
    wi = wi_ref[0]

    lanes_keys = P > 0
    kidv = kid[:, 0:Lk] if lanes_keys else kid[0:Lk, :]
    head_q = lambda h: _half_masked(qi[:, (h // 2) * LANES:(h // 2 + 1) * LANES], h % 2)
    score = jnp.zeros(shape, F32)
    if key_major:
        pair_q = [jnp.concatenate([head_q(2 * pr), head_q(2 * pr + 1)], axis=0)
                  for pr in range(N_IDX_HEADS // 2)]
        tiles = []
        tile = _KEY_TILE if Lk % _KEY_TILE == 0 else LANES
        for kt in range(Lk // tile):
            kid_t = kidv[kt * tile:(kt + 1) * tile]
            acc = jnp.zeros((tile, Tq), F32)
            for pr in range(N_IDX_HEADS // 2):
                s2 = _dot_nt(kid_t, pair_q[pr])
                for u in range(2):
                    h = 2 * pr + u
                    acc = acc + wi[h:h + 1, :] * jnp.maximum(s2[:, u * Tq:(u + 1) * Tq], 0.0)
            tiles.append(acc)
        score = jnp.concatenate(tiles, axis=0)
    else:
        lhs_all = jnp.concatenate([head_q(h) for h in range(N_IDX_HEADS)], axis=0)
        s_all = (jnp.dot(lhs_all, kidv, preferred_element_type=F32) if lanes_keys
                 else _dot_nt(lhs_all, kidv))
        for h in range(N_IDX_HEADS):
            score = score + wi[:, h:h + 1] * jnp.maximum(s_all[h * Tq:(h + 1) * Tq], 0.0)

    s_pos = lax.broadcasted_iota(jnp.int32, shape, kax)
    q_pos = P + i * Tq + lax.broadcasted_iota(jnp.int32, shape, 1 - kax)
    adm = jnp.logical_and(s_pos // CHUNK <= q_pos // CHUNK, s_pos < L)
    score = jnp.where(adm, score, NEG)

    def count(mask):
        ones = jnp.where(mask, 1.0, 0.0)
        if key_major:
            ones = jnp.sum(ones.reshape(Lk // LANES, LANES, Tq), axis=0)
        return jnp.sum(ones, axis=kax, keepdims=True)

    def key_to_float(key):
        return pltpu.bitcast(jnp.where(key < 0, key ^ jnp.int32(0x7FFFFFFF), key), F32)

    kf = float(topk)
    c0 = count(score >= 0.0)
    state = (jnp.where(c0 >= kf, jnp.int32(0), jnp.int32(_INT_MIN)), jnp.where(c0 >= kf, c0, float(Lk)))

    def try_cand(cand, ans, cnt):
        c = count(score >= key_to_float(cand))
        ok = c >= kf
        return jnp.where(ok, cand, ans), jnp.where(ok, c, cnt)

    def bit_step(it, st):
        return try_cand(st[0] | (jnp.int32(1) << (30 - it)), *st)

    def two_bit_step(it, st):
        shift = 28 - 2 * it
        best = st
        for m in (1, 2, 3):
            best = try_cand(st[0] | (jnp.int32(m) << shift), *best)
        return best

    def stacked_q(j):
        return jnp.concatenate(
            [_half_masked(q[:, ((j * GROUP + g) // 2) * LANES:((j * GROUP + g) // 2 + 1) * LANES], g % 2)
             for g in range(GROUP)], axis=0)

    def logits_rows(j, lhs_rows, rsl):
        lg[j, rsl, 0:Lk] = (jnp.dot(lhs_rows, kd[j, :, 0:Lk], preferred_element_type=F32) if lanes_keys
                            else _dot_nt(lhs_rows, kd[j, 0:Lk, :]))

    if key_major:
        thr_key, n_ge = lax.fori_loop(0, 31, bit_step, state)
    else:
        thr_key, n_ge = lax.fori_loop(0, 15, two_bit_step, bit_step(0, state))
    thr_key = jnp.maximum(thr_key, jnp.int32(_NEG_KEY))
    thr = key_to_float(thr_key)
    gt = score > thr
    eq = score == thr
    tied = jnp.logical_and(n_ge > kf, thr > NEG)
    any_tied = jnp.max(jnp.where(tied, 1.0, 0.0)) > 0.0

    n_idx_bits = (Lk - 1).bit_length()

    def idx_search():
        need = kf - count(gt)

        def idx_step(it, x):
            cand = x | (jnp.int32(1) << (n_idx_bits - 1 - it))
            return jnp.where(count(jnp.logical_and(eq, s_pos < cand)) < need, cand, x)
        return lax.fori_loop(0, n_idx_bits, idx_step, jnp.zeros(qvec, jnp.int32))

    last = lax.cond(any_tied, idx_search, lambda: jnp.full(qvec, Lk, jnp.int32))
    sel = jnp.logical_and(jnp.logical_or(gt, jnp.logical_and(eq, s_pos <= last)), adm)
    sb = jnp.where(sel, 0.0, NEG)
    sbias[:, 0:Lk] = sb.T if key_major else sb

    blk_end = (P + i * Tq) // LANES + 1
    start = pl.multiple_of(jnp.maximum(blk_end - 2, 0) * LANES, LANES)
    t_off = pl.multiple_of(jnp.maximum(2 - blk_end, 0) * LANES, LANES)
    for j in range(N_KV_HEADS):
        logits_rows(j, stacked_q(j), slice(0, GROUP * Tq))
    for j in range(N_KV_HEADS):
        for g in range(GROUP):
            r = slice(g * Tq, (g + 1) * Tq)
            if Lk < 2 * LANES:
                lg[j, r, 0:LANES] = lg[j, r, 0:LANES] + tbl[j * GROUP + g, :, LANES:2 * LANES]
            else:
                lg[j, r, pl.ds(start, 2 * LANES)] = (lg[j, r, pl.ds(start, 2 * LANES)]
                                                     + tbl[j * GROUP + g, :, pl.ds(t_off, 2 * LANES)])
    for j in range(N_KV_HEADS):
        sb = sbias[:, 0:Lk]
        ps = []
        for g in range(GROUP):
            lgv = lg[j, g * Tq:(g + 1) * Tq, 0:Lk]
            m = jnp.max(lgv + sb, axis=1, keepdims=True)
            ps.append(jnp.exp((lgv - m) + sb).astype(BF16))
        pv = lambda p, u: (_dot_nt(p, vd[2 * j + u, :, 0:Lk]) if lanes_keys
                           else jnp.dot(p, vd[2 * j + u, 0:Lk, :], preferred_element_type=F32))
        if key_major:
            os_ = [pv(ps[g], g % 2) for g in range(GROUP)]
        else:
            stacked = [pv(jnp.concatenate(ps[u::2], axis=0), u) for u in range(2)]
            os_ = [stacked[g % 2][(g // 2) * Tq:(g // 2 + 1) * Tq] for g in range(GROUP)]
        low = _lane_iota((Tq, LANES)) < HEAD_DIM
        for pp in range(GROUP // 2):
            c0 = (j * GROUP // 2 + pp) * LANES
            o_even, o_odd = os_[2 * pp], os_[2 * pp + 1]
            num = jnp.where(low, o_even, o_odd)
            den = jnp.where(low, pltpu.roll(o_even, HEAD_DIM, axis=1), pltpu.roll(o_odd, HEAD_DIM, axis=1))
            att_ref[0, :, c0:c0 + LANES] = (num / den).astype(att_ref.dtype)


def _dsa_call(q, qi, wi, hist, new, rel_bias, Tq):
    B, T, _ = q.shape
    P = 0 if hist is None else hist[0].shape[-1]
    Tn = new[0].shape[2 if P > 0 else 1]
    L = P + Tn
    Lp = -(-L // LANES) * LANES
    topk = min(MAX_TOPK, L // 4)
    assert T % Tq == 0 and Tn == T and P % LANES == 0 and (Tq % LANES == 0 or T == Tq)
    assert (P == 0 and Lp == L) or (P > 0 and Tq % LANES != 0)
    qspec = lambda w: pl.BlockSpec((1, Tq, w), lambda b, i: (b, i, 0))
    full = lambda *dims: pl.BlockSpec((1,) + dims, lambda b, i: (b,) + (0,) * len(dims))
    if Tq % LANES == 0:
        wi = jnp.swapaxes(wi, 1, 2)
        wi_spec = pl.BlockSpec((1, N_IDX_HEADS, Tq), lambda b, i: (b, 0, i))
    else:
        wi_spec = qspec(N_IDX_HEADS)
    in_specs = [qspec(ATT_WIDTH), qspec(N_IDX_HEADS * IDX_DIM), wi_spec]
    args = [q, qi, wi]
    if P > 0:
        in_specs += [full(N_KV_HEADS, HEAD_DIM, P), full(N_KV_HEADS, HEAD_DIM, P), full(IDX_DIM, P),
                     full(KV_WIDTH, Tn), full(KV_WIDTH, Tn), full(IDX_DIM, Tn)]
        args += list(hist)
        kv_scratch, ki_scratch = (N_KV_HEADS, LANES, Lp), (LANES, Lp)
    else:
        in_specs += [full(Tn, KV_WIDTH), full(Tn, KV_WIDTH), full(Tn, IDX_DIM)]
        kv_scratch, ki_scratch = (N_KV_HEADS, Lp, LANES), (Lp, LANES)
    in_specs.append(pl.BlockSpec(memory_space=pltpu.SMEM))
    args += list(new) + [rel_bias.astype(F32)]
    n_cls = Lp // 256 if (P == 0 and Lp % 256 == 0) else 1
    kern = functools.partial(_dsa_kernel, Tq=Tq, P=P, Tn=Tn, Lp=Lp, topk=topk, n_cls=n_cls)
    return pl.pallas_call(
        kern,
        grid=(B, T // Tq),
        in_specs=in_specs,
        out_specs=pl.BlockSpec((1, Tq, ATT_WIDTH), lambda b, i: (b, i, 0)),
        out_shape=jax.ShapeDtypeStruct((B, T, ATT_WIDTH), BF16),
        scratch_shapes=[pltpu.VMEM(kv_scratch, BF16),
                        pltpu.VMEM((2 * N_KV_HEADS,) + kv_scratch[1:], BF16),
                        pltpu.VMEM(ki_scratch, BF16),
                        pltpu.VMEM((N_KV_HEADS, GROUP * Tq, Lp), F32),
                        pltpu.VMEM((N_HEADS, Tq, 3 * LANES), F32),
                        pltpu.VMEM((Tq, Lp), F32)],
        compiler_params=pltpu.CompilerParams(dimension_semantics=("arbitrary", "arbitrary"),
                                             vmem_limit_bytes=VMEM_LIMIT),
        name="dsa_attention",
    )(*args)


ATT_COLS = 2 * ATT_WIDTH + 2 * KV_WIDTH + LANES
_C_Q, _C_QI, _C_K, _C_V, _C_KI = 0, 512, 1024, 1152, 1280
_C_WI = _C_KI + IDX_DIM


def _rms(x, g):
    return x * lax.rsqrt(jnp.mean(x * x, axis=-1, keepdims=True) + NORM_EPS) * g


def _inproj_kernel(x_ref, g_ref, wa_ref, wr_ref, q_ref, qi_ref, k_ref, v_ref, ki_ref, wi_ref, prw_ref,
                   *t_refs):
    h = _rms(x_ref[...], g_ref[...]).astype(BF16)
    pa = jnp.dot(h, wa_ref[...], preferred_element_type=F32)
    q_ref[...] = (pa[:, _C_Q:_C_Q + ATT_WIDTH] * (HEAD_DIM ** -0.5)).astype(BF16)
    qi_ref[...] = (pa[:, _C_QI:_C_QI + ATT_WIDTH] * (IDX_DIM ** -0.5)).astype(BF16)
    k_ref[...] = pa[:, _C_K:_C_K + KV_WIDTH]
    v_ref[...] = pa[:, _C_V:_C_V + KV_WIDTH]
    ki_ref[...] = pa[:, _C_KI:_C_KI + IDX_DIM]
    wi_ref[...] = pa[:, _C_WI:_C_WI + N_IDX_HEADS] * (N_IDX_HEADS ** -0.5)
    prw_ref[...] = jnp.dot(h, wr_ref[...], preferred_element_type=F32)
    if t_refs:
        kT_ref, vT_ref, kiT_ref = t_refs
        kT_ref[0] = pa[:, _C_K:_C_K + KV_WIDTH].T
        vT_ref[0] = pa[:, _C_V:_C_V + KV_WIDTH].T
        kiT_ref[0] = pa[:, _C_KI:_C_KI + LANES].T[0:IDX_DIM]


def _const_spec(shape):
    return pl.BlockSpec(shape, lambda i: (0,) * len(shape), pipeline_mode=pl.Buffered(1))


def _inproj_call(x2d, g, wa, wr, tm, t_per_batch=None):
    N = x2d.shape[0]
    assert N % tm == 0
    rows = lambda w: pl.BlockSpec((tm, w), lambda i: (i, 0))
    widths = [ATT_WIDTH, ATT_WIDTH, KV_WIDTH, KV_WIDTH, IDX_DIM, N_IDX_HEADS, RW_COLS]
    dtypes = [BF16, BF16, F32, F32, F32, F32, F32]
    out_specs = [rows(w) for w in widths]
    out_shape = [jax.ShapeDtypeStruct((N, w), d) for w, d in zip(widths, dtypes)]
    if t_per_batch is not None:
        assert t_per_batch % tm == 0
        tpb = t_per_batch // tm
        for feat in (KV_WIDTH, KV_WIDTH, IDX_DIM):
            out_specs.append(pl.BlockSpec((1, feat, tm), lambda i: (i // tpb, 0, i % tpb)))
            out_shape.append(jax.ShapeDtypeStruct((N // t_per_batch, feat, t_per_batch), F32))
    return pl.pallas_call(
        _inproj_kernel,
        grid=(N // tm,),
        in_specs=[rows(D_MODEL), _const_spec((1, D_MODEL)), _const_spec((D_MODEL, ATT_COLS)),
                  _const_spec((D_MODEL, RW_COLS))],
        out_specs=out_specs,
        out_shape=out_shape,
        compiler_params=pltpu.CompilerParams(dimension_semantics=("arbitrary",),
                                             vmem_limit_bytes=VMEM_LIMIT),
        name="rmsnorm_inproj",
    )(x2d, g, wa, wr)


def _tail_kernel(x_ref, att_ref, rw_ref, g1_ref, wgate_ref, wao_ref, wro_ref, wout_ref,
                 g2_ref, wff1_ref, wff2_ref, gf_ref, y_ref):
    x = x_ref[...]
    h = _rms(x, g1_ref[...]).astype(BF16)
    gates = jax.nn.sigmoid(jnp.dot(h, wgate_ref[...], preferred_element_type=F32))
    z_a = jnp.dot(att_ref[...], wao_ref[...], preferred_element_type=F32)
    z_b = jnp.dot(rw_ref[...], wro_ref[...], preferred_element_type=F32)
    mixed = gates[:, 0:D_MODEL] * z_a + gates[:, D_MODEL:2 * D_MODEL] * z_b
    x1 = x + jnp.dot(mixed.astype(BF16), wout_ref[...], preferred_element_type=F32)
    h2 = _rms(x1, g2_ref[...]).astype(BF16)
    f = jnp.maximum(jnp.dot(h2, wff1_ref[...], preferred_element_type=F32), 0.0)
    x2 = x1 + jnp.dot((f * f).astype(BF16), wff2_ref[...], preferred_element_type=F32)
    y_ref[...] = _rms(x2, gf_ref[...])


def _tail_call(x2d, att, rw, g1, wgate, wao, wro, wout, g2, wff1, wff2, gf, tm):
    N = x2d.shape[0]
    assert N % tm == 0
    rows = lambda w: pl.BlockSpec((tm, w), lambda i: (i, 0))
    return pl.pallas_call(
        _tail_kernel,
        grid=(N // tm,),
        in_specs=[rows(D_MODEL), rows(ATT_WIDTH), rows(RW_WIDTH), _const_spec((1, D_MODEL)),
                  _const_spec((D_MODEL, 2 * D_MODEL)), _const_spec((ATT_WIDTH, D_MODEL)),
                  _const_spec((RW_WIDTH, D_MODEL)), _const_spec((D_MODEL, D_MODEL)),
                  _const_spec((1, D_MODEL)), _const_spec((D_MODEL, D_FF)), _const_spec((D_FF, D_MODEL)),
                  _const_spec((1, D_MODEL))],
        out_specs=rows(D_MODEL),
        out_shape=jax.ShapeDtypeStruct((N, D_MODEL), F32),
        compiler_params=pltpu.CompilerParams(dimension_semantics=("arbitrary",),
                                             vmem_limit_bytes=VMEM_LIMIT),
        name="merge_out_mlp",
    )(x2d, att, rw, g1, wgate, wao, wro, wout, g2, wff1, wff2, gf)


_IN_SPLITS = (ATT_WIDTH, KV_WIDTH, KV_WIDTH, N_IDX_HEADS * IDX_DIM, IDX_DIM, N_IDX_HEADS,
              RW_COLS, D_MODEL, D_MODEL)


def _split_w_in(w_in):
    offs = [0]
    for s in _IN_SPLITS:
        offs.append(offs[-1] + s)
    col = lambda n: w_in[:, offs[n]:offs[n + 1]]
    pad = jnp.zeros((D_MODEL, ATT_COLS - _C_WI - N_IDX_HEADS), w_in.dtype)
    wa = jnp.concatenate([col(0), col(3), col(1), col(2), col(4), col(5), pad], axis=1).astype(BF16)
    wr = col(6).astype(BF16)
    wgate = jnp.concatenate([col(7), col(8)], axis=1).astype(BF16)
    return wa, wr, wgate


def _branch(x, hist, wkv0, shift0, rel_bias, prm, w, tm_in, tm_tail, Tq, C, nb_rw):
    B, T, _ = x.shape
    x2d = x.reshape(B * T, D_MODEL)
    r3 = lambda a: a.reshape(B, T, a.shape[-1])
    heads_last = lambda aT: jnp.transpose(aT.reshape(B, N_KV_HEADS, HEAD_DIM, T), (0, 3, 1, 2))[None]
    if hist is None:
        q, qi, k, v, ki, wi, p_rw, kT, vT, kiT = _inproj_call(x2d, prm['g1'], w['wa'], w['wr'], tm_in, T)
        new = (r3(k), r3(v), r3(ki))
        new_k, new_v, new_ki = heads_last(kT), heads_last(vT), jnp.swapaxes(kiT, 1, 2)[None]
    else:
        q, qi, k, v, ki, wi, p_rw = _inproj_call(x2d, prm['g1'], w['wa'], w['wr'], tm_in)
        new = tuple(jnp.swapaxes(r3(a), 1, 2) for a in (k, v, ki))
        new_k = k.reshape(1, B, T, N_KV_HEADS, HEAD_DIM)
        new_v = v.reshape(1, B, T, N_KV_HEADS, HEAD_DIM)
        new_ki = ki.reshape(1, B, T, IDX_DIM)
    rw, wkv_new = _rwkv_call(r3(p_rw), shift0, wkv0, prm, C, nb_rw)
    att = _dsa_call(r3(q), r3(qi), r3(wi), hist, new, rel_bias, Tq)
    y = _tail_call(x2d, att.reshape(B * T, ATT_WIDTH), rw.reshape(B * T, RW_WIDTH), prm['g1'],
                   w['wgate'], w['wao'], w['wro'], w['wout'], prm['g2'], w['wff1'], w['wff2'],
                   prm['gf'], tm_tail)
    wkv = wkv_new[None]
    shift = r3(p_rw)[:, T - 1:T, :][None]
    return y.reshape(B, T, D_MODEL), new_k, new_v, new_ki, wkv, shift


def kernel(x_prompt, x_sample, cache_k, cache_v, cache_kidx, state_wkv, state_shift, rel_bias,
           norm1_g, w_in, w_att_out, mu_shift, w0, w_decay_up, a0, w_a_up, w_g_up, k_k, k_a, r_k,
           lnx_w, lnx_b, w_rw_out, w_out, norm2_g, w_ff1, w_ff2, norm_f_g):
    assert w_in.shape[0] == 1, "single-layer step"
    wa, wr, wgate = _split_w_in(w_in[0])
    w = dict(wa=wa, wr=wr, wgate=wgate, wao=w_att_out[0].astype(BF16), wro=w_rw_out[0].astype(BF16),
             wout=w_out[0].astype(BF16), wff1=w_ff1[0].astype(BF16), wff2=w_ff2[0].astype(BF16))
    prm = dict(mu_shift=mu_shift[0], w0=w0[0], w_decay_up=w_decay_up[0], a0=a0[0], w_a_up=w_a_up[0],
               w_g_up=w_g_up[0], k_k=k_k[0], k_a=k_a[0], r_k=r_k[0].reshape(-1), lnx_w=lnx_w[0],
               lnx_b=lnx_b[0], g1=norm1_g[0].reshape(1, -1), g2=norm2_g[0].reshape(1, -1),
               gf=norm_f_g.reshape(1, -1))
    Bp, Tp, _ = x_prompt.shape
    Bs, Ts, _ = x_sample.shape
    past = cache_k.shape[2]
    zero_state = jnp.zeros((Bp, RW_HEADS, RW_HEAD, RW_HEAD), F32)
    zero_shift = jnp.zeros((Bp, 1, RW_COLS), F32)
    outs_p = _branch(x_prompt, None, zero_state, zero_shift, rel_bias, prm, w,
                     tm_in=512, tm_tail=256, Tq=Q_BLOCK, C=CHUNK, nb_rw=4)
    hist = (jnp.transpose(cache_k[0], (0, 2, 3, 1)), jnp.transpose(cache_v[0], (0, 2, 3, 1)),
            jnp.swapaxes(cache_kidx[0], 1, 2))
    outs_s = _branch(x_sample, hist, state_wkv[0], state_shift[0], rel_bias, prm, w,
                     tm_in=Bs * Ts, tm_tail=256, Tq=Ts, C=Ts, nb_rw=4)
    return (outs_p[0], outs_s[0]) + outs_p[1:] + outs_s[1:]
```
